```python
import math
import jax
import jax.numpy as jnp
from jax import lax
import numpy as np

D_MODEL = 1024
BATCH = 8
SEQ = 2048
DEPTH = 4
DEC_BATCH = 32
DEC_SEQ = 32
PAST_LEN = 2048

CHUNK = 64
W_A = D_MODEL // 4
W_B = D_MODEL // 4
W_C = D_MODEL // 4
W_D = D_MODEL // 4
MIX_WIDTH = W_A + W_B + W_C + W_D
POOL_WINDOWS = (2, 4, 8, 16)
N_POOL = 4
POOL_GW = W_A // N_POOL
POOL_HIST = 15
H_B = 4
DH_B = W_B // H_B
BAND_CHUNKS = 8
ATTN_WINDOW = BAND_CHUNKS * CHUNK
REL_CLIP = 256
H_C = 4
DK_C = W_C // H_C
DV_C = W_C // H_C
CONV_W = 4
QKV_C = 3 * W_C
H_D = 4
DK_D = W_D // (2 * H_D)
DV_D = W_D // H_D
WK_D = H_D * DK_D
GLA_RANK = 16
GLA_GATE_NORM = 16.0
D_FF = 4 * D_MODEL
EPS = 1e-6
NEG_INF = -1e30
PROJ_SIZES = (W_A, 3 * W_B, QKV_C, W_C, H_C, H_C, WK_D, WK_D, W_D, W_D, GLA_RANK)
IN_COLS = W_A + 3 * W_B + QKV_C + W_C + 2 * H_C + 2 * WK_D + 2 * W_D + GLA_RANK

kernel_name = 'hybrid_streaming_encoder_step'


def _rmsnorm(x, g):
    xf = x.astype(jnp.float32)
    y = xf * lax.rsqrt(jnp.mean(xf * xf, axis=-1, keepdims=True) + EPS)
    return (y * g.astype(jnp.float32)).astype(x.dtype)


def _l2norm(x):
    return x * lax.rsqrt(jnp.sum(x * x, axis=-1, keepdims=True) + EPS)


def _split_cols(p):
    parts, off = [], 0
    for n in PROJ_SIZES:
        parts.append(p[..., off:off + n])
        off += n
    return parts


def _rel_bias(table, rel):
    idx = jnp.clip(rel, -REL_CLIP, REL_CLIP) + REL_CLIP
    return table.astype(jnp.float32)[:, idx]


def _to_chunks(t, cs):
    b, n = t.shape[0], t.shape[1] // cs
    t = t.reshape((b, n, cs) + t.shape[2:])
    return jnp.transpose(t, (1, 0, 3, 2) + tuple(range(4, t.ndim)))


def _from_chunks(t):
    n, b, h, cs = t.shape[:4]
    t = jnp.transpose(t, (1, 0, 3, 2) + tuple(range(4, t.ndim)))
    return t.reshape((b, n * cs, h) + t.shape[4:])


def _pool_mixer(u, hist, pos0, w_pool, scale):
    b, t = u.shape[:2]
    ext = jnp.concatenate([hist.astype(u.dtype), u], axis=1)
    ef = ext.astype(jnp.float32)
    c0 = jnp.concatenate([jnp.zeros((b, 1, W_A), jnp.float32), jnp.cumsum(ef, axis=1)], axis=1)
    pos = pos0 + jnp.arange(t)
    groups = []
    for gi, win in enumerate(POOL_WINDOWS):
        lo, hi = gi * POOL_GW, (gi + 1) * POOL_GW
        wsum = (c0[:, POOL_HIST + 1:POOL_HIST + 1 + t, lo:hi]
                - c0[:, POOL_HIST + 1 - win:POOL_HIST + 1 - win + t, lo:hi])
        cnt = jnp.minimum(win, pos + 1).astype(jnp.float32)[None, :, None]
        groups.append(wsum / cnt - ef[:, POOL_HIST:, lo:hi])
    pooled = jnp.stack(groups, axis=2)
    y = jnp.einsum('btgc,gcd->btgd', pooled, w_pool.astype(jnp.float32)).reshape(b, t, W_A)
    return (y * scale.astype(jnp.float32)).astype(u.dtype), ext[:, -POOL_HIST:]


def _band_attention_prompt(q, k, v, table):
    b, t = q.shape[:2]
    nc = t // CHUNK
    nb = BAND_CHUNKS + 1
    qc, kc, vc = (a.astype(jnp.float32).reshape(b, nc, CHUNK, H_B, DH_B) for a in (q, k, v))
    pad = ((0, 0), (BAND_CHUNKS, 0), (0, 0), (0, 0), (0, 0))
    kp, vp = jnp.pad(kc, pad), jnp.pad(vc, pad)
    kband = jnp.concatenate([kp[:, j:j + nc] for j in range(nb)], axis=2)
    vband = jnp.concatenate([vp[:, j:j + nc] for j in range(nb)], axis=2)
    qi = jnp.arange(CHUNK)
    kj = jnp.arange(nb * CHUNK)
    bias = _rel_bias(table, kj[None, :] - BAND_CHUNKS * CHUNK - qi[:, None])
    kpos = (jnp.arange(nc)[:, None] - BAND_CHUNKS) * CHUNK + kj[None, :]
    s = jnp.einsum('bnqhd,bnkhd->bnhqk', qc, kband) * (DH_B ** -0.5) + bias[None, None]
    s = jnp.where((kpos >= 0)[None, :, None, None, :], s, NEG_INF)
    p = jax.nn.softmax(s, axis=-1)
    o = jnp.einsum('bnhqk,bnkhd->bnqhd', p, vband)
    return o.reshape(b, t, W_B).astype(q.dtype)


def _attention_sample(q, k, v, k_cache, v_cache, table):
    b, t = q.shape[:2]
    n_cache = k_cache.shape[2]
    kall = jnp.concatenate([k_cache.astype(jnp.float32), jnp.transpose(k, (0, 2, 1, 3)).astype(jnp.float32)], axis=2)
    vall = jnp.concatenate([v_cache.astype(jnp.float32), jnp.transpose(v, (0, 2, 1, 3)).astype(jnp.float32)], axis=2)
    bias = _rel_bias(table, jnp.arange(n_cache + t)[None, :] - n_cache - jnp.arange(t)[:, None])
    s = jnp.einsum('bqhd,bhkd->bhqk', q.astype(jnp.float32), kall) * (DH_B ** -0.5) + bias[None]
    p = jax.nn.softmax(s, axis=-1)
    o = jnp.einsum('bhqk,bhkd->bqhd', p, vall)
    return o.reshape(b, t, W_B).astype(q.dtype)


def _gated_delta_chunked(q, k, v, g, beta, s0):
    t = q.shape[1]
    cs = min(CHUNK, t)
    xs = tuple(_to_chunks(a, cs) for a in (q, k, v, g, beta))
    incl = jnp.tril(jnp.ones((cs, cs), dtype=bool))
    strict = jnp.tril(jnp.ones((cs, cs), dtype=bool), -1)
    eye = jnp.eye(cs, dtype=jnp.float32)

    def step(S, inp):
        qc, kc, vc, gc, bc = inp
        dv = vc.shape[-1]
        G = jnp.cumsum(gc, axis=-1)
        diff = G[..., :, None] - G[..., None, :]
        dec = jnp.where(incl, jnp.exp(jnp.where(incl, diff, 0.0)), 0.0)
        A = jnp.where(strict, bc[..., :, None] * jnp.einsum('bhid,bhjd->bhij', kc, kc) * dec, 0.0)
        rhs = jnp.concatenate([bc[..., None] * vc, (bc * jnp.exp(G))[..., None] * kc], axis=-1)
        sol = lax.linalg.triangular_solve(A + eye, rhs, left_side=True, lower=True, unit_diagonal=True)
        u, w = sol[..., :dv], sol[..., dv:]
        un = u - jnp.einsum('bhik,bhkv->bhiv', w, S)
        qk = jnp.einsum('bhik,bhjk->bhij', qc, kc) * dec
        o = (jnp.einsum('bhik,bhkv->bhiv', qc * jnp.exp(G)[..., None], S)
             + jnp.einsum('bhij,bhjv->bhiv', qk, un))
        gl = G[..., -1:]
        S = (jnp.exp(gl)[..., None] * S
             + jnp.einsum('bhjk,bhjv->bhkv', kc * jnp.exp(gl - G)[..., None], un))
        return S, o

    s_fin, o = lax.scan(step, s0, xs)
    return _from_chunks(o), s_fin


def _gla_chunked(q, k, v, gk, s0):
    t = q.shape[1]
    cs = min(CHUNK, t)
    xs = tuple(_to_chunks(a, cs) for a in (q, k, v, gk))
    incl = jnp.tril(jnp.ones((cs, cs), dtype=bool))[:, :, None]

    def step(S, inp):
        qc, kc, vc, gc = inp
        G = jnp.cumsum(gc, axis=2)
        diff = G[:, :, :, None, :] - G[:, :, None, :, :]
        dec = jnp.where(incl, jnp.exp(jnp.where(incl, diff, 0.0)), 0.0)
        att = jnp.einsum('bhik,bhjk,bhijk->bhij', qc, kc, dec)
        o = (jnp.einsum('bhik,bhkv->bhiv', qc * jnp.exp(G), S)
             + jnp.einsum('bhij,bhjv->bhiv', att, vc))
        gl = G[:, :, -1:, :]
        S = (jnp.exp(gl[:, :, 0, :])[..., None] * S
             + jnp.einsum('bhjk,bhjv->bhkv', kc * jnp.exp(gl - G), vc))
        return S, o

    s_fin, o = lax.scan(step, s0, xs)
    return _from_chunks(o), s_fin


def _delta_mixer(qkv, z, a, bb, conv_hist, s0, conv_w, a_log, dt_bias, norm_g):
    b, t = qkv.shape[:2]
    ext = jnp.concatenate([conv_hist.astype(qkv.dtype), qkv], axis=1)
    acc = ext[:, 0:t] * conv_w[0]
    for j in range(1, CONV_W):
        acc = acc + ext[:, j:j + t] * conv_w[j]
    c = jax.nn.silu(acc.astype(jnp.float32))
    q = _l2norm(c[..., :W_C].reshape(b, t, H_C, DK_C)) * (DK_C ** -0.5)
    k = _l2norm(c[..., W_C:2 * W_C].reshape(b, t, H_C, DK_C))
    v = c[..., 2 * W_C:].reshape(b, t, H_C, DV_C)
    beta = jax.nn.sigmoid(bb.astype(jnp.float32))
    g = -jnp.exp(a_log.astype(jnp.float32)) * jax.nn.softplus(a.astype(jnp.float32) + dt_bias.astype(jnp.float32))
    o, s_new = _gated_delta_chunked(q, k, v, g, beta, s0.astype(jnp.float32))
    o = _rmsnorm(o, norm_g) * jax.nn.silu(z.astype(jnp.float32).reshape(b, t, H_C, DV_C))
    return o.reshape(b, t, W_C).astype(qkv.dtype), ext[:, -(CONV_W - 1):], s_new


def _gla_mixer(q, k, v, gate, gk_low, s0, w_gk, b_gk, norm_g):
    b, t = q.shape[:2]
    gk = jax.nn.log_sigmoid(jnp.einsum('btr,rk->btk', gk_low.astype(jnp.float32), w_gk.astype(jnp.float32))
                            + b_gk.astype(jnp.float32)) / GLA_GATE_NORM
    qh = q.astype(jnp.float32).reshape(b, t, H_D, DK_D) * (DK_D ** -0.5)
    kh = k.astype(jnp.float32).reshape(b, t, H_D, DK_D)
    vh = v.astype(jnp.float32).reshape(b, t, H_D, DV_D)
    o, s_new = _gla_chunked(qh, kh, vh, gk.reshape(b, t, H_D, DK_D), s0.astype(jnp.float32))
    o = _rmsnorm(o, norm_g) * jax.nn.silu(gate.astype(jnp.float32).reshape(b, t, H_D, DV_D))
    return o.reshape(b, t, W_D).astype(q.dtype), s_new


def _trunk(x, pos0, pool_h, kv_h, conv_h, sd_h, sg_h, w):
    (attn_norm_g, w_in, pool_w, pool_scale, rel_bias, conv_w, a_log, dt_bias, delta_norm_g,
     gla_w_gk, gla_b_gk, gla_norm_g, w_out, mlp_norm_g, w_up, w_down, final_norm_g) = w
    b, t = x.shape[:2]
    new_pool, new_k, new_v, new_conv, new_sd, new_sg = [], [], [], [], [], []
    for l in range(DEPTH):
        h = _rmsnorm(x, attn_norm_g[l])
        (u_a, qkv_b, qkv_c, z_c, a_c, b_c, q_d, k_d, v_d, g_d, gk_d) = _split_cols(
            jnp.einsum('btd,dc->btc', h, w_in[l]))
        y_a, st_pool = _pool_mixer(u_a, pool_h[l], pos0, pool_w[l], pool_scale[l])
        q_b, k_b, v_b = (qkv_b[..., i * W_B:(i + 1) * W_B].reshape(b, t, H_B, DH_B) for i in range(3))
        if kv_h is None:
            y_b = _band_attention_prompt(q_b, k_b, v_b, rel_bias[l])
            keep = min(ATTN_WINDOW, t)
            k_rows, v_rows = k_b[:, t - keep:], v_b[:, t - keep:]
        else:
            y_b = _attention_sample(q_b, k_b, v_b, kv_h[0][l], kv_h[1][l], rel_bias[l])
            k_rows, v_rows = k_b, v_b
        y_c, st_conv, st_d = _delta_mixer(qkv_c, z_c, a_c, b_c, conv_h[l], sd_h[l], conv_w[l],
                                          a_log[l], dt_bias[l], delta_norm_g[l])
        y_d, st_g = _gla_mixer(q_d, k_d, v_d, g_d, gk_d, sg_h[l], gla_w_gk[l], gla_b_gk[l], gla_norm_g[l])
        mix = jnp.concatenate([y_a, y_b, y_c, y_d], axis=-1)
        x = x + jnp.einsum('btc,cd->btd', mix, w_out[l])
        h2 = _rmsnorm(x, mlp_norm_g[l])
        up = jnp.square(jax.nn.relu(jnp.einsum('btd,df->btf', h2, w_up[l])))
        x = x + jnp.einsum('btf,fd->btd', up, w_down[l])
        new_pool.append(st_pool.astype(x.dtype))
        new_k.append(jnp.transpose(k_rows, (0, 2, 1, 3)))
        new_v.append(jnp.transpose(v_rows, (0, 2, 1, 3)))
        new_conv.append(st_conv.astype(x.dtype))
        new_sd.append(st_d.astype(x.dtype))
        new_sg.append(st_g.astype(x.dtype))
    y = _rmsnorm(x, final_norm_g)
    return (y, jnp.stack(new_pool), jnp.stack(new_k), jnp.stack(new_v),
            jnp.stack(new_conv), jnp.stack(new_sd), jnp.stack(new_sg))


def setup_inputs(seed: int = 0) -> dict:
    key = jax.random.key(seed)
    ks = jax.random.split(key, 32)
    f32 = jnp.float32

    def nrm(i, shape, scale):
        return scale * jax.random.normal(ks[i], shape, f32)

    l_b = min(ATTN_WINDOW, PAST_LEN)
    dt = jnp.exp(jax.random.uniform(ks[20], (DEPTH, H_C), f32, math.log(1e-3), math.log(1e-1)))
    return {
        'x_prompt': nrm(0, (BATCH, SEQ, D_MODEL), 1.0),
        'x_sample': nrm(1, (DEC_BATCH, DEC_SEQ, D_MODEL), 1.0),
        'cache_pool': nrm(2, (DEPTH, DEC_BATCH, POOL_HIST, W_A), 1.0),
        'cache_attn_k': nrm(3, (DEPTH, DEC_BATCH, H_B, l_b, DH_B), 1.0),
        'cache_attn_v': nrm(4, (DEPTH, DEC_BATCH, H_B, l_b, DH_B), 1.0),
        'state_conv': nrm(5, (DEPTH, DEC_BATCH, CONV_W - 1, QKV_C), 1.0),
        'state_delta': nrm(6, (DEPTH, DEC_BATCH, H_C, DK_C, DV_C), DK_C ** -0.5),
        'state_gla': nrm(7, (DEPTH, DEC_BATCH, H_D, DK_D, DV_D), 1.0),
        'attn_norm_g': 1.0 + nrm(8, (DEPTH, D_MODEL), 0.05),
        'w_in': nrm(9, (DEPTH, D_MODEL, IN_COLS), D_MODEL ** -0.5),
        'pool_w': nrm(10, (DEPTH, N_POOL, POOL_GW, POOL_GW), POOL_GW ** -0.5),
        'pool_scale': 1.0 + nrm(11, (DEPTH, W_A), 0.1),
        'rel_bias': nrm(12, (DEPTH, H_B, 2 * REL_CLIP + 1), 0.5),
        'conv_w': nrm(13, (DEPTH, CONV_W, QKV_C), CONV_W ** -0.5),
        'a_log': jnp.log(jax.random.uniform(ks[14], (DEPTH, H_C), f32, 1.0, 16.0)),
        'dt_bias': dt + jnp.log(-jnp.expm1(-dt)),
        'delta_norm_g': 1.0 + nrm(15, (DEPTH, DV_C), 0.05),
        'gla_w_gk': nrm(16, (DEPTH, GLA_RANK, WK_D), GLA_RANK ** -0.5),
        'gla_b_gk': nrm(17, (DEPTH, WK_D), 0.1),
        'gla_norm_g': 1.0 + nrm(18, (DEPTH, DV_D), 0.05),
        'w_out': nrm(19, (DEPTH, MIX_WIDTH, D_MODEL), MIX_WIDTH ** -0.5),
        'mlp_norm_g': 1.0 + nrm(21, (DEPTH, D_MODEL), 0.05),
        'w_up': nrm(22, (DEPTH, D_MODEL, D_FF), D_MODEL ** -0.5),
        'w_down': nrm(23, (DEPTH, D_FF, D_MODEL), 0.5 * D_FF ** -0.5),
        'final_norm_g': 1.0 + nrm(24, (D_MODEL,), 0.05),
    }


def reference(x_prompt, x_sample, cache_pool, cache_attn_k, cache_attn_v, state_conv, state_delta,
              state_gla, attn_norm_g, w_in, pool_w, pool_scale, rel_bias, conv_w, a_log, dt_bias,
              delta_norm_g, gla_w_gk, gla_b_gk, gla_norm_g, w_out, mlp_norm_g, w_up, w_down,
              final_norm_g):
    w = (attn_norm_g, w_in, pool_w, pool_scale, rel_bias, conv_w, a_log, dt_bias, delta_norm_g,
         gla_w_gk, gla_b_gk, gla_norm_g, w_out, mlp_norm_g, w_up, w_down, final_norm_g)
    bp = x_prompt.shape[0]
    dtp = x_prompt.dtype
    y_prompt, pool_p, k_p, v_p, conv_p, delta_p, gla_p = _trunk(
        x_prompt, 0,
        jnp.zeros((DEPTH, bp, POOL_HIST, W_A), dtp), None,
        jnp.zeros((DEPTH, bp, CONV_W - 1, QKV_C), dtp),
        jnp.zeros((DEPTH, bp, H_C, DK_C, DV_C), dtp),
        jnp.zeros((DEPTH, bp, H_D, DK_D, DV_D), dtp), w)
    y_sample, pool_s, k_s, v_s, conv_s, delta_s, gla_s = _trunk(
        x_sample, PAST_LEN, cache_pool, (cache_attn_k, cache_attn_v), state_conv,
        state_delta, state_gla, w)
    return (y_prompt, y_sample, pool_p, k_p, v_p, conv_p, delta_p, gla_p,
            pool_s, k_s, v_s, conv_s, delta_s, gla_s)
```

```python
import functools
import math

import jax
import jax.numpy as jnp
from jax import lax
from jax.experimental import pallas as pl
from jax.experimental.pallas import tpu as pltpu

D_MODEL = 1024
DEPTH = 4
CHUNK = 64
W_A = W_B = W_C = W_D = 256
POOL_WINDOWS = (2, 4, 8, 16)
N_POOL = 4
POOL_GW = 64
POOL_HIST = 15
H_B = 4
DH_B = 64
BAND_CHUNKS = 8
ATTN_WINDOW = 512
REL_CLIP = 256
H_C = 4
DK_C = 64
DV_C = 64
CONV_W = 4
QKV_C = 768
H_D = 4
DK_D = 32
DV_D = 64
WK_D = 128
GLA_RANK = 16
GLA_GATE_NORM = 16.0
D_FF = 4096
EPS = 1e-6
NEG_INF = -1e30

P_UA = 0
P_QB, P_KB, P_VB = 256, 512, 768
P_QKVC = 1024
P_ZC = 1792
P_VD = 2048
P_GD = 2304
P_QD = 2560
P_KD = 2688
P_SMALL = 2816
P_COLS = 2944

ROW_TILE = 512
VMEM_LIMIT = 56 * 1024 * 1024

f32 = jnp.float32
bf16 = jnp.bfloat16


def _reorder_w_in(w_in):
    pad = jnp.zeros(w_in.shape[:-1] + (P_COLS - P_SMALL - 24,), w_in.dtype)
    return jnp.concatenate([
        w_in[..., 0:2048],
        w_in[..., 2312:2568],
        w_in[..., 2568:2824],
        w_in[..., 2056:2184],
        w_in[..., 2184:2312],
        w_in[..., 2048:2056],
        w_in[..., 2824:2840],
        pad], axis=-1)


def _inproj_kernel(x_ref, g_ref, w_ref, o_ref):
    x = x_ref[...]
    ms = jnp.mean(x * x, axis=-1, keepdims=True)
    h = x * lax.rsqrt(ms + EPS) * g_ref[...]
    o_ref[...] = jnp.dot(h.astype(bf16), w_ref[...], preferred_element_type=f32)


def _inproj(x, g, w):
    n = x.shape[0]
    return pl.pallas_call(
        _inproj_kernel,
        out_shape=jax.ShapeDtypeStruct((n, P_COLS), f32),
        grid=(n // ROW_TILE,),
        in_specs=[
            pl.BlockSpec((ROW_TILE, D_MODEL), lambda i: (i, 0)),
            pl.BlockSpec((1, D_MODEL), lambda i: (0, 0)),
            pl.BlockSpec((D_MODEL, P_COLS), lambda i: (0, 0), pipeline_mode=pl.Buffered(1)),
        ],
        out_specs=pl.BlockSpec((ROW_TILE, P_COLS), lambda i: (i, 0)),
        compiler_params=pltpu.CompilerParams(
            dimension_semantics=("arbitrary",), vmem_limit_bytes=VMEM_LIMIT),
        name="inproj",
    )(x, g, w)


FF_CHUNK = 1024


def _outmlp_kernel(x_ref, ya_ref, yb_ref, yc_ref, yd_ref, wo_ref, g2_ref, wu_ref, wd_ref, gf_ref,
                   o_ref, *, final):
    acc = None
    for k, y_ref in enumerate((ya_ref, yb_ref, yc_ref, yd_ref)):
        d = jnp.dot(y_ref[...].astype(bf16), wo_ref[k * 256:(k + 1) * 256, :],
                    preferred_element_type=f32)
        acc = d if acc is None else acc + d
    x = x_ref[...] + acc
    ms = jnp.mean(x * x, axis=-1, keepdims=True)
    h2 = (x * lax.rsqrt(ms + EPS) * g2_ref[...]).astype(bf16)
    acc = None
    for c in range(D_FF // FF_CHUNK):
        up = jnp.dot(h2, wu_ref[:, c * FF_CHUNK:(c + 1) * FF_CHUNK], preferred_element_type=f32)
        up = jnp.square(jnp.maximum(up, 0.0)).astype(bf16)
        d = jnp.dot(up, wd_ref[c * FF_CHUNK:(c + 1) * FF_CHUNK, :], preferred_element_type=f32)
        acc = d if acc is None else acc + d
    x = x + acc
    if final:
        ms = jnp.mean(x * x, axis=-1, keepdims=True)
        x = x * lax.rsqrt(ms + EPS) * gf_ref[...]
    o_ref[...] = x


def _outmlp(x, ys, wo, g2, wu, wd, gf, final):
    n = x.shape[0]
    const = lambda i: (0, 0)
    row = lambda i: (i, 0)
    single = pl.Buffered(1)
    return pl.pallas_call(
        functools.partial(_outmlp_kernel, final=final),
        out_shape=jax.ShapeDtypeStruct((n, D_MODEL), f32),
        grid=(n // ROW_TILE,),
        in_specs=[pl.BlockSpec((ROW_TILE, D_MODEL), row)]
        + [pl.BlockSpec((ROW_TILE, 256), row)] * 4
        + [pl.BlockSpec((D_MODEL, D_MODEL), const, pipeline_mode=single),
           pl.BlockSpec((1, D_MODEL), const),
           pl.BlockSpec((D_MODEL, D_FF), const, pipeline_mode=single),
           pl.BlockSpec((D_FF, D_MODEL), const, pipeline_mode=single),
           pl.BlockSpec((1, D_MODEL), const)],
        out_specs=pl.BlockSpec((ROW_TILE, D_MODEL), row),
        compiler_params=pltpu.CompilerParams(
            dimension_semantics=("arbitrary",), vmem_limit_bytes=VMEM_LIMIT),
        name="outmlp_final" if final else "outmlp",
    )(x, *ys, wo, g2, wu, wd, gf)


def _rmsnorm(x, g):
    xf = x.astype(f32)
    y = xf * lax.rsqrt(jnp.mean(xf * xf, axis=-1, keepdims=True) + EPS)
    return (y * g.astype(f32)).astype(x.dtype)


def _l2norm(x):
    return x * lax.rsqrt(jnp.sum(x * x, axis=-1, keepdims=True) + EPS)


def _rel_bias(table, rel):
    idx = jnp.clip(rel, -REL_CLIP, REL_CLIP) + REL_CLIP
    return table.astype(f32)[:, idx]


def _to_chunks(t, cs):
    b, n = t.shape[0], t.shape[1] // cs
    t = t.reshape((b, n, cs) + t.shape[2:])
    return jnp.transpose(t, (1, 0, 3, 2) + tuple(range(4, t.ndim)))


def _from_chunks(t):
    n, b, h, cs = t.shape[:4]
    t = jnp.transpose(t, (1, 0, 3, 2) + tuple(range(4, t.ndim)))
    return t.reshape((b, n * cs, h) + t.shape[4:])


def _pool_mixer(u, hist, pos0, w_pool, scale):
    b, t = u.shape[:2]
    ext = jnp.concatenate([hist.astype(u.dtype), u], axis=1)
    ef = ext.astype(f32)
    c0 = jnp.concatenate([jnp.zeros((b, 1, W_A), f32), jnp.cumsum(ef, axis=1)], axis=1)
    pos = pos0 + jnp.arange(t)
    groups = []
    for gi, win in enumerate(POOL_WINDOWS):
        lo, hi = gi * POOL_GW, (gi + 1) * POOL_GW
        wsum = (c0[:, POOL_HIST + 1:POOL_HIST + 1 + t, lo:hi]
                - c0[:, POOL_HIST + 1 - win:POOL_HIST + 1 - win + t, lo:hi])
        cnt = jnp.minimum(win, pos + 1).astype(f32)[None, :, None]
        groups.append(wsum / cnt - ef[:, POOL_HIST:, lo:hi])
    pooled = jnp.stack(groups, axis=2)
    y = jnp.einsum('btgc,gcd->btgd', pooled, w_pool.astype(f32)).reshape(b, t, W_A)
    return (y * scale.astype(f32)).astype(u.dtype), ext[:, -POOL_HIST:]


def _band_attention_prompt(q, k, v, table):
    b, t = q.shape[:2]
    nc = t // CHUNK
    nb = BAND_CHUNKS + 1
    qc, kc, vc = (a.astype(f32).reshape(b, nc, CHUNK, H_B, DH_B) for a in (q, k, v))
    pad = ((0, 0), (BAND_CHUNKS, 0), (0, 0), (0, 0), (0, 0))
    kp, vp = jnp.pad(kc, pad), jnp.pad(vc, pad)
    kband = jnp.concatenate([kp[:, j:j + nc] for j in range(nb)], axis=2)
    vband = jnp.concatenate([vp[:, j:j + nc] for j in range(nb)], axis=2)
    qi = jnp.arange(CHUNK)
    kj = jnp.arange(nb * CHUNK)
    bias = _rel_bias(table, kj[None, :] - BAND_CHUNKS * CHUNK - qi[:, None])
    kpos = (jnp.arange(nc)[:, None] - BAND_CHUNKS) * CHUNK + kj[None, :]
    s = jnp.einsum('bnqhd,bnkhd->bnhqk', qc, kband) * (DH_B ** -0.5) + bias[None, None]
    s = jnp.where((kpos >= 0)[None, :, None, None, :], s, NEG_INF)
    p = jax.nn.softmax(s, axis=-1)
    o = jnp.einsum('bnhqk,bnkhd->bnqhd', p, vband)
    return o.reshape(b, t, W_B).astype(q.dtype)


def _attention_sample(q, k, v, k_cache, v_cache, table):
    b, t = q.shape[:2]
    n_cache = k_cache.shape[2]
    kall = jnp.concatenate([k_cache.astype(f32), jnp.transpose(k, (0, 2, 1, 3)).astype(f32)], axis=2)
    vall = jnp.concatenate([v_cache.astype(f32), jnp.transpose(v, (0, 2, 1, 3)).astype(f32)], axis=2)
    bias = _rel_bias(table, jnp.arange(n_cache + t)[None, :] - n_cache - jnp.arange(t)[:, None])
    s = jnp.einsum('bqhd,bhkd->bhqk', q.astype(f32), kall) * (DH_B ** -0.5) + bias[None]
    p = jax.nn.softmax(s, axis=-1)
    o = jnp.einsum('bhqk,bhkd->bqhd', p, vall)
    return o.reshape(b, t, W_B).astype(q.dtype)


def _gated_delta_chunked(q, k, v, g, beta, s0):
    t = q.shape[1]
    cs = min(CHUNK, t)
    xs = tuple(_to_chunks(a, cs) for a in (q, k, v, g, beta))
    incl = jnp.tril(jnp.ones((cs, cs), dtype=bool))
    strict = jnp.tril(jnp.ones((cs, cs), dtype=bool), -1)
    eye = jnp.eye(cs, dtype=f32)

    def step(S, inp):
        qc, kc, vc, gc, bc = inp
        dv = vc.shape[-1]
        G = jnp.cumsum(gc, axis=-1)
        diff = G[..., :, None] - G[..., None, :]
        dec = jnp.where(incl, jnp.exp(jnp.where(incl, diff, 0.0)), 0.0)
        A = jnp.where(strict, bc[..., :, None] * jnp.einsum('bhid,bhjd->bhij', kc, kc) * dec, 0.0)
        rhs = jnp.concatenate([bc[..., None] * vc, (bc * jnp.exp(G))[..., None] * kc], axis=-1)
        sol = lax.linalg.triangular_solve(A + eye, rhs, left_side=True, lower=True, unit_diagonal=True)
        u, w = sol[..., :dv], sol[..., dv:]
        un = u - jnp.einsum('bhik,bhkv->bhiv', w, S)
        qk = jnp.einsum('bhik,bhjk->bhij', qc, kc) * dec
        o = (jnp.einsum('bhik,bhkv->bhiv', qc * jnp.exp(G)[..., None], S)
             + jnp.einsum('bhij,bhjv->bhiv', qk, un))
        gl = G[..., -1:]
        S = (jnp.exp(gl)[..., None] * S
             + jnp.einsum('bhjk,bhjv->bhkv', kc * jnp.exp(gl - G)[..., None], un))
        return S, o

    s_fin, o = lax.scan(step, s0, xs)
    return _from_chunks(o), s_fin


def _gla_chunked(q, k, v, gk, s0):
    t = q.shape[1]
    cs = min(CHUNK, t)
    xs = tuple(_to_chunks(a, cs) for a in (q, k, v, gk))
    incl = jnp.tril(jnp.ones((cs, cs), dtype=bool))[:, :, None]

    def step(S, inp):
        qc, kc, vc, gc = inp
        G = jnp.cumsum(gc, axis=2)
        diff = G[:, :, :, None, :] - G[:, :, None, :, :]
        dec = jnp.where(incl, jnp.exp(jnp.where(incl, diff, 0.0)), 0.0)
        att = jnp.einsum('bhik,bhjk,bhijk->bhij', qc, kc, dec)
        o = (jnp.einsum('bhik,bhkv->bhiv', qc * jnp.exp(G), S)
             + jnp.einsum('bhij,bhjv->bhiv', att, vc))
        gl = G[:, :, -1:, :]
        S = (jnp.exp(gl[:, :, 0, :])[..., None] * S
             + jnp.einsum('bhjk,bhjv->bhkv', kc * jnp.exp(gl - G), vc))
        return S, o

    s_fin, o = lax.scan(step, s0, xs)
    return _from_chunks(o), s_fin


def _delta_mixer(qkv, z, a, bb, conv_hist, s0, conv_w, a_log, dt_bias, norm_g):
    b, t = qkv.shape[:2]
    ext = jnp.concatenate([conv_hist.astype(qkv.dtype), qkv], axis=1)
    acc = ext[:, 0:t] * conv_w[0]
    for j in range(1, CONV_W):
        acc = acc + ext[:, j:j + t] * conv_w[j]
    c = jax.nn.silu(acc.astype(f32))
    q = _l2norm(c[..., :W_C].reshape(b, t, H_C, DK_C)) * (DK_C ** -0.5)
    k = _l2norm(c[..., W_C:2 * W_C].reshape(b, t, H_C, DK_C))
    v = c[..., 2 * W_C:].reshape(b, t, H_C, DV_C)
    beta = jax.nn.sigmoid(bb.astype(f32))
    g = -jnp.exp(a_log.astype(f32)) * jax.nn.softplus(a.astype(f32) + dt_bias.astype(f32))
    o, s_new = _gated_delta_chunked(q, k, v, g, beta, s0.astype(f32))
    o = _rmsnorm(o, norm_g) * jax.nn.silu(z.astype(f32).reshape(b, t, H_C, DV_C))
    return o.reshape(b, t, W_C).astype(qkv.dtype), ext[:, -(CONV_W - 1):], s_new


def _gla_mixer(q, k, v, gate, gk_low, s0, w_gk, b_gk, norm_g):
    b, t = q.shape[:2]
    gk = jax.nn.log_sigmoid(jnp.einsum('btr,rk->btk', gk_low.astype(f32), w_gk.astype(f32))
                            + b_gk.astype(f32)) / GLA_GATE_NORM
    qh = q.astype(f32).reshape(b, t, H_D, DK_D) * (DK_D ** -0.5)
    kh = k.astype(f32).reshape(b, t, H_D, DK_D)
    vh = v.astype(f32).reshape(b, t, H_D, DV_D)
    o, s_new = _gla_chunked(qh, kh, vh, gk.reshape(b, t, H_D, DK_D), s0.astype(f32))
    o = _rmsnorm(o, norm_g) * jax.nn.silu(gate.astype(f32).reshape(b, t, H_D, DV_D))
    return o.reshape(b, t, W_D).astype(q.dtype), s_new


def _mixers_jax(p, b, t, pos0, pool_h, kv_h, conv_h, sd_h, sg_h, lw):
    (pool_w, pool_scale, rel_bias, conv_w, a_log, dt_bias, delta_norm_g,
     gla_w_gk, gla_b_gk, gla_norm_g) = lw
    p = p.reshape(b, t, P_COLS)
    u_a = p[..., P_UA:P_UA + 256]
    y_a, st_pool = _pool_mixer(u_a, pool_h, pos0, pool_w, pool_scale)
    q_b, k_b, v_b = (p[..., o:o + 256].reshape(b, t, H_B, DH_B) for o in (P_QB, P_KB, P_VB))
    if kv_h is None:
        y_b = _band_attention_prompt(q_b, k_b, v_b, rel_bias)
        keep = min(ATTN_WINDOW, t)
        k_rows, v_rows = k_b[:, t - keep:], v_b[:, t - keep:]
    else:
        y_b = _attention_sample(q_b, k_b, v_b, kv_h[0], kv_h[1], rel_bias)
        k_rows, v_rows = k_b, v_b
    small = p[..., P_SMALL:P_SMALL + 24]
    y_c, st_conv, st_d = _delta_mixer(p[..., P_QKVC:P_QKVC + 768], p[..., P_ZC:P_ZC + 256],
                                      small[..., 0:4], small[..., 4:8], conv_h, sd_h, conv_w,
                                      a_log, dt_bias, delta_norm_g)
    y_d, st_g = _gla_mixer(p[..., P_QD:P_QD + 128], p[..., P_KD:P_KD + 128], p[..., P_VD:P_VD + 256],
                           p[..., P_GD:P_GD + 256], small[..., 8:24], sg_h, gla_w_gk, gla_b_gk,
                           gla_norm_g)
    n = b * t
    ys = tuple(y.reshape(n, 256) for y in (y_a, y_b, y_c, y_d))
    return ys, (st_pool, jnp.transpose(k_rows, (0, 2, 1, 3)), jnp.transpose(v_rows, (0, 2, 1, 3)),
                st_conv, st_d, st_g)


def kernel(x_prompt, x_sample, cache_pool, cache_attn_k, cache_attn_v, state_conv, state_delta,
           state_gla, attn_norm_g, w_in, pool_w, pool_scale, rel_bias, conv_w, a_log, dt_bias,
           delta_norm_g, gla_w_gk, gla_b_gk, gla_norm_g, w_out, mlp_norm_g, w_up, w_down,
           final_norm_g):
    bp, tp, _ = x_prompt.shape
    bs, ts, _ = x_sample.shape
    n_p, n_s = bp * tp, bs * ts
    x = jnp.concatenate([x_prompt.reshape(n_p, D_MODEL), x_sample.reshape(n_s, D_MODEL)], axis=0)
    w_in_r = _reorder_w_in(w_in).astype(bf16)
    w_out_b, w_up_b, w_down_b = w_out.astype(bf16), w_up.astype(bf16), w_down.astype(bf16)
    gf = final_norm_g.reshape(1, D_MODEL)
    zeros_p = (jnp.zeros((bp, POOL_HIST, W_A), f32), jnp.zeros((bp, CONV_W - 1, QKV_C), f32),
               jnp.zeros((bp, H_C, DK_C, DV_C), f32), jnp.zeros((bp, H_D, DK_D, DV_D), f32))
    st_p, st_s = [], []
    for l in range(DEPTH):
        p = _inproj(x, attn_norm_g[l].reshape(1, D_MODEL), w_in_r[l])
        lw = (pool_w[l], pool_scale[l], rel_bias[l], conv_w[l], a_log[l], dt_bias[l], delta_norm_g[l],
              gla_w_gk[l], gla_b_gk[l], gla_norm_g[l])
        ys_p, s_p = _mixers_jax(p[:n_p], bp, tp, 0, zeros_p[0], None, zeros_p[1], zeros_p[2],
                                zeros_p[3], lw)
        ys_s, s_s = _mixers_jax(p[n_p:], bs, ts, tp, cache_pool[l],
                                (cache_attn_k[l], cache_attn_v[l]), state_conv[l], state_delta[l],
                                state_gla[l], lw)
        ys = tuple(jnp.concatenate([a, c], axis=0) for a, c in zip(ys_p, ys_s))
        x = _outmlp(x, ys, w_out_b[l], mlp_norm_g[l].reshape(1, D_MODEL), w_up_b[l], w_down_b[l], gf,
                    final=(l == DEPTH - 1))
        st_p.append(s_p)
        st_s.append(s_s)
    y_prompt = x[:n_p].reshape(bp, tp, D_MODEL)
    y_sample = x[n_p:].reshape(bs, ts, D_MODEL)
    outs_p = tuple(jnp.stack([s[i] for s in st_p]) for i in range(6))
    outs_s = tuple(jnp.stack([s[i] for s in st_s]) for i in range(6))
    return (y_prompt, y_sample) + outs_p + outs_s
```

```python
import functools
import math

import jax
import jax.numpy as jnp
from jax import lax
from jax.experimental import pallas as pl
from jax.experimental.pallas import tpu as pltpu

D_MODEL = 1024
DEPTH = 4
CHUNK = 64
W_A = W_B = W_C = W_D = 256
POOL_WINDOWS = (2, 4, 8, 16)
N_POOL = 4
POOL_GW = 64
POOL_HIST = 15
H_B = 4
DH_B = 64
BAND_CHUNKS = 8
ATTN_WINDOW = 512
REL_CLIP = 256
H_C = 4
DK_C = 64
DV_C = 64
CONV_W = 4
QKV_C = 768
H_D = 4
DK_D = 32
DV_D = 64
WK_D = 128
GLA_RANK = 16
GLA_GATE_NORM = 16.0
D_FF = 4096
EPS = 1e-6
NEG_INF = -1e30

P_UA = 0
P_QB, P_KB, P_VB = 256, 512, 768
P_QKVC = 1024
P_ZC = 1792
P_VD = 2048
P_GD = 2304
P_QD = 2560
P_KD = 2688
P_SMALL = 2816
P_COLS = 2944

ROW_TILE = 512
VMEM_LIMIT = 56 * 1024 * 1024

f32 = jnp.float32
bf16 = jnp.bfloat16


def _reorder_w_in(w_in):
    pad = jnp.zeros(w_in.shape[:-1] + (P_COLS - P_SMALL - 24,), w_in.dtype)
    return jnp.concatenate([
        w_in[..., 0:2048],
        w_in[..., 2312:2568],
        w_in[..., 2568:2824],
        w_in[..., 2056:2184],
        w_in[..., 2184:2312],
        w_in[..., 2048:2056],
        w_in[..., 2824:2840],
        pad], axis=-1)


def _inproj_kernel(x_ref, g_ref, w_ref, o_ref):
    x = x_ref[...]
    ms = jnp.mean(x * x, axis=-1, keepdims=True)
    h = x * lax.rsqrt(ms + EPS) * g_ref[...]
    o_ref[...] = jnp.dot(h.astype(bf16), w_ref[...], preferred_element_type=f32)


def _inproj(x, g, w):
    n = x.shape[0]
    return pl.pallas_call(
        _inproj_kernel,
        out_shape=jax.ShapeDtypeStruct((n, P_COLS), f32),
        grid=(n // ROW_TILE,),
        in_specs=[
            pl.BlockSpec((ROW_TILE, D_MODEL), lambda i: (i, 0)),
            pl.BlockSpec((1, D_MODEL), lambda i: (0, 0)),
            pl.BlockSpec((D_MODEL, P_COLS), lambda i: (0, 0), pipeline_mode=pl.Buffered(1)),
        ],
        out_specs=pl.BlockSpec((ROW_TILE, P_COLS), lambda i: (i, 0)),
        compiler_params=pltpu.CompilerParams(
            dimension_semantics=("arbitrary",), vmem_limit_bytes=VMEM_LIMIT),
        name="inproj",
    )(x, g, w)


FF_CHUNK = 1024


def _outmlp_kernel(x_ref, ya_ref, yb_ref, yc_ref, yd_ref, wo_ref, g2_ref, wu_ref, wd_ref, gf_ref,
                   o_ref, *, final):
    acc = None
    for k, y_ref in enumerate((ya_ref, yb_ref, yc_ref, yd_ref)):
        d = jnp.dot(y_ref[...].astype(bf16), wo_ref[k * 256:(k + 1) * 256, :],
                    preferred_element_type=f32)
        acc = d if acc is None else acc + d
    x = x_ref[...] + acc
    ms = jnp.mean(x * x, axis=-1, keepdims=True)
    h2 = (x * lax.rsqrt(ms + EPS) * g2_ref[...]).astype(bf16)
    acc = None
    for c in range(D_FF // FF_CHUNK):
        up = jnp.dot(h2, wu_ref[:, c * FF_CHUNK:(c + 1) * FF_CHUNK], preferred_element_type=f32)
        up = jnp.square(jnp.maximum(up, 0.0)).astype(bf16)
        d = jnp.dot(up, wd_ref[c * FF_CHUNK:(c + 1) * FF_CHUNK, :], preferred_element_type=f32)
        acc = d if acc is None else acc + d
    x = x + acc
    if final:
        ms = jnp.mean(x * x, axis=-1, keepdims=True)
        x = x * lax.rsqrt(ms + EPS) * gf_ref[...]
    o_ref[...] = x


def _outmlp(x, ys, wo, g2, wu, wd, gf, final):
    n = x.shape[0]
    const = lambda i: (0, 0)
    row = lambda i: (i, 0)
    single = pl.Buffered(1)
    return pl.pallas_call(
        functools.partial(_outmlp_kernel, final=final),
        out_shape=jax.ShapeDtypeStruct((n, D_MODEL), f32),
        grid=(n // ROW_TILE,),
        in_specs=[pl.BlockSpec((ROW_TILE, D_MODEL), row)]
        + [pl.BlockSpec((ROW_TILE, 256), row)] * 4
        + [pl.BlockSpec((D_MODEL, D_MODEL), const, pipeline_mode=single),
           pl.BlockSpec((1, D_MODEL), const),
           pl.BlockSpec((D_MODEL, D_FF), const, pipeline_mode=single),
           pl.BlockSpec((D_FF, D_MODEL), const, pipeline_mode=single),
           pl.BlockSpec((1, D_MODEL), const)],
        out_specs=pl.BlockSpec((ROW_TILE, D_MODEL), row),
        compiler_params=pltpu.CompilerParams(
            dimension_semantics=("arbitrary",), vmem_limit_bytes=VMEM_LIMIT),
        name="outmlp_final" if final else "outmlp",
    )(x, *ys, wo, g2, wu, wd, gf)


HI = lax.Precision.HIGHEST


def _dot(a, b, prec=None):
    return jnp.dot(a, b, preferred_element_type=f32, precision=prec)


def _dot_nt(a, b, prec=None):
    return lax.dot_general(a, b, (((1,), (1,)), ((), ())), preferred_element_type=f32, precision=prec)


def _dot_tn(a, b, prec=None):
    return lax.dot_general(a, b, (((0,), (0,)), ((), ())), preferred_element_type=f32, precision=prec)


def _iota2(shape, dim):
    return lax.broadcasted_iota(jnp.int32, shape, dim)


def _seg_ones(n, seg):
    return (_iota2((n, n), 0) // seg == _iota2((n, n), 1) // seg).astype(f32)


def _silu(x):
    return x * jax.nn.sigmoid(x)


def _unit_lower_inverse(a, c):
    ri, ci = _iota2((c, c), 0), _iota2((c, c), 1)
    d = jnp.where(ri == ci, 1.0, 0.0) - jnp.where((ri // 2 == ci // 2) & (ri > ci), a, 0.0)
    s = 2
    while s < c:
        e = jnp.where((ri // (2 * s) == ci // (2 * s)) & (ri % (2 * s) >= s) & (ci % (2 * s) < s), a, 0.0)
        d = d - _dot(d, _dot(e, d))
        s *= 2
    return d


def _delta_kernel(q_ref, k_ref, v_ref, z_ref, sm_ref, hist_ref, s0_ref, cw_ref, alog_ref, dtb_ref, ng_ref,
                  y_ref, sout_ref, ext_ref, qn_ref, kn_ref, vv_ref, g_ref, bt_ref, o_ref, s_ref,
                  *, tb, c):
    t_idx = pl.program_id(1)

    @pl.when(t_idx == 0)
    def _():
        s_ref[...] = s0_ref[0]
        for s in range(3):
            ext_ref[s, 0:8, :] = jnp.zeros((8, 256), f32)
            ext_ref[s, 5:8, :] = hist_ref[0, :, s * 256:(s + 1) * 256]

    seg = _seg_ones(256, DK_C)
    parts = []
    for s, ref in enumerate((q_ref, k_ref, v_ref)):
        ext_ref[s, 8:8 + tb, :] = ref[...]
        acc = ext_ref[s, 5:5 + tb, :] * cw_ref[0:1, s * 256:(s + 1) * 256]
        for j in range(1, CONV_W):
            acc = acc + ext_ref[s, 5 + j:5 + j + tb, :] * cw_ref[j:j + 1, s * 256:(s + 1) * 256]
        parts.append(_silu(acc))
        ext_ref[s, 5:8, :] = ext_ref[s, tb + 5:tb + 8, :]
    cq, ck, cv = parts
    qn_ref[...] = cq * lax.rsqrt(_dot(cq * cq, seg, HI) + EPS) * (DK_C ** -0.5)
    kn_ref[...] = ck * lax.rsqrt(_dot(ck * ck, seg, HI) + EPS)
    vv_ref[...] = cv
    sm = sm_ref[...]
    g_ref[...] = -jnp.exp(alog_ref[...]) * jax.nn.softplus(sm + dtb_ref[...])
    bt_ref[...] = jax.nn.sigmoid(sm)

    ri, ci = _iota2((c, c), 0), _iota2((c, c), 1)
    incl = ri >= ci
    strict = ri > ci
    ltri = incl.astype(f32)

    def chunk(ic, carry):
        r0 = pl.multiple_of(ic * c, c)
        rows = pl.ds(r0, c)
        gcum = _dot(ltri, g_ref[rows, :], HI)
        gcum_t = gcum.T
        beta = bt_ref[rows, :]
        for h in range(H_C):
            lanes = slice(h * 64, (h + 1) * 64)
            qh, kh, vh = qn_ref[rows, lanes], kn_ref[rows, lanes], vv_ref[rows, lanes]
            gc = gcum[:, h:h + 1]
            gr = gcum_t[h:h + 1, :]
            dec = jnp.where(incl, jnp.exp(jnp.where(incl, gc - gr, 0.0)), 0.0)
            bc = beta[:, 4 + h:5 + h]
            eg = jnp.exp(gc)
            a = jnp.where(strict, bc * _dot_nt(kh, kh) * dec, 0.0)
            rhs = jnp.concatenate([bc * vh, (bc * eg) * kh], axis=1)
            sol = _dot(_unit_lower_inverse(a, c), rhs)
            u, w = sol[:, :64], sol[:, 64:]
            st = s_ref[h]
            un = u - _dot(w, st)
            qk = _dot_nt(qh, kh) * dec
            o_ref[rows, lanes] = _dot(qh * eg, st) + _dot(qk, un)
            gl = gcum[c - 1:c, h:h + 1]
            s_ref[h] = jnp.exp(gl) * st + _dot_tn(kh * jnp.exp(gl - gc), un)
        return carry

    lax.fori_loop(0, tb // c, chunk, 0)

    o = o_ref[...]
    o = o * lax.rsqrt(_dot(o * o, seg, HI) * (1.0 / DV_C) + EPS) * ng_ref[...]
    y_ref[...] = o * _silu(z_ref[...])

    @pl.when(t_idx == pl.num_programs(1) - 1)
    def _():
        sout_ref[0] = s_ref[...]


def _delta_call(p, row0, b, t, hist, s0, conv_w, a_log, dt_bias, norm_g):
    tb = min(t, ROW_TILE)
    c = min(CHUNK, t)
    nt = t // tb
    rb0 = row0 // tb

    def col(cb):
        return lambda i, j: (rb0 + i * nt + j, cb)

    pad = jnp.zeros((1, 128 - H_C), f32)
    alog = jnp.concatenate([a_log.reshape(1, H_C), pad], axis=1)
    dtb = jnp.concatenate([dt_bias.reshape(1, H_C), pad], axis=1)
    ng = jnp.tile(norm_g.reshape(1, DV_C), (1, H_C))
    const = lambda i, j: (0, 0)
    y, s_new = pl.pallas_call(
        functools.partial(_delta_kernel, tb=tb, c=c),
        out_shape=(jax.ShapeDtypeStruct((b * t, 256), f32),
                   jax.ShapeDtypeStruct((b, H_C, DK_C, DV_C), f32)),
        grid=(b, nt),
        in_specs=[
            pl.BlockSpec((tb, 256), col(P_QKVC // 256)),
            pl.BlockSpec((tb, 256), col(P_QKVC // 256 + 1)),
            pl.BlockSpec((tb, 256), col(P_QKVC // 256 + 2)),
            pl.BlockSpec((tb, 256), col(P_ZC // 256)),
            pl.BlockSpec((tb, 128), col(P_SMALL // 128)),
            pl.BlockSpec((1, CONV_W - 1, QKV_C), lambda i, j: (i, 0, 0)),
            pl.BlockSpec((1, H_C, DK_C, DV_C), lambda i, j: (i, 0, 0, 0)),
            pl.BlockSpec((CONV_W, QKV_C), const),
            pl.BlockSpec((1, 128), const),
            pl.BlockSpec((1, 128), const),
            pl.BlockSpec((1, 256), const),
        ],
        out_specs=(pl.BlockSpec((tb, 256), lambda i, j: (i * nt + j, 0)),
                   pl.BlockSpec((1, H_C, DK_C, DV_C), lambda i, j: (i, 0, 0, 0))),
        scratch_shapes=[
            pltpu.VMEM((3, tb + 8, 256), f32),
            pltpu.VMEM((tb, 256), f32), pltpu.VMEM((tb, 256), f32), pltpu.VMEM((tb, 256), f32),
            pltpu.VMEM((tb, 128), f32), pltpu.VMEM((tb, 128), f32),
            pltpu.VMEM((tb, 256), f32),
            pltpu.VMEM((H_C, DK_C, DV_C), f32),
        ],
        compiler_params=pltpu.CompilerParams(
            dimension_semantics=("arbitrary", "arbitrary"), vmem_limit_bytes=VMEM_LIMIT),
        name=f"delta_t{t}",
    )(p, p, p, p, p, hist, s0, conv_w, alog, dtb, ng)
    return y, s_new


def _gla_kernel(q_ref, k_ref, v_ref, gate_ref, sm_ref, s0_ref, wgk_ref, bgk_ref, ng_ref,
                y_ref, sout_ref, qs_ref, g_ref, o_ref, s_ref, *, tb, c):
    t_idx = pl.program_id(1)
    blk = _iota2((WK_D, W_D), 0) // DK_D == _iota2((WK_D, W_D), 1) // DV_D
    seg_kv = blk.astype(f32)

    @pl.when(t_idx == 0)
    def _():
        s_ref[...] = jnp.zeros((WK_D, W_D), f32)
        for h in range(H_D):
            s_ref[h * DK_D:(h + 1) * DK_D, h * DV_D:(h + 1) * DV_D] = s0_ref[0, h]

    gk = jax.nn.log_sigmoid(_dot(sm_ref[...], wgk_ref[...], HI) + bgk_ref[...]) * (1.0 / GLA_GATE_NORM)
    g_ref[...] = gk
    qs_ref[...] = q_ref[...] * (DK_D ** -0.5)
    ltri = (_iota2((c, c), 0) >= _iota2((c, c), 1)).astype(f32)
    rowid = _iota2((c, WK_D), 0)

    def chunk(ic, carry):
        rows = pl.ds(pl.multiple_of(ic * c, c), c)
        g = _dot(ltri, g_ref[rows, :], HI)
        q, k, v = qs_ref[rows, :], k_ref[rows, :], v_ref[rows, :]
        st = s_ref[...]
        o = _dot(q * jnp.exp(g), st)
        for jb in range(c // 8):
            lo = 8 * jb
            m = c - lo
            e = []
            for j in range(lo, lo + 8):
                ej = jnp.exp(g[lo:] - g[j:j + 1]) * (q[lo:] * k[j:j + 1])
                e.append(jnp.where(rowid[lo:] >= j, ej, 0.0))
            att = _dot(jnp.concatenate(e, axis=0), seg_kv)
            upd = att[0:m] * v[lo:lo + 1]
            for jj in range(1, 8):
                upd = upd + att[jj * m:(jj + 1) * m] * v[lo + jj:lo + jj + 1]
            o = o + upd if lo == 0 else jnp.concatenate([o[:lo], o[lo:] + upd], axis=0)
        o_ref[rows, :] = o
        gl = g[c - 1:c]
        egl_col = jnp.exp(g.T[:, c - 1:c])
        s_new = egl_col * st + _dot_tn(k * jnp.exp(gl - g), v)
        s_ref[...] = jnp.where(blk, s_new, 0.0)
        return carry

    lax.fori_loop(0, tb // c, chunk, 0)

    o = o_ref[...]
    o = o * lax.rsqrt(_dot(o * o, _seg_ones(W_D, DV_D), HI) * (1.0 / DV_D) + EPS) * ng_ref[...]
    y_ref[...] = o * _silu(gate_ref[...])

    @pl.when(t_idx == pl.num_programs(1) - 1)
    def _():
        for h in range(H_D):
            sout_ref[0, h] = s_ref[h * DK_D:(h + 1) * DK_D, h * DV_D:(h + 1) * DV_D]


def _gla_call(p, row0, b, t, s0, w_gk, b_gk, norm_g):
    tb = min(t, ROW_TILE)
    c = min(CHUNK, t)
    nt = t // tb
    rb0 = row0 // tb

    def col(cb):
        return lambda i, j: (rb0 + i * nt + j, cb)

    wgk = jnp.zeros((128, WK_D), f32).at[8:8 + GLA_RANK].set(w_gk)
    ng = jnp.tile(norm_g.reshape(1, DV_D), (1, H_D))
    const = lambda i, j: (0, 0)
    y, s_new = pl.pallas_call(
        functools.partial(_gla_kernel, tb=tb, c=c),
        out_shape=(jax.ShapeDtypeStruct((b * t, W_D), f32),
                   jax.ShapeDtypeStruct((b, H_D, DK_D, DV_D), f32)),
        grid=(b, nt),
        in_specs=[
            pl.BlockSpec((tb, WK_D), col(P_QD // WK_D)),
            pl.BlockSpec((tb, WK_D), col(P_KD // WK_D)),
            pl.BlockSpec((tb, W_D), col(P_VD // W_D)),
            pl.BlockSpec((tb, W_D), col(P_GD // W_D)),
            pl.BlockSpec((tb, 128), col(P_SMALL // 128)),
            pl.BlockSpec((1, H_D, DK_D, DV_D), lambda i, j: (i, 0, 0, 0)),
            pl.BlockSpec((128, WK_D), const),
            pl.BlockSpec((1, WK_D), const),
            pl.BlockSpec((1, W_D), const),
        ],
        out_specs=(pl.BlockSpec((tb, W_D), lambda i, j: (i * nt + j, 0)),
                   pl.BlockSpec((1, H_D, DK_D, DV_D), lambda i, j: (i, 0, 0, 0))),
        scratch_shapes=[
            pltpu.VMEM((tb, WK_D), f32), pltpu.VMEM((tb, WK_D), f32),
            pltpu.VMEM((tb, W_D), f32),
            pltpu.VMEM((WK_D, W_D), f32),
        ],
        compiler_params=pltpu.CompilerParams(
            dimension_semantics=("arbitrary", "arbitrary"), vmem_limit_bytes=VMEM_LIMIT),
        name=f"gla_t{t}",
    )(p, p, p, p, p, s0, wgk, b_gk.reshape(1, WK_D), ng)
    return y, s_new


def _pool_kernel(u_ref, hist_ref, w_ref, scale_ref, y_ref, ext_ref, *, tb, pos0):
    t_idx = pl.program_id(1)

    @pl.when(t_idx == 0)
    def _():
        ext_ref[0:8, :] = jnp.zeros((8, W_A), f32)
        ext_ref[1:16, :] = hist_ref[0]

    x = u_ref[...]
    ext_ref[16:16 + tb, :] = x
    wsum = {}
    acc = x
    for k in range(1, 16):
        acc = acc + ext_ref[16 - k:16 - k + tb, :]
        if k + 1 in POOL_WINDOWS:
            wsum[k + 1] = acc
    ext_ref[1:16, :] = ext_ref[tb + 1:tb + 16, :]
    lane_group = _iota2((tb, W_A), 1) // POOL_GW
    pos = pos0 + t_idx * tb + _iota2((tb, W_A), 0)
    ws, win = wsum[POOL_WINDOWS[-1]], jnp.full((tb, W_A), POOL_WINDOWS[-1], jnp.int32)
    for gi in range(N_POOL - 2, -1, -1):
        ws = jnp.where(lane_group == gi, wsum[POOL_WINDOWS[gi]], ws)
        win = jnp.where(lane_group == gi, POOL_WINDOWS[gi], win)
    cnt = jnp.minimum(win, pos + 1).astype(f32)
    pooled = ws / cnt - x
    y_ref[...] = _dot(pooled, w_ref[...]) * scale_ref[...]


def _pool_call(p, row0, b, t, pos0, hist, pool_w, pool_scale):
    tb = min(t, ROW_TILE)
    nt = t // tb
    rb0 = row0 // tb
    wbd = jnp.zeros((W_A, W_A), f32)
    for gi in range(N_POOL):
        wbd = wbd.at[gi * POOL_GW:(gi + 1) * POOL_GW, gi * POOL_GW:(gi + 1) * POOL_GW].set(pool_w[gi])
    const = lambda i, j: (0, 0)
    return pl.pallas_call(
        functools.partial(_pool_kernel, tb=tb, pos0=pos0),
        out_shape=jax.ShapeDtypeStruct((b * t, W_A), f32),
        grid=(b, nt),
        in_specs=[
            pl.BlockSpec((tb, W_A), lambda i, j: (rb0 + i * nt + j, P_UA // W_A)),
            pl.BlockSpec((1, POOL_HIST, W_A), lambda i, j: (i, 0, 0)),
            pl.BlockSpec((W_A, W_A), const),
            pl.BlockSpec((1, W_A), const),
        ],
        out_specs=pl.BlockSpec((tb, W_A), lambda i, j: (i * nt + j, 0)),
        scratch_shapes=[pltpu.VMEM((tb + 16, W_A), f32)],
        compiler_params=pltpu.CompilerParams(
            dimension_semantics=("arbitrary", "arbitrary"), vmem_limit_bytes=VMEM_LIMIT),
        name=f"pool_t{t}",
    )(p, hist, wbd, pool_scale.reshape(1, W_A))


Q_BLOCK = 256
K_WINDOW = Q_BLOCK + BAND_CHUNKS * CHUNK


def _softmax_av(s_parts, v_parts):
    m = functools.reduce(jnp.maximum, [jnp.max(s, axis=-1, keepdims=True) for s in s_parts])
    e_parts = [jnp.exp(s - m) for s in s_parts]
    l = functools.reduce(jnp.add, [jnp.sum(e, axis=-1, keepdims=True) for e in e_parts])
    o = functools.reduce(jnp.add, [_dot(e, v) for e, v in zip(e_parts, v_parts)])
    return o / l


def _band_attn_kernel(q_ref, k_ref, v_ref, bias_ref, y_ref):
    qb = pl.program_id(1)
    start = pl.multiple_of(jnp.maximum(qb * Q_BLOCK - BAND_CHUNKS * CHUNK, 0), Q_BLOCK)
    outs = []
    for h in range(H_B):
        lanes = slice(h * DH_B, (h + 1) * DH_B)
        q = q_ref[:, lanes] * (DH_B ** -0.5)
        k = k_ref[pl.ds(start, K_WINDOW), lanes]
        v = v_ref[pl.ds(start, K_WINDOW), lanes]
        s = _dot_nt(q, k) + bias_ref[0, h]
        outs.append(_softmax_av([s], [v]))
    y_ref[...] = jnp.concatenate(outs, axis=1)


def _band_bias(table):
    variants = []
    qi = jnp.arange(Q_BLOCK)[:, None]
    kj = jnp.arange(K_WINDOW)[None, :]
    for v in range(3):
        rel = kj - (qi + v * Q_BLOCK)
        dchunk = (qi + v * Q_BLOCK) // CHUNK - kj // CHUNK
        vis = (dchunk >= 0) & (dchunk <= BAND_CHUNKS)
        bias = _rel_bias(table, rel)
        variants.append(jnp.where(vis[None], bias, NEG_INF))
    return jnp.stack(variants)


def _band_attn_call(p, b, t, rel_table):
    nq = t // Q_BLOCK
    bias = _band_bias(rel_table)
    return pl.pallas_call(
        _band_attn_kernel,
        out_shape=jax.ShapeDtypeStruct((b * t, W_B), f32),
        grid=(b, nq),
        in_specs=[
            pl.BlockSpec((Q_BLOCK, W_B), lambda i, j: (i * nq + j, P_QB // W_B)),
            pl.BlockSpec((t, W_B), lambda i, j: (i, P_KB // W_B)),
            pl.BlockSpec((t, W_B), lambda i, j: (i, P_VB // W_B)),
            pl.BlockSpec((1, H_B, Q_BLOCK, K_WINDOW), lambda i, j: (jnp.minimum(j, 2), 0, 0, 0)),
        ],
        out_specs=pl.BlockSpec((Q_BLOCK, W_B), lambda i, j: (i * nq + j, 0)),
        compiler_params=pltpu.CompilerParams(
            dimension_semantics=("arbitrary", "arbitrary"), vmem_limit_bytes=VMEM_LIMIT),
        name="band_attn",
    )(p, p, p, bias)


def _sample_attn_kernel(q_ref, k_ref, v_ref, kc_ref, vc_ref, bias_ref, y_ref, *, n_cache):
    outs = []
    for h in range(H_B):
        lanes = slice(h * DH_B, (h + 1) * DH_B)
        q = q_ref[:, lanes] * (DH_B ** -0.5)
        s_old = _dot_nt(q, kc_ref[0, h]) + bias_ref[h, :, 0:n_cache]
        s_new = _dot_nt(q, k_ref[:, lanes]) + bias_ref[h, :, n_cache:]
        outs.append(_softmax_av([s_old, s_new], [vc_ref[0, h], v_ref[:, lanes]]))
    y_ref[...] = jnp.concatenate(outs, axis=1)


def _sample_attn_call(p, row0, b, t, k_cache, v_cache, rel_table):
    n_cache = k_cache.shape[2]
    rb0 = row0 // t
    bias = _rel_bias(rel_table, jnp.arange(n_cache + t)[None, :] - n_cache - jnp.arange(t)[:, None])
    cache_spec = pl.BlockSpec((1, H_B, n_cache, DH_B), lambda i: (i, 0, 0, 0))
    return pl.pallas_call(
        functools.partial(_sample_attn_kernel, n_cache=n_cache),
        out_shape=jax.ShapeDtypeStruct((b * t, W_B), f32),
        grid=(b,),
        in_specs=[
            pl.BlockSpec((t, W_B), lambda i: (rb0 + i, P_QB // W_B)),
            pl.BlockSpec((t, W_B), lambda i: (rb0 + i, P_KB // W_B)),
            pl.BlockSpec((t, W_B), lambda i: (rb0 + i, P_VB // W_B)),
            cache_spec, cache_spec,
            pl.BlockSpec((H_B, t, n_cache + t), lambda i: (0, 0, 0)),
        ],
        out_specs=pl.BlockSpec((t, W_B), lambda i: (i, 0)),
        compiler_params=pltpu.CompilerParams(
            dimension_semantics=("arbitrary",), vmem_limit_bytes=VMEM_LIMIT),
        name="sample_attn",
    )(p, p, p, k_cache, v_cache, bias)


def _rmsnorm(x, g):
    xf = x.astype(f32)
    y = xf * lax.rsqrt(jnp.mean(xf * xf, axis=-1, keepdims=True) + EPS)
    return (y * g.astype(f32)).astype(x.dtype)


def _l2norm(x):
    return x * lax.rsqrt(jnp.sum(x * x, axis=-1, keepdims=True) + EPS)


def _rel_bias(table, rel):
    idx = jnp.clip(rel, -REL_CLIP, REL_CLIP) + REL_CLIP
    return table.astype(f32)[:, idx]


def _to_chunks(t, cs):
    b, n = t.shape[0], t.shape[1] // cs
    t = t.reshape((b, n, cs) + t.shape[2:])
    return jnp.transpose(t, (1, 0, 3, 2) + tuple(range(4, t.ndim)))


def _from_chunks(t):
    n, b, h, cs = t.shape[:4]
    t = jnp.transpose(t, (1, 0, 3, 2) + tuple(range(4, t.ndim)))
    return t.reshape((b, n * cs, h) + t.shape[4:])


def _pool_mixer(u, hist, pos0, w_pool, scale):
    b, t = u.shape[:2]
    ext = jnp.concatenate([hist.astype(u.dtype), u], axis=1)
    ef = ext.astype(f32)
    c0 = jnp.concatenate([jnp.zeros((b, 1, W_A), f32), jnp.cumsum(ef, axis=1)], axis=1)
    pos = pos0 + jnp.arange(t)
    groups = []
    for gi, win in enumerate(POOL_WINDOWS):
        lo, hi = gi * POOL_GW, (gi + 1) * POOL_GW
        wsum = (c0[:, POOL_HIST + 1:POOL_HIST + 1 + t, lo:hi]
                - c0[:, POOL_HIST + 1 - win:POOL_HIST + 1 - win + t, lo:hi])
        cnt = jnp.minimum(win, pos + 1).astype(f32)[None, :, None]
        groups.append(wsum / cnt - ef[:, POOL_HIST:, lo:hi])
    pooled = jnp.stack(groups, axis=2)
    y = jnp.einsum('btgc,gcd->btgd', pooled, w_pool.astype(f32)).reshape(b, t, W_A)
    return (y * scale.astype(f32)).astype(u.dtype), ext[:, -POOL_HIST:]


def _band_attention_prompt(q, k, v, table):
    b, t = q.shape[:2]
    nc = t // CHUNK
    nb = BAND_CHUNKS + 1
    qc, kc, vc = (a.astype(f32).reshape(b, nc, CHUNK, H_B, DH_B) for a in (q, k, v))
    pad = ((0, 0), (BAND_CHUNKS, 0), (0, 0), (0, 0), (0, 0))
    kp, vp = jnp.pad(kc, pad), jnp.pad(vc, pad)
    kband = jnp.concatenate([kp[:, j:j + nc] for j in range(nb)], axis=2)
    vband = jnp.concatenate([vp[:, j:j + nc] for j in range(nb)], axis=2)
    qi = jnp.arange(CHUNK)
    kj = jnp.arange(nb * CHUNK)
    bias = _rel_bias(table, kj[None, :] - BAND_CHUNKS * CHUNK - qi[:, None])
    kpos = (jnp.arange(nc)[:, None] - BAND_CHUNKS) * CHUNK + kj[None, :]
    s = jnp.einsum('bnqhd,bnkhd->bnhqk', qc, kband) * (DH_B ** -0.5) + bias[None, None]
    s = jnp.where((kpos >= 0)[None, :, None, None, :], s, NEG_INF)
    p = jax.nn.softmax(s, axis=-1)
    o = jnp.einsum('bnhqk,bnkhd->bnqhd', p, vband)
    return o.reshape(b, t, W_B).astype(q.dtype)


def _attention_sample(q, k, v, k_cache, v_cache, table):
    b, t = q.shape[:2]
    n_cache = k_cache.shape[2]
    kall = jnp.concatenate([k_cache.astype(f32), jnp.transpose(k, (0, 2, 1, 3)).astype(f32)], axis=2)
    vall = jnp.concatenate([v_cache.astype(f32), jnp.transpose(v, (0, 2, 1, 3)).astype(f32)], axis=2)
    bias = _rel_bias(table, jnp.arange(n_cache + t)[None, :] - n_cache - jnp.arange(t)[:, None])
    s = jnp.einsum('bqhd,bhkd->bhqk', q.astype(f32), kall) * (DH_B ** -0.5) + bias[None]
    p = jax.nn.softmax(s, axis=-1)
    o = jnp.einsum('bhqk,bhkd->bqhd', p, vall)
    return o.reshape(b, t, W_B).astype(q.dtype)


def _gated_delta_chunked(q, k, v, g, beta, s0):
    t = q.shape[1]
    cs = min(CHUNK, t)
    xs = tuple(_to_chunks(a, cs) for a in (q, k, v, g, beta))
    incl = jnp.tril(jnp.ones((cs, cs), dtype=bool))
    strict = jnp.tril(jnp.ones((cs, cs), dtype=bool), -1)
    eye = jnp.eye(cs, dtype=f32)

    def step(S, inp):
        qc, kc, vc, gc, bc = inp
        dv = vc.shape[-1]
        G = jnp.cumsum(gc, axis=-1)
        diff = G[..., :, None] - G[..., None, :]
        dec = jnp.where(incl, jnp.exp(jnp.where(incl, diff, 0.0)), 0.0)
        A = jnp.where(strict, bc[..., :, None] * jnp.einsum('bhid,bhjd->bhij', kc, kc) * dec, 0.0)
        rhs = jnp.concatenate([bc[..., None] * vc, (bc * jnp.exp(G))[..., None] * kc], axis=-1)
        sol = lax.linalg.triangular_solve(A + eye, rhs, left_side=True, lower=True, unit_diagonal=True)
        u, w = sol[..., :dv], sol[..., dv:]
        un = u - jnp.einsum('bhik,bhkv->bhiv', w, S)
        qk = jnp.einsum('bhik,bhjk->bhij', qc, kc) * dec
        o = (jnp.einsum('bhik,bhkv->bhiv', qc * jnp.exp(G)[..., None], S)
             + jnp.einsum('bhij,bhjv->bhiv', qk, un))
        gl = G[..., -1:]
        S = (jnp.exp(gl)[..., None] * S
             + jnp.einsum('bhjk,bhjv->bhkv', kc * jnp.exp(gl - G)[..., None], un))
        return S, o

    s_fin, o = lax.scan(step, s0, xs)
    return _from_chunks(o), s_fin


def _gla_chunked(q, k, v, gk, s0):
    t = q.shape[1]
    cs = min(CHUNK, t)
    xs = tuple(_to_chunks(a, cs) for a in (q, k, v, gk))
    incl = jnp.tril(jnp.ones((cs, cs), dtype=bool))[:, :, None]

    def step(S, inp):
        qc, kc, vc, gc = inp
        G = jnp.cumsum(gc, axis=2)
        diff = G[:, :, :, None, :] - G[:, :, None, :, :]
        dec = jnp.where(incl, jnp.exp(jnp.where(incl, diff, 0.0)), 0.0)
        att = jnp.einsum('bhik,bhjk,bhijk->bhij', qc, kc, dec)
        o = (jnp.einsum('bhik,bhkv->bhiv', qc * jnp.exp(G), S)
             + jnp.einsum('bhij,bhjv->bhiv', att, vc))
        gl = G[:, :, -1:, :]
        S = (jnp.exp(gl[:, :, 0, :])[..., None] * S
             + jnp.einsum('bhjk,bhjv->bhkv', kc * jnp.exp(gl - G), vc))
        return S, o

    s_fin, o = lax.scan(step, s0, xs)
    return _from_chunks(o), s_fin


def _delta_mixer(qkv, z, a, bb, conv_hist, s0, conv_w, a_log, dt_bias, norm_g):
    b, t = qkv.shape[:2]
    ext = jnp.concatenate([conv_hist.astype(qkv.dtype), qkv], axis=1)
    acc = ext[:, 0:t] * conv_w[0]
    for j in range(1, CONV_W):
        acc = acc + ext[:, j:j + t] * conv_w[j]
    c = jax.nn.silu(acc.astype(f32))
    q = _l2norm(c[..., :W_C].reshape(b, t, H_C, DK_C)) * (DK_C ** -0.5)
    k = _l2norm(c[..., W_C:2 * W_C].reshape(b, t, H_C, DK_C))
    v = c[..., 2 * W_C:].reshape(b, t, H_C, DV_C)
    beta = jax.nn.sigmoid(bb.astype(f32))
    g = -jnp.exp(a_log.astype(f32)) * jax.nn.softplus(a.astype(f32) + dt_bias.astype(f32))
    o, s_new = _gated_delta_chunked(q, k, v, g, beta, s0.astype(f32))
    o = _rmsnorm(o, norm_g) * jax.nn.silu(z.astype(f32).reshape(b, t, H_C, DV_C))
    return o.reshape(b, t, W_C).astype(qkv.dtype), ext[:, -(CONV_W - 1):], s_new


def _gla_mixer(q, k, v, gate, gk_low, s0, w_gk, b_gk, norm_g):
    b, t = q.shape[:2]
    gk = jax.nn.log_sigmoid(jnp.einsum('btr,rk->btk', gk_low.astype(f32), w_gk.astype(f32))
                            + b_gk.astype(f32)) / GLA_GATE_NORM
    qh = q.astype(f32).reshape(b, t, H_D, DK_D) * (DK_D ** -0.5)
    kh = k.astype(f32).reshape(b, t, H_D, DK_D)
    vh = v.astype(f32).reshape(b, t, H_D, DV_D)
    o, s_new = _gla_chunked(qh, kh, vh, gk.reshape(b, t, H_D, DK_D), s0.astype(f32))
    o = _rmsnorm(o, norm_g) * jax.nn.silu(gate.astype(f32).reshape(b, t, H_D, DV_D))
    return o.reshape(b, t, W_D).astype(q.dtype), s_new


def _mixers_jax(p_all, row0, b, t, pos0, pool_h, kv_h, conv_h, sd_h, sg_h, lw):
    (pool_w, pool_scale, rel_bias, conv_w, a_log, dt_bias, delta_norm_g,
     gla_w_gk, gla_b_gk, gla_norm_g) = lw
    p = p_all[row0:row0 + b * t].reshape(b, t, P_COLS)
    u_a = p[..., P_UA:P_UA + 256]
    y_a, st_pool = _pool_mixer(u_a, pool_h, pos0, pool_w, pool_scale)
    q_b, k_b, v_b = (p[..., o:o + 256].reshape(b, t, H_B, DH_B) for o in (P_QB, P_KB, P_VB))
    if kv_h is None:
        y_b = _band_attention_prompt(q_b, k_b, v_b, rel_bias)
        keep = min(ATTN_WINDOW, t)
        k_rows, v_rows = k_b[:, t - keep:], v_b[:, t - keep:]
    else:
        y_b = _attention_sample(q_b, k_b, v_b, kv_h[0], kv_h[1], rel_bias)
        k_rows, v_rows = k_b, v_b
    small = p[..., P_SMALL:P_SMALL + 24]
    y_c, st_d = _delta_call(p_all, row0, b, t, conv_h, sd_h, conv_w, a_log, dt_bias, delta_norm_g)
    st_conv = jnp.concatenate([conv_h, p[..., P_QKVC:P_QKVC + 768]], axis=1)[:, -(CONV_W - 1):]
    y_d, st_g = _gla_mixer(p[..., P_QD:P_QD + 128], p[..., P_KD:P_KD + 128], p[..., P_VD:P_VD + 256],
                           p[..., P_GD:P_GD + 256], small[..., 8:24], sg_h, gla_w_gk, gla_b_gk,
                           gla_norm_g)
    n = b * t
    ys = tuple(y.reshape(n, 256) for y in (y_a, y_b, y_c, y_d))
    return ys, (st_pool, jnp.transpose(k_rows, (0, 2, 1, 3)), jnp.transpose(v_rows, (0, 2, 1, 3)),
                st_conv, st_d, st_g)


def _mix_layer(p, b, t, pos0, hist, lw):
    (pool_w, pool_scale, rel_table, conv_w, a_log, dt_bias, delta_norm_g,
     gla_w_gk, gla_b_gk, gla_norm_g) = lw
    pool_h, conv_h, sd_h, sg_h, kv_h = hist
    y_a = _pool_call(p, 0, b, t, pos0, pool_h, pool_w, pool_scale)
    if kv_h is None:
        y_b = _band_attn_call(p, b, t, rel_table)
        keep = min(ATTN_WINDOW, t)
    else:
        y_b = _sample_attn_call(p, 0, b, t, kv_h[0], kv_h[1], rel_table)
        keep = t
    y_c, st_d = _delta_call(p, 0, b, t, conv_h, sd_h, conv_w, a_log, dt_bias, delta_norm_g)
    y_d, st_g = _gla_call(p, 0, b, t, sg_h, gla_w_gk, gla_b_gk, gla_norm_g)
    p3 = p.reshape(b, t, P_COLS)
    st_pool = jnp.concatenate([pool_h, p3[:, :, P_UA:P_UA + W_A]], axis=1)[:, -POOL_HIST:]
    st_conv = jnp.concatenate([conv_h, p3[:, :, P_QKVC:P_QKVC + QKV_C]], axis=1)[:, -(CONV_W - 1):]
    k_rows = p3[:, t - keep:, P_KB:P_KB + W_B].reshape(b, keep, H_B, DH_B)
    v_rows = p3[:, t - keep:, P_VB:P_VB + W_B].reshape(b, keep, H_B, DH_B)
    return (y_a, y_b, y_c, y_d), (st_pool, jnp.transpose(k_rows, (0, 2, 1, 3)),
                                  jnp.transpose(v_rows, (0, 2, 1, 3)), st_conv, st_d, st_g)


def kernel(x_prompt, x_sample, cache_pool, cache_attn_k, cache_attn_v, state_conv, state_delta,
           state_gla, attn_norm_g, w_in, pool_w, pool_scale, rel_bias, conv_w, a_log, dt_bias,
           delta_norm_g, gla_w_gk, gla_b_gk, gla_norm_g, w_out, mlp_norm_g, w_up, w_down,
           final_norm_g):
    bp, tp, _ = x_prompt.shape
    bs, ts, _ = x_sample.shape
    w_in_r = _reorder_w_in(w_in).astype(bf16)
    w_out_b, w_up_b, w_down_b = w_out.astype(bf16), w_up.astype(bf16), w_down.astype(bf16)
    gf = final_norm_g.reshape(1, D_MODEL)
    zero_hist = (jnp.zeros((bp, POOL_HIST, W_A), f32), jnp.zeros((bp, CONV_W - 1, QKV_C), f32),
                 jnp.zeros((bp, H_C, DK_C, DV_C), f32), jnp.zeros((bp, H_D, DK_D, DV_D), f32))
    xs = [x_prompt.reshape(bp * tp, D_MODEL), x_sample.reshape(bs * ts, D_MODEL)]
    states = ([], [])
    for l in range(DEPTH):
        lw = (pool_w[l], pool_scale[l], rel_bias[l], conv_w[l], a_log[l], dt_bias[l], delta_norm_g[l],
              gla_w_gk[l], gla_b_gk[l], gla_norm_g[l])
        hists = (zero_hist + (None,),
                 (cache_pool[l], state_conv[l], state_delta[l], state_gla[l],
                  (cache_attn_k[l], cache_attn_v[l])))
        for s, (b, t, pos0) in enumerate(((bp, tp, 0), (bs, ts, tp))):
            p = _inproj(xs[s], attn_norm_g[l].reshape(1, D_MODEL), w_in_r[l])
            ys, st = _mix_layer(p, b, t, pos0, hists[s], lw)
            xs[s] = _outmlp(xs[s], ys, w_out_b[l], mlp_norm_g[l].reshape(1, D_MODEL), w_up_b[l],
                            w_down_b[l], gf, final=(l == DEPTH - 1))
            states[s].append(st)
    y_prompt = xs[0].reshape(bp, tp, D_MODEL)
    y_sample = xs[1].reshape(bs, ts, D_MODEL)
    outs_p = tuple(jnp.stack([st[i] for st in states[0]]) for i in range(6))
    outs_s = tuple(jnp.stack([st[i] for st in states[1]]) for i in range(6))
    return (y_prompt, y_sample) + outs_p + outs_s
```

```python
import functools
import math

import jax
import jax.numpy as jnp
import numpy as np
from jax import lax
from jax.experimental import pallas as pl
from jax.experimental.pallas import tpu as pltpu

D_MODEL = 1024
DEPTH = 4
CHUNK = 64
W_A = W_B = W_C = W_D = 256
POOL_WINDOWS = (2, 4, 8, 16)
N_POOL = 4
POOL_GW = 64
POOL_HIST = 15
H_B = 4
DH_B = 64
BAND_CHUNKS = 8
ATTN_WINDOW = 512
REL_CLIP = 256
H_C = 4
DK_C = 64
DV_C = 64
CONV_W = 4
QKV_C = 768
H_D = 4
DK_D = 32
DV_D = 64
WK_D = 128
GLA_RANK = 16
GLA_GATE_NORM = 16.0
D_FF = 4096
EPS = 1e-6
NEG_INF = -1e30

P_UA = 0
P_QB, P_KB, P_VB = 256, 512, 768
P_QKVC = 1024
P_ZC = 1792
P_VD = 2048
P_GD = 2304
P_QD = 2560
P_KD = 2688
P_SMALL = 2816
P_COLS = 2944

ROW_TILE = 512
VMEM_LIMIT = 56 * 1024 * 1024

f32 = jnp.float32
bf16 = jnp.bfloat16


def _reorder_w_in(w_in):
    pad = jnp.zeros(w_in.shape[:-1] + (P_COLS - P_SMALL - 24,), w_in.dtype)
    return jnp.concatenate([
        w_in[..., 0:2048],
        w_in[..., 2312:2568],
        w_in[..., 2568:2824],
        w_in[..., 2056:2184],
        w_in[..., 2184:2312],
        w_in[..., 2048:2056],
        w_in[..., 2824:2840],
        pad], axis=-1)


def _inproj_kernel(x_ref, g_ref, w_ref, o_ref):
    x = x_ref[...]
    ms = jnp.mean(x * x, axis=-1, keepdims=True)
    h = x * lax.rsqrt(ms + EPS) * g_ref[...]
    o_ref[...] = jnp.dot(h.astype(bf16), w_ref[...], preferred_element_type=f32)


def _inproj(x, g, w):
    n = x.shape[0]
    return pl.pallas_call(
        _inproj_kernel,
        out_shape=jax.ShapeDtypeStruct((n, P_COLS), f32),
        grid=(n // ROW_TILE,),
        in_specs=[
            pl.BlockSpec((ROW_TILE, D_MODEL), lambda i: (i, 0)),
            pl.BlockSpec((1, D_MODEL), lambda i: (0, 0)),
            pl.BlockSpec((D_MODEL, P_COLS), lambda i: (0, 0), pipeline_mode=pl.Buffered(1)),
        ],
        out_specs=pl.BlockSpec((ROW_TILE, P_COLS), lambda i: (i, 0)),
        compiler_params=pltpu.CompilerParams(
            dimension_semantics=("arbitrary",), vmem_limit_bytes=VMEM_LIMIT),
        name="inproj",
    )(x, g, w)


FF_CHUNK = 1024


def _outmlp_kernel(x_ref, ya_ref, yb_ref, yc_ref, yd_ref, wo_ref, g2_ref, wu_ref, wd_ref, gf_ref,
                   o_ref, *, final):
    acc = None
    for k, y_ref in enumerate((ya_ref, yb_ref, yc_ref, yd_ref)):
        d = jnp.dot(y_ref[...].astype(bf16), wo_ref[k * 256:(k + 1) * 256, :],
                    preferred_element_type=f32)
        acc = d if acc is None else acc + d
    x = x_ref[...] + acc
    ms = jnp.mean(x * x, axis=-1, keepdims=True)
    h2 = (x * lax.rsqrt(ms + EPS) * g2_ref[...]).astype(bf16)
    acc = None
    for c in range(D_FF // FF_CHUNK):
        up = jnp.dot(h2, wu_ref[:, c * FF_CHUNK:(c + 1) * FF_CHUNK], preferred_element_type=f32)
        up = jnp.square(jnp.maximum(up, 0.0)).astype(bf16)
        d = jnp.dot(up, wd_ref[c * FF_CHUNK:(c + 1) * FF_CHUNK, :], preferred_element_type=f32)
        acc = d if acc is None else acc + d
    x = x + acc
    if final:
        ms = jnp.mean(x * x, axis=-1, keepdims=True)
        x = x * lax.rsqrt(ms + EPS) * gf_ref[...]
    o_ref[...] = x


def _outmlp(x, ys, wo, g2, wu, wd, gf, final):
    n = x.shape[0]
    const = lambda i: (0, 0)
    row = lambda i: (i, 0)
    single = pl.Buffered(1)
    return pl.pallas_call(
        functools.partial(_outmlp_kernel, final=final),
        out_shape=jax.ShapeDtypeStruct((n, D_MODEL), f32),
        grid=(n // ROW_TILE,),
        in_specs=[pl.BlockSpec((ROW_TILE, D_MODEL), row)]
        + [pl.BlockSpec((ROW_TILE, 256), row)] * 4
        + [pl.BlockSpec((D_MODEL, D_MODEL), const, pipeline_mode=single),
           pl.BlockSpec((1, D_MODEL), const),
           pl.BlockSpec((D_MODEL, D_FF), const, pipeline_mode=single),
           pl.BlockSpec((D_FF, D_MODEL), const, pipeline_mode=single),
           pl.BlockSpec((1, D_MODEL), const)],
        out_specs=pl.BlockSpec((ROW_TILE, D_MODEL), row),
        compiler_params=pltpu.CompilerParams(
            dimension_semantics=("arbitrary",), vmem_limit_bytes=VMEM_LIMIT),
        name="outmlp_final" if final else "outmlp",
    )(x, *ys, wo, g2, wu, wd, gf)


HI = lax.Precision.HIGHEST


def _dot(a, b, prec=None):
    return jnp.dot(a, b, preferred_element_type=f32, precision=prec)


def _dot_nt(a, b, prec=None):
    return lax.dot_general(a, b, (((1,), (1,)), ((), ())), preferred_element_type=f32, precision=prec)


def _dot_tn(a, b, prec=None):
    return lax.dot_general(a, b, (((0,), (0,)), ((), ())), preferred_element_type=f32, precision=prec)


def _iota2(shape, dim):
    return lax.broadcasted_iota(jnp.int32, shape, dim)


def _seg_ones(n, seg):
    return (_iota2((n, n), 0) // seg == _iota2((n, n), 1) // seg).astype(f32)


def _silu(x):
    return x * jax.nn.sigmoid(x)


def _unit_lower_inverse(a, c):
    ri, ci = _iota2((c, c), 0), _iota2((c, c), 1)
    d = jnp.where(ri == ci, 1.0, 0.0) - jnp.where((ri // 2 == ci // 2) & (ri > ci), a, 0.0)
    s = 2
    while s < c:
        e = jnp.where((ri // (2 * s) == ci // (2 * s)) & (ri % (2 * s) >= s) & (ci % (2 * s) < s), a, 0.0)
        d = d - _dot(d, _dot(e, d))
        s *= 2
    return d


def _delta_kernel_v1(q_ref, k_ref, v_ref, z_ref, sm_ref, hist_ref, s0_ref, cw_ref, alog_ref, dtb_ref, ng_ref,
                     y_ref, sout_ref, ext_ref, qn_ref, kn_ref, vv_ref, g_ref, bt_ref, o_ref, s_ref,
                     *, tb, c):
    t_idx = pl.program_id(1)

    @pl.when(t_idx == 0)
    def _():
        s_ref[...] = s0_ref[0]
        for s in range(3):
            ext_ref[s, 0:8, :] = jnp.zeros((8, 256), f32)
            ext_ref[s, 5:8, :] = hist_ref[0, :, s * 256:(s + 1) * 256]

    seg = _seg_ones(256, DK_C)
    parts = []
    for s, ref in enumerate((q_ref, k_ref, v_ref)):
        ext_ref[s, 8:8 + tb, :] = ref[...]
        acc = ext_ref[s, 5:5 + tb, :] * cw_ref[0:1, s * 256:(s + 1) * 256]
        for j in range(1, CONV_W):
            acc = acc + ext_ref[s, 5 + j:5 + j + tb, :] * cw_ref[j:j + 1, s * 256:(s + 1) * 256]
        parts.append(_silu(acc))
        ext_ref[s, 5:8, :] = ext_ref[s, tb + 5:tb + 8, :]
    cq, ck, cv = parts
    qn_ref[...] = cq * lax.rsqrt(_dot(cq * cq, seg, HI) + EPS) * (DK_C ** -0.5)
    kn_ref[...] = ck * lax.rsqrt(_dot(ck * ck, seg, HI) + EPS)
    vv_ref[...] = cv
    sm = sm_ref[...]
    g_ref[...] = -jnp.exp(alog_ref[...]) * jax.nn.softplus(sm + dtb_ref[...])
    bt_ref[...] = jax.nn.sigmoid(sm)

    ri, ci = _iota2((c, c), 0), _iota2((c, c), 1)
    incl = ri >= ci
    strict = ri > ci
    ltri = incl.astype(f32)

    def chunk(ic, carry):
        r0 = pl.multiple_of(ic * c, c)
        rows = pl.ds(r0, c)
        gcum = _dot(ltri, g_ref[rows, :], HI)
        gcum_t = gcum.T
        beta = bt_ref[rows, :]
        for h in range(H_C):
            lanes = slice(h * 64, (h + 1) * 64)
            qh, kh, vh = qn_ref[rows, lanes], kn_ref[rows, lanes], vv_ref[rows, lanes]
            gc = gcum[:, h:h + 1]
            gr = gcum_t[h:h + 1, :]
            dec = jnp.where(incl, jnp.exp(jnp.where(incl, gc - gr, 0.0)), 0.0)
            bc = beta[:, 4 + h:5 + h]
            eg = jnp.exp(gc)
            a = jnp.where(strict, bc * _dot_nt(kh, kh) * dec, 0.0)
            rhs = jnp.concatenate([bc * vh, (bc * eg) * kh], axis=1)
            sol = _dot(_unit_lower_inverse(a, c), rhs)
            u, w = sol[:, :64], sol[:, 64:]
            st = s_ref[h]
            un = u - _dot(w, st)
            qk = _dot_nt(qh, kh) * dec
            o_ref[rows, lanes] = _dot(qh * eg, st) + _dot(qk, un)
            gl = gcum[c - 1:c, h:h + 1]
            s_ref[h] = jnp.exp(gl) * st + _dot_tn(kh * jnp.exp(gl - gc), un)
        return carry

    lax.fori_loop(0, tb // c, chunk, 0)

    o = o_ref[...]
    o = o * lax.rsqrt(_dot(o * o, seg, HI) * (1.0 / DV_C) + EPS) * ng_ref[...]
    y_ref[...] = o * _silu(z_ref[...])

    @pl.when(t_idx == pl.num_programs(1) - 1)
    def _():
        sout_ref[0] = s_ref[...]


def _delta_call_v1(p, row0, b, t, hist, s0, conv_w, a_log, dt_bias, norm_g):
    tb = min(t, ROW_TILE)
    c = min(CHUNK, t)
    nt = t // tb
    rb0 = row0 // tb

    def col(cb):
        return lambda i, j: (rb0 + i * nt + j, cb)

    pad = jnp.zeros((1, 128 - H_C), f32)
    alog = jnp.concatenate([a_log.reshape(1, H_C), pad], axis=1)
    dtb = jnp.concatenate([dt_bias.reshape(1, H_C), pad], axis=1)
    ng = jnp.tile(norm_g.reshape(1, DV_C), (1, H_C))
    const = lambda i, j: (0, 0)
    y, s_new = pl.pallas_call(
        functools.partial(_delta_kernel_v1, tb=tb, c=c),
        out_shape=(jax.ShapeDtypeStruct((b * t, 256), f32),
                   jax.ShapeDtypeStruct((b, H_C, DK_C, DV_C), f32)),
        grid=(b, nt),
        in_specs=[
            pl.BlockSpec((tb, 256), col(P_QKVC // 256)),
            pl.BlockSpec((tb, 256), col(P_QKVC // 256 + 1)),
            pl.BlockSpec((tb, 256), col(P_QKVC // 256 + 2)),
            pl.BlockSpec((tb, 256), col(P_ZC // 256)),
            pl.BlockSpec((tb, 128), col(P_SMALL // 128)),
            pl.BlockSpec((1, CONV_W - 1, QKV_C), lambda i, j: (i, 0, 0)),
            pl.BlockSpec((1, H_C, DK_C, DV_C), lambda i, j: (i, 0, 0, 0)),
            pl.BlockSpec((CONV_W, QKV_C), const),
            pl.BlockSpec((1, 128), const),
            pl.BlockSpec((1, 128), const),
            pl.BlockSpec((1, 256), const),
        ],
        out_specs=(pl.BlockSpec((tb, 256), lambda i, j: (i * nt + j, 0)),
                   pl.BlockSpec((1, H_C, DK_C, DV_C), lambda i, j: (i, 0, 0, 0))),
        scratch_shapes=[
            pltpu.VMEM((3, tb + 8, 256), f32),
            pltpu.VMEM((tb, 256), f32), pltpu.VMEM((tb, 256), f32), pltpu.VMEM((tb, 256), f32),
            pltpu.VMEM((tb, 128), f32), pltpu.VMEM((tb, 128), f32),
            pltpu.VMEM((tb, 256), f32),
            pltpu.VMEM((H_C, DK_C, DV_C), f32),
        ],
        compiler_params=pltpu.CompilerParams(
            dimension_semantics=("arbitrary", "arbitrary"), vmem_limit_bytes=VMEM_LIMIT),
        name=f"delta_t{t}",
    )(p, p, p, p, p, hist, s0, conv_w, alog, dtb, ng)
    return y, s_new


def _bmm(a, b, prec=None):
    return jnp.einsum('nij,njk->nik', a, b, preferred_element_type=f32, precision=prec)


def _bmm_nt(a, b):
    return jnp.einsum('nid,njd->nij', a, b, preferred_element_type=f32)


def _bmm_tn(a, b):
    return jnp.einsum('nji,njk->nik', a, b, preferred_element_type=f32)


def _unit_lower_inverse_batched(a):
    n, c, _ = a.shape
    ri, ci = _iota2((n, c, c), 1), _iota2((n, c, c), 2)
    d = jnp.where(ri == ci, 1.0, 0.0) - jnp.where((ri // 2 == ci // 2) & (ri > ci), a, 0.0)
    s = 2
    while s < c:
        e = jnp.where((ri // (2 * s) == ci // (2 * s)) & (ri % (2 * s) >= s) & (ci % (2 * s) < s), a, 0.0)
        d = d - _bmm(d, _bmm(e, d))
        s *= 2
    return d


def _delta_kernel(q_ref, k_ref, v_ref, z_ref, sm_ref, hist_ref, s0_ref, cw_ref, alog_ref, dtb_ref, ng_ref,
                  y_ref, sout_ref, ext_ref, s_ref, *, ns, tl, c, carry):
    t_idx = pl.program_id(1)
    tb = ns * tl
    nch = tb // c
    nb = H_C * nch

    def heads(x):
        return jnp.concatenate(
            [x[:, h * 64:(h + 1) * 64].reshape(nch, c, 64) for h in range(H_C)], axis=0)

    if carry:
        @pl.when(t_idx == 0)
        def _():
            s_ref[...] = s0_ref[0]
            for s in range(3):
                ext_ref[s, :, 0:8, :] = jnp.zeros((ns, 8, 256), f32)
                ext_ref[s, :, 5:8, :] = hist_ref[:, :, s * 256:(s + 1) * 256]
    else:
        for s in range(3):
            ext_ref[s, :, 5:8, :] = hist_ref[:, :, s * 256:(s + 1) * 256]

    parts = []
    for s, ref in enumerate((q_ref, k_ref, v_ref)):
        ext_ref[s, :, 8:8 + tl, :] = ref[...].reshape(ns, tl, 256)
        acc = ext_ref[s, :, 5:5 + tl, :] * cw_ref[0:1, s * 256:(s + 1) * 256]
        for j in range(1, CONV_W):
            acc = acc + ext_ref[s, :, 5 + j:5 + j + tl, :] * cw_ref[j:j + 1, s * 256:(s + 1) * 256]
        parts.append(_silu(acc).reshape(tb, 256))
        if carry:
            ext_ref[s, :, 5:8, :] = ext_ref[s, :, tl + 5:tl + 8, :]
    cq, ck, cv = parts
    seg = _seg_ones(256, DK_C)
    qn = cq * lax.rsqrt(_dot(cq * cq, seg, HI) + EPS) * (DK_C ** -0.5)
    kn = ck * lax.rsqrt(_dot(ck * ck, seg, HI) + EPS)

    sm = sm_ref[...]
    g = -jnp.exp(alog_ref[...]) * jax.nn.softplus(sm + dtb_ref[...])
    beta = jax.nn.sigmoid(sm)
    ltri = (_iota2((nch, c, c), 1) >= _iota2((nch, c, c), 2)).astype(f32)
    gcum = _bmm(ltri, g.reshape(nch, c, 128), HI).reshape(tb, 128)
    lane_head = _iota2((128, 256), 1) // 64
    gb = _dot(gcum, (_iota2((128, 256), 0) == lane_head).astype(f32), HI)
    betab = _dot(beta, (_iota2((128, 256), 0) == lane_head + H_C).astype(f32), HI)
    eg = jnp.exp(gb)
    gl = jnp.broadcast_to(gb.reshape(nch, c, 256)[:, c - 1:c, :], (nch, c, 256)).reshape(tb, 256)

    k3, q3 = heads(kn), heads(qn)
    gcb = heads(gb)[:, :, :c]
    grb = jnp.swapaxes(gcb, 1, 2)
    ri, ci = _iota2((nb, c, c), 1), _iota2((nb, c, c), 2)
    incl = ri >= ci
    dec = jnp.where(incl, jnp.exp(jnp.where(incl, gcb - grb, 0.0)), 0.0)
    a = jnp.where(ri > ci, heads(betab)[:, :, :c] * _bmm_nt(k3, k3) * dec, 0.0)
    rhs = jnp.concatenate([heads(betab * cv), heads(betab * eg * kn)], axis=-1)
    sol = _bmm(_unit_lower_inverse_batched(a), rhs)
    u, w = sol[:, :, :64], sol[:, :, 64:]
    qk = _bmm_nt(q3, k3) * dec
    qt3 = heads(qn * eg)
    kt3 = heads(kn * jnp.exp(gl - gb))
    egl = heads(jnp.exp(gl))[:, 0:1, :]

    if carry:
        o_heads = []
        for h in range(H_C):
            st = s_ref[h]
            o_chunks = []
            for ic in range(nch):
                n = h * nch + ic
                un = u[n] - _dot(w[n], st)
                o_chunks.append(_dot(qt3[n], st) + _dot(qk[n], un))
                st = egl[n] * st + _dot_tn(kt3[n], un)
            s_ref[h] = st
            o_heads.append(jnp.concatenate(o_chunks, axis=0))
        o = jnp.concatenate(o_heads, axis=1)

        @pl.when(t_idx == pl.num_programs(1) - 1)
        def _():
            sout_ref[0] = s_ref[...]
    else:
        st = jnp.concatenate([s0_ref[:, h] for h in range(H_C)], axis=0)
        un = u - _bmm(w, st)
        o3 = _bmm(qt3, st) + _bmm(qk, un)
        st = egl * st + _bmm_tn(kt3, un)
        for h in range(H_C):
            sout_ref[:, h] = st[h * nch:(h + 1) * nch]
        o = jnp.concatenate([o3[h * nch:(h + 1) * nch].reshape(tb, 64) for h in range(H_C)], axis=1)

    o = o * lax.rsqrt(_dot(o * o, seg, HI) * (1.0 / DV_C) + EPS) * ng_ref[...]
    y_ref[...] = o * _silu(z_ref[...])


def _delta_call(p, row0, b, t, hist, s0, conv_w, a_log, dt_bias, norm_g):
    c = min(CHUNK, t)
    carry = t > c
    ns, tl = (1, min(t, ROW_TILE)) if carry else (ROW_TILE // (2 * t), t)
    tb = ns * tl
    nt = t // tl
    rb0 = row0 // tb

    def col(cb):
        return lambda i, j: (rb0 + i * nt + j, cb)

    pad = jnp.zeros((1, 128 - H_C), f32)
    alog = jnp.concatenate([a_log.reshape(1, H_C), pad], axis=1)
    dtb = jnp.concatenate([dt_bias.reshape(1, H_C), pad], axis=1)
    ng = jnp.tile(norm_g.reshape(1, DV_C), (1, H_C))
    const = lambda i, j: (0, 0)
    y, s_new = pl.pallas_call(
        functools.partial(_delta_kernel, ns=ns, tl=tl, c=c, carry=carry),
        out_shape=(jax.ShapeDtypeStruct((b * t, 256), f32),
                   jax.ShapeDtypeStruct((b, H_C, DK_C, DV_C), f32)),
        grid=(b // ns, nt),
        in_specs=[
            pl.BlockSpec((tb, 256), col(P_QKVC // 256)),
            pl.BlockSpec((tb, 256), col(P_QKVC // 256 + 1)),
            pl.BlockSpec((tb, 256), col(P_QKVC // 256 + 2)),
            pl.BlockSpec((tb, 256), col(P_ZC // 256)),
            pl.BlockSpec((tb, 128), col(P_SMALL // 128)),
            pl.BlockSpec((ns, CONV_W - 1, QKV_C), lambda i, j: (i, 0, 0)),
            pl.BlockSpec((ns, H_C, DK_C, DV_C), lambda i, j: (i, 0, 0, 0)),
            pl.BlockSpec((CONV_W, QKV_C), const),
            pl.BlockSpec((1, 128), const),
            pl.BlockSpec((1, 128), const),
            pl.BlockSpec((1, 256), const),
        ],
        out_specs=(pl.BlockSpec((tb, 256), lambda i, j: (i * nt + j, 0)),
                   pl.BlockSpec((ns, H_C, DK_C, DV_C), lambda i, j: (i, 0, 0, 0))),
        scratch_shapes=[
            pltpu.VMEM((3, ns, tl + 8, 256), f32),
            pltpu.VMEM((H_C, DK_C, DV_C), f32),
        ],
        compiler_params=pltpu.CompilerParams(
            dimension_semantics=("arbitrary", "arbitrary"), vmem_limit_bytes=VMEM_LIMIT),
        name=f"delta_t{t}",
    )(p, p, p, p, p, hist, s0, conv_w, alog, dtb, ng)
    return y, s_new


def _gla_kernel(q_ref, k_ref, v_ref, gate_ref, sm_ref, s0_ref, wgk_ref, bgk_ref, ng_ref,
                y_ref, sout_ref, qs_ref, g_ref, o_ref, s_ref, *, tb, c):
    t_idx = pl.program_id(1)
    blk = _iota2((WK_D, W_D), 0) // DK_D == _iota2((WK_D, W_D), 1) // DV_D
    seg_kv = blk.astype(f32)

    @pl.when(t_idx == 0)
    def _():
        s_ref[...] = jnp.zeros((WK_D, W_D), f32)
        for h in range(H_D):
            s_ref[h * DK_D:(h + 1) * DK_D, h * DV_D:(h + 1) * DV_D] = s0_ref[0, h]

    gk = jax.nn.log_sigmoid(_dot(sm_ref[...], wgk_ref[...], HI) + bgk_ref[...]) * (1.0 / GLA_GATE_NORM)
    g_ref[...] = gk
    qs_ref[...] = q_ref[...] * (DK_D ** -0.5)
    ltri = (_iota2((c, c), 0) >= _iota2((c, c), 1)).astype(f32)
    rowid = _iota2((c, WK_D), 0)

    def chunk(ic, carry):
        rows = pl.ds(pl.multiple_of(ic * c, c), c)
        g = _dot(ltri, g_ref[rows, :], HI)
        q, k, v = qs_ref[rows, :], k_ref[rows, :], v_ref[rows, :]
        st = s_ref[...]
        o = _dot(q * jnp.exp(g), st)
        for jb in range(c // 8):
            lo = 8 * jb
            m = c - lo
            e = []
            for j in range(lo, lo + 8):
                ej = jnp.exp(g[lo:] - g[j:j + 1]) * (q[lo:] * k[j:j + 1])
                e.append(jnp.where(rowid[lo:] >= j, ej, 0.0))
            att = _dot(jnp.concatenate(e, axis=0), seg_kv)
            upd = att[0:m] * v[lo:lo + 1]
            for jj in range(1, 8):
                upd = upd + att[jj * m:(jj + 1) * m] * v[lo + jj:lo + jj + 1]
            o = o + upd if lo == 0 else jnp.concatenate([o[:lo], o[lo:] + upd], axis=0)
        o_ref[rows, :] = o
        gl = g[c - 1:c]
        egl_col = jnp.exp(g.T[:, c - 1:c])
        s_new = egl_col * st + _dot_tn(k * jnp.exp(gl - g), v)
        s_ref[...] = jnp.where(blk, s_new, 0.0)
        return carry

    lax.fori_loop(0, tb // c, chunk, 0)

    o = o_ref[...]
    o = o * lax.rsqrt(_dot(o * o, _seg_ones(W_D, DV_D), HI) * (1.0 / DV_D) + EPS) * ng_ref[...]
    y_ref[...] = o * _silu(gate_ref[...])

    @pl.when(t_idx == pl.num_programs(1) - 1)
    def _():
        for h in range(H_D):
            sout_ref[0, h] = s_ref[h * DK_D:(h + 1) * DK_D, h * DV_D:(h + 1) * DV_D]


def _gla_call(p, row0, b, t, s0, w_gk, b_gk, norm_g):
    tb = min(t, ROW_TILE)
    c = min(CHUNK, t)
    nt = t // tb
    rb0 = row0 // tb

    def col(cb):
        return lambda i, j: (rb0 + i * nt + j, cb)

    wgk = jnp.zeros((128, WK_D), f32).at[8:8 + GLA_RANK].set(w_gk)
    ng = jnp.tile(norm_g.reshape(1, DV_D), (1, H_D))
    const = lambda i, j: (0, 0)
    y, s_new = pl.pallas_call(
        functools.partial(_gla_kernel, tb=tb, c=c),
        out_shape=(jax.ShapeDtypeStruct((b * t, W_D), f32),
                   jax.ShapeDtypeStruct((b, H_D, DK_D, DV_D), f32)),
        grid=(b, nt),
        in_specs=[
            pl.BlockSpec((tb, WK_D), col(P_QD // WK_D)),
            pl.BlockSpec((tb, WK_D), col(P_KD // WK_D)),
            pl.BlockSpec((tb, W_D), col(P_VD // W_D)),
            pl.BlockSpec((tb, W_D), col(P_GD // W_D)),
            pl.BlockSpec((tb, 128), col(P_SMALL // 128)),
            pl.BlockSpec((1, H_D, DK_D, DV_D), lambda i, j: (i, 0, 0, 0)),
            pl.BlockSpec((128, WK_D), const),
            pl.BlockSpec((1, WK_D), const),
            pl.BlockSpec((1, W_D), const),
        ],
        out_specs=(pl.BlockSpec((tb, W_D), lambda i, j: (i * nt + j, 0)),
                   pl.BlockSpec((1, H_D, DK_D, DV_D), lambda i, j: (i, 0, 0, 0))),
        scratch_shapes=[
            pltpu.VMEM((tb, WK_D), f32), pltpu.VMEM((tb, WK_D), f32),
            pltpu.VMEM((tb, W_D), f32),
            pltpu.VMEM((WK_D, W_D), f32),
        ],
        compiler_params=pltpu.CompilerParams(
            dimension_semantics=("arbitrary", "arbitrary"), vmem_limit_bytes=VMEM_LIMIT),
        name=f"gla_t{t}",
    )(p, p, p, p, p, s0, wgk, b_gk.reshape(1, WK_D), ng)
    return y, s_new


def _pool_kernel(u_ref, hist_ref, w_ref, scale_ref, y_ref, ext_ref, *, tb, pos0):
    t_idx = pl.program_id(1)

    @pl.when(t_idx == 0)
    def _():
        ext_ref[0:8, :] = jnp.zeros((8, W_A), f32)
        ext_ref[1:16, :] = hist_ref[0]

    x = u_ref[...]
    ext_ref[16:16 + tb, :] = x
    wsum = {}
    acc = x
    for k in range(1, 16):
        acc = acc + ext_ref[16 - k:16 - k + tb, :]
        if k + 1 in POOL_WINDOWS:
            wsum[k + 1] = acc
    ext_ref[1:16, :] = ext_ref[tb + 1:tb + 16, :]
    lane_group = _iota2((tb, W_A), 1) // POOL_GW
    pos = pos0 + t_idx * tb + _iota2((tb, W_A), 0)
    ws, win = wsum[POOL_WINDOWS[-1]], jnp.full((tb, W_A), POOL_WINDOWS[-1], jnp.int32)
    for gi in range(N_POOL - 2, -1, -1):
        ws = jnp.where(lane_group == gi, wsum[POOL_WINDOWS[gi]], ws)
        win = jnp.where(lane_group == gi, POOL_WINDOWS[gi], win)
    cnt = jnp.minimum(win, pos + 1).astype(f32)
    pooled = ws / cnt - x
    y_ref[...] = _dot(pooled, w_ref[...]) * scale_ref[...]


def _pool_call(p, row0, b, t, pos0, hist, pool_w, pool_scale):
    tb = min(t, ROW_TILE)
    nt = t // tb
    rb0 = row0 // tb
    wbd = jnp.zeros((W_A, W_A), f32)
    for gi in range(N_POOL):
        wbd = wbd.at[gi * POOL_GW:(gi + 1) * POOL_GW, gi * POOL_GW:(gi + 1) * POOL_GW].set(pool_w[gi])
    const = lambda i, j: (0, 0)
    return pl.pallas_call(
        functools.partial(_pool_kernel, tb=tb, pos0=pos0),
        out_shape=jax.ShapeDtypeStruct((b * t, W_A), f32),
        grid=(b, nt),
        in_specs=[
            pl.BlockSpec((tb, W_A), lambda i, j: (rb0 + i * nt + j, P_UA // W_A)),
            pl.BlockSpec((1, POOL_HIST, W_A), lambda i, j: (i, 0, 0)),
            pl.BlockSpec((W_A, W_A), const),
            pl.BlockSpec((1, W_A), const),
        ],
        out_specs=pl.BlockSpec((tb, W_A), lambda i, j: (i * nt + j, 0)),
        scratch_shapes=[pltpu.VMEM((tb + 16, W_A), f32)],
        compiler_params=pltpu.CompilerParams(
            dimension_semantics=("arbitrary", "arbitrary"), vmem_limit_bytes=VMEM_LIMIT),
        name=f"pool_t{t}",
    )(p, hist, wbd, pool_scale.reshape(1, W_A))


Q_BLOCK = 256
K_WINDOW = Q_BLOCK + BAND_CHUNKS * CHUNK


def _softmax_av(s_parts, v_parts):
    m = functools.reduce(jnp.maximum, [jnp.max(s, axis=-1, keepdims=True) for s in s_parts])
    e_parts = [jnp.exp(s - m) for s in s_parts]
    l = functools.reduce(jnp.add, [jnp.sum(e, axis=-1, keepdims=True) for e in e_parts])
    o = functools.reduce(jnp.add, [_dot(e, v) for e, v in zip(e_parts, v_parts)])
    return o / l


def _band_attn_kernel(q_ref, k_ref, v_ref, bias_ref, y_ref):
    qb = pl.program_id(1)
    start = pl.multiple_of(jnp.maximum(qb * Q_BLOCK - BAND_CHUNKS * CHUNK, 0), Q_BLOCK)
    outs = []
    for h in range(H_B):
        lanes = slice(h * DH_B, (h + 1) * DH_B)
        q = q_ref[:, lanes] * (DH_B ** -0.5)
        k = k_ref[pl.ds(start, K_WINDOW), lanes]
        v = v_ref[pl.ds(start, K_WINDOW), lanes]
        s = _dot_nt(q, k) + bias_ref[0, h]
        outs.append(_softmax_av([s], [v]))
    y_ref[...] = jnp.concatenate(outs, axis=1)


def _rel_bias_rows(table, nq, nk, off):
    period = nq + nk
    m = np.arange(period)
    rel = np.where(m < nk, m, m - period) - off
    row = table.astype(f32)[:, np.clip(rel, -REL_CLIP, REL_CLIP) + REL_CLIP]
    flat = jnp.tile(row, (1, nq))[:, :nq * (period - 1)]
    return flat.reshape(table.shape[0], nq, period - 1)[:, :, :nk]


def _band_bias(table):
    variants = []
    qi = np.arange(Q_BLOCK)[:, None]
    kj = np.arange(K_WINDOW)[None, :]
    for v in range(3):
        dchunk = (qi + v * Q_BLOCK) // CHUNK - kj // CHUNK
        vis = (dchunk >= 0) & (dchunk <= BAND_CHUNKS)
        bias = _rel_bias_rows(table, Q_BLOCK, K_WINDOW, v * Q_BLOCK)
        variants.append(jnp.where(vis[None], bias, NEG_INF))
    return jnp.stack(variants)


def _band_attn_call(p, b, t, rel_table):
    nq = t // Q_BLOCK
    bias = _band_bias(rel_table)
    return pl.pallas_call(
        _band_attn_kernel,
        out_shape=jax.ShapeDtypeStruct((b * t, W_B), f32),
        grid=(b, nq),
        in_specs=[
            pl.BlockSpec((Q_BLOCK, W_B), lambda i, j: (i * nq + j, P_QB // W_B)),
            pl.BlockSpec((t, W_B), lambda i, j: (i, P_KB // W_B)),
            pl.BlockSpec((t, W_B), lambda i, j: (i, P_VB // W_B)),
            pl.BlockSpec((1, H_B, Q_BLOCK, K_WINDOW), lambda i, j: (jnp.minimum(j, 2), 0, 0, 0)),
        ],
        out_specs=pl.BlockSpec((Q_BLOCK, W_B), lambda i, j: (i * nq + j, 0)),
        compiler_params=pltpu.CompilerParams(
            dimension_semantics=("arbitrary", "arbitrary"), vmem_limit_bytes=VMEM_LIMIT),
        name="band_attn",
    )(p, p, p, bias)


def _sample_attn_kernel(q_ref, k_ref, v_ref, kc_ref, vc_ref, bias_ref, y_ref, *, n_cache):
    outs = []
    for h in range(H_B):
        lanes = slice(h * DH_B, (h + 1) * DH_B)
        q = q_ref[:, lanes] * (DH_B ** -0.5)
        s_old = _dot_nt(q, kc_ref[0, h]) + bias_ref[h, :, 0:n_cache]
        s_new = _dot_nt(q, k_ref[:, lanes]) + bias_ref[h, :, n_cache:]
        outs.append(_softmax_av([s_old, s_new], [vc_ref[0, h], v_ref[:, lanes]]))
    y_ref[...] = jnp.concatenate(outs, axis=1)


def _sample_attn_call(p, row0, b, t, k_cache, v_cache, rel_table):
    n_cache = k_cache.shape[2]
    rb0 = row0 // t
    bias = _rel_bias_rows(rel_table, t, n_cache + t, n_cache)
    cache_spec = pl.BlockSpec((1, H_B, n_cache, DH_B), lambda i: (i, 0, 0, 0))
    return pl.pallas_call(
        functools.partial(_sample_attn_kernel, n_cache=n_cache),
        out_shape=jax.ShapeDtypeStruct((b * t, W_B), f32),
        grid=(b,),
        in_specs=[
            pl.BlockSpec((t, W_B), lambda i: (rb0 + i, P_QB // W_B)),
            pl.BlockSpec((t, W_B), lambda i: (rb0 + i, P_KB // W_B)),
            pl.BlockSpec((t, W_B), lambda i: (rb0 + i, P_VB // W_B)),
            cache_spec, cache_spec,
            pl.BlockSpec((H_B, t, n_cache + t), lambda i: (0, 0, 0)),
        ],
        out_specs=pl.BlockSpec((t, W_B), lambda i: (i, 0)),
        compiler_params=pltpu.CompilerParams(
            dimension_semantics=("arbitrary",), vmem_limit_bytes=VMEM_LIMIT),
        name="sample_attn",
    )(p, p, p, k_cache, v_cache, bias)


def _rmsnorm(x, g):
    xf = x.astype(f32)
    y = xf * lax.rsqrt(jnp.mean(xf * xf, axis=-1, keepdims=True) + EPS)
    return (y * g.astype(f32)).astype(x.dtype)


def _l2norm(x):
    return x * lax.rsqrt(jnp.sum(x * x, axis=-1, keepdims=True) + EPS)


def _rel_bias(table, rel):
    idx = jnp.clip(rel, -REL_CLIP, REL_CLIP) + REL_CLIP
    return table.astype(f32)[:, idx]


def _to_chunks(t, cs):
    b, n = t.shape[0], t.shape[1] // cs
    t = t.reshape((b, n, cs) + t.shape[2:])
    return jnp.transpose(t, (1, 0, 3, 2) + tuple(range(4, t.ndim)))


def _from_chunks(t):
    n, b, h, cs = t.shape[:4]
    t = jnp.transpose(t, (1, 0, 3, 2) + tuple(range(4, t.ndim)))
    return t.reshape((b, n * cs, h) + t.shape[4:])


def _pool_mixer(u, hist, pos0, w_pool, scale):
    b, t = u.shape[:2]
    ext = jnp.concatenate([hist.astype(u.dtype), u], axis=1)
    ef = ext.astype(f32)
    c0 = jnp.concatenate([jnp.zeros((b, 1, W_A), f32), jnp.cumsum(ef, axis=1)], axis=1)
    pos = pos0 + jnp.arange(t)
    groups = []
    for gi, win in enumerate(POOL_WINDOWS):
        lo, hi = gi * POOL_GW, (gi + 1) * POOL_GW
        wsum = (c0[:, POOL_HIST + 1:POOL_HIST + 1 + t, lo:hi]
                - c0[:, POOL_HIST + 1 - win:POOL_HIST + 1 - win + t, lo:hi])
        cnt = jnp.minimum(win, pos + 1).astype(f32)[None, :, None]
        groups.append(wsum / cnt - ef[:, POOL_HIST:, lo:hi])
    pooled = jnp.stack(groups, axis=2)
    y = jnp.einsum('btgc,gcd->btgd', pooled, w_pool.astype(f32)).reshape(b, t, W_A)
    return (y * scale.astype(f32)).astype(u.dtype), ext[:, -POOL_HIST:]


def _band_attention_prompt(q, k, v, table):
    b, t = q.shape[:2]
    nc = t // CHUNK
    nb = BAND_CHUNKS + 1
    qc, kc, vc = (a.astype(f32).reshape(b, nc, CHUNK, H_B, DH_B) for a in (q, k, v))
    pad = ((0, 0), (BAND_CHUNKS, 0), (0, 0), (0, 0), (0, 0))
    kp, vp = jnp.pad(kc, pad), jnp.pad(vc, pad)
    kband = jnp.concatenate([kp[:, j:j + nc] for j in range(nb)], axis=2)
    vband = jnp.concatenate([vp[:, j:j + nc] for j in range(nb)], axis=2)
    qi = jnp.arange(CHUNK)
    kj = jnp.arange(nb * CHUNK)
    bias = _rel_bias(table, kj[None, :] - BAND_CHUNKS * CHUNK - qi[:, None])
    kpos = (jnp.arange(nc)[:, None] - BAND_CHUNKS) * CHUNK + kj[None, :]
    s = jnp.einsum('bnqhd,bnkhd->bnhqk', qc, kband) * (DH_B ** -0.5) + bias[None, None]
    s = jnp.where((kpos >= 0)[None, :, None, None, :], s, NEG_INF)
    p = jax.nn.softmax(s, axis=-1)
    o = jnp.einsum('bnhqk,bnkhd->bnqhd', p, vband)
    return o.reshape(b, t, W_B).astype(q.dtype)


def _attention_sample(q, k, v, k_cache, v_cache, table):
    b, t = q.shape[:2]
    n_cache = k_cache.shape[2]
    kall = jnp.concatenate([k_cache.astype(f32), jnp.transpose(k, (0, 2, 1, 3)).astype(f32)], axis=2)
    vall = jnp.concatenate([v_cache.astype(f32), jnp.transpose(v, (0, 2, 1, 3)).astype(f32)], axis=2)
    bias = _rel_bias(table, jnp.arange(n_cache + t)[None, :] - n_cache - jnp.arange(t)[:, None])
    s = jnp.einsum('bqhd,bhkd->bhqk', q.astype(f32), kall) * (DH_B ** -0.5) + bias[None]
    p = jax.nn.softmax(s, axis=-1)
    o = jnp.einsum('bhqk,bhkd->bqhd', p, vall)
    return o.reshape(b, t, W_B).astype(q.dtype)


def _gated_delta_chunked(q, k, v, g, beta, s0):
    t = q.shape[1]
    cs = min(CHUNK, t)
    xs = tuple(_to_chunks(a, cs) for a in (q, k, v, g, beta))
    incl = jnp.tril(jnp.ones((cs, cs), dtype=bool))
    strict = jnp.tril(jnp.ones((cs, cs), dtype=bool), -1)
    eye = jnp.eye(cs, dtype=f32)

    def step(S, inp):
        qc, kc, vc, gc, bc = inp
        dv = vc.shape[-1]
        G = jnp.cumsum(gc, axis=-1)
        diff = G[..., :, None] - G[..., None, :]
        dec = jnp.where(incl, jnp.exp(jnp.where(incl, diff, 0.0)), 0.0)
        A = jnp.where(strict, bc[..., :, None] * jnp.einsum('bhid,bhjd->bhij', kc, kc) * dec, 0.0)
        rhs = jnp.concatenate([bc[..., None] * vc, (bc * jnp.exp(G))[..., None] * kc], axis=-1)
        sol = lax.linalg.triangular_solve(A + eye, rhs, left_side=True, lower=True, unit_diagonal=True)
        u, w = sol[..., :dv], sol[..., dv:]
        un = u - jnp.einsum('bhik,bhkv->bhiv', w, S)
        qk = jnp.einsum('bhik,bhjk->bhij', qc, kc) * dec
        o = (jnp.einsum('bhik,bhkv->bhiv', qc * jnp.exp(G)[..., None], S)
             + jnp.einsum('bhij,bhjv->bhiv', qk, un))
        gl = G[..., -1:]
        S = (jnp.exp(gl)[..., None] * S
             + jnp.einsum('bhjk,bhjv->bhkv', kc * jnp.exp(gl - G)[..., None], un))
        return S, o

    s_fin, o = lax.scan(step, s0, xs)
    return _from_chunks(o), s_fin


def _gla_chunked(q, k, v, gk, s0):
    t = q.shape[1]
    cs = min(CHUNK, t)
    xs = tuple(_to_chunks(a, cs) for a in (q, k, v, gk))
    incl = jnp.tril(jnp.ones((cs, cs), dtype=bool))[:, :, None]

    def step(S, inp):
        qc, kc, vc, gc = inp
        G = jnp.cumsum(gc, axis=2)
        diff = G[:, :, :, None, :] - G[:, :, None, :, :]
        dec = jnp.where(incl, jnp.exp(jnp.where(incl, diff, 0.0)), 0.0)
        att = jnp.einsum('bhik,bhjk,bhijk->bhij', qc, kc, dec)
        o = (jnp.einsum('bhik,bhkv->bhiv', qc * jnp.exp(G), S)
             + jnp.einsum('bhij,bhjv->bhiv', att, vc))
        gl = G[:, :, -1:, :]
        S = (jnp.exp(gl[:, :, 0, :])[..., None] * S
             + jnp.einsum('bhjk,bhjv->bhkv', kc * jnp.exp(gl - G), vc))
        return S, o

    s_fin, o = lax.scan(step, s0, xs)
    return _from_chunks(o), s_fin


def _delta_mixer(qkv, z, a, bb, conv_hist, s0, conv_w, a_log, dt_bias, norm_g):
    b, t = qkv.shape[:2]
    ext = jnp.concatenate([conv_hist.astype(qkv.dtype), qkv], axis=1)
    acc = ext[:, 0:t] * conv_w[0]
    for j in range(1, CONV_W):
        acc = acc + ext[:, j:j + t] * conv_w[j]
    c = jax.nn.silu(acc.astype(f32))
    q = _l2norm(c[..., :W_C].reshape(b, t, H_C, DK_C)) * (DK_C ** -0.5)
    k = _l2norm(c[..., W_C:2 * W_C].reshape(b, t, H_C, DK_C))
    v = c[..., 2 * W_C:].reshape(b, t, H_C, DV_C)
    beta = jax.nn.sigmoid(bb.astype(f32))
    g = -jnp.exp(a_log.astype(f32)) * jax.nn.softplus(a.astype(f32) + dt_bias.astype(f32))
    o, s_new = _gated_delta_chunked(q, k, v, g, beta, s0.astype(f32))
    o = _rmsnorm(o, norm_g) * jax.nn.silu(z.astype(f32).reshape(b, t, H_C, DV_C))
    return o.reshape(b, t, W_C).astype(qkv.dtype), ext[:, -(CONV_W - 1):], s_new


def _gla_mixer(q, k, v, gate, gk_low, s0, w_gk, b_gk, norm_g):
    b, t = q.shape[:2]
    gk = jax.nn.log_sigmoid(jnp.einsum('btr,rk->btk', gk_low.astype(f32), w_gk.astype(f32))
                            + b_gk.astype(f32)) / GLA_GATE_NORM
    qh = q.astype(f32).reshape(b, t, H_D, DK_D) * (DK_D ** -0.5)
    kh = k.astype(f32).reshape(b, t, H_D, DK_D)
    vh = v.astype(f32).reshape(b, t, H_D, DV_D)
    o, s_new = _gla_chunked(qh, kh, vh, gk.reshape(b, t, H_D, DK_D), s0.astype(f32))
    o = _rmsnorm(o, norm_g) * jax.nn.silu(gate.astype(f32).reshape(b, t, H_D, DV_D))
    return o.reshape(b, t, W_D).astype(q.dtype), s_new


def _mixers_jax(p_all, row0, b, t, pos0, pool_h, kv_h, conv_h, sd_h, sg_h, lw):
    (pool_w, pool_scale, rel_bias, conv_w, a_log, dt_bias, delta_norm_g,
     gla_w_gk, gla_b_gk, gla_norm_g) = lw
    p = p_all[row0:row0 + b * t].reshape(b, t, P_COLS)
    u_a = p[..., P_UA:P_UA + 256]
    y_a, st_pool = _pool_mixer(u_a, pool_h, pos0, pool_w, pool_scale)
    q_b, k_b, v_b = (p[..., o:o + 256].reshape(b, t, H_B, DH_B) for o in (P_QB, P_KB, P_VB))
    if kv_h is None:
        y_b = _band_attention_prompt(q_b, k_b, v_b, rel_bias)
        keep = min(ATTN_WINDOW, t)
        k_rows, v_rows = k_b[:, t - keep:], v_b[:, t - keep:]
    else:
        y_b = _attention_sample(q_b, k_b, v_b, kv_h[0], kv_h[1], rel_bias)
        k_rows, v_rows = k_b, v_b
    small = p[..., P_SMALL:P_SMALL + 24]
    y_c, st_d = _delta_call(p_all, row0, b, t, conv_h, sd_h, conv_w, a_log, dt_bias, delta_norm_g)
    st_conv = jnp.concatenate([conv_h, p[..., P_QKVC:P_QKVC + 768]], axis=1)[:, -(CONV_W - 1):]
    y_d, st_g = _gla_mixer(p[..., P_QD:P_QD + 128], p[..., P_KD:P_KD + 128], p[..., P_VD:P_VD + 256],
                           p[..., P_GD:P_GD + 256], small[..., 8:24], sg_h, gla_w_gk, gla_b_gk,
                           gla_norm_g)
    n = b * t
    ys = tuple(y.reshape(n, 256) for y in (y_a, y_b, y_c, y_d))
    return ys, (st_pool, jnp.transpose(k_rows, (0, 2, 1, 3)), jnp.transpose(v_rows, (0, 2, 1, 3)),
                st_conv, st_d, st_g)


def _mix_layer(p, b, t, pos0, hist, lw):
    (pool_w, pool_scale, rel_table, conv_w, a_log, dt_bias, delta_norm_g,
     gla_w_gk, gla_b_gk, gla_norm_g) = lw
    pool_h, conv_h, sd_h, sg_h, kv_h = hist
    y_a = _pool_call(p, 0, b, t, pos0, pool_h, pool_w, pool_scale)
    if kv_h is None:
        y_b = _band_attn_call(p, b, t, rel_table)
        keep = min(ATTN_WINDOW, t)
    else:
        y_b = _sample_attn_call(p, 0, b, t, kv_h[0], kv_h[1], rel_table)
        keep = t
    y_c, st_d = _delta_call(p, 0, b, t, conv_h, sd_h, conv_w, a_log, dt_bias, delta_norm_g)
    y_d, st_g = _gla_call(p, 0, b, t, sg_h, gla_w_gk, gla_b_gk, gla_norm_g)
    p3 = p.reshape(b, t, P_COLS)
    st_pool = jnp.concatenate([pool_h, p3[:, :, P_UA:P_UA + W_A]], axis=1)[:, -POOL_HIST:]
    st_conv = jnp.concatenate([conv_h, p3[:, :, P_QKVC:P_QKVC + QKV_C]], axis=1)[:, -(CONV_W - 1):]
    k_rows = p3[:, t - keep:, P_KB:P_KB + W_B].reshape(b, keep, H_B, DH_B)
    v_rows = p3[:, t - keep:, P_VB:P_VB + W_B].reshape(b, keep, H_B, DH_B)
    return (y_a, y_b, y_c, y_d), (st_pool, jnp.transpose(k_rows, (0, 2, 1, 3)),
                                  jnp.transpose(v_rows, (0, 2, 1, 3)), st_conv, st_d, st_g)


def kernel(x_prompt, x_sample, cache_pool, cache_attn_k, cache_attn_v, state_conv, state_delta,
           state_gla, attn_norm_g, w_in, pool_w, pool_scale, rel_bias, conv_w, a_log, dt_bias,
           delta_norm_g, gla_w_gk, gla_b_gk, gla_norm_g, w_out, mlp_norm_g, w_up, w_down,
           final_norm_g):
    bp, tp, _ = x_prompt.shape
    bs, ts, _ = x_sample.shape
    w_in_r = _reorder_w_in(w_in).astype(bf16)
    w_out_b, w_up_b, w_down_b = w_out.astype(bf16), w_up.astype(bf16), w_down.astype(bf16)
    gf = final_norm_g.reshape(1, D_MODEL)
    zero_hist = (jnp.zeros((bp, POOL_HIST, W_A), f32), jnp.zeros((bp, CONV_W - 1, QKV_C), f32),
                 jnp.zeros((bp, H_C, DK_C, DV_C), f32), jnp.zeros((bp, H_D, DK_D, DV_D), f32))
    xs = [x_prompt.reshape(bp * tp, D_MODEL), x_sample.reshape(bs * ts, D_MODEL)]
    states = ([], [])
    for l in range(DEPTH):
        lw = (pool_w[l], pool_scale[l], rel_bias[l], conv_w[l], a_log[l], dt_bias[l], delta_norm_g[l],
              gla_w_gk[l], gla_b_gk[l], gla_norm_g[l])
        hists = (zero_hist + (None,),
                 (cache_pool[l], state_conv[l], state_delta[l], state_gla[l],
                  (cache_attn_k[l], cache_attn_v[l])))
        for s, (b, t, pos0) in enumerate(((bp, tp, 0), (bs, ts, tp))):
            p = _inproj(xs[s], attn_norm_g[l].reshape(1, D_MODEL), w_in_r[l])
            ys, st = _mix_layer(p, b, t, pos0, hists[s], lw)
            xs[s] = _outmlp(xs[s], ys, w_out_b[l], mlp_norm_g[l].reshape(1, D_MODEL), w_up_b[l],
                            w_down_b[l], gf, final=(l == DEPTH - 1))
            states[s].append(st)
    y_prompt = xs[0].reshape(bp, tp, D_MODEL)
    y_sample = xs[1].reshape(bs, ts, D_MODEL)
    outs_p = tuple(jnp.stack([st[i] for st in states[0]]) for i in range(6))
    outs_s = tuple(jnp.stack([st[i] for st in states[1]]) for i in range(6))
    return (y_prompt, y_sample) + outs_p + outs_s
```

```python
import functools

import jax
import jax.numpy as jnp
import numpy as np
from jax import lax
from jax.experimental import pallas as pl
from jax.experimental.pallas import tpu as pltpu

D_MODEL = 1024
DEPTH = 4
CHUNK = 64
W_A = W_B = W_C = W_D = 256
POOL_WINDOWS = (2, 4, 8, 16)
N_POOL = 4
POOL_GW = 64
POOL_HIST = 15
H_B = 4
DH_B = 64
BAND_CHUNKS = 8
ATTN_WINDOW = 512
REL_CLIP = 256
H_C = 4
DK_C = 64
DV_C = 64
CONV_W = 4
QKV_C = 768
H_D = 4
DK_D = 32
DV_D = 64
WK_D = 128
GLA_RANK = 16
GLA_GATE_NORM = 16.0
D_FF = 4096
EPS = 1e-6
NEG_INF = -1e30

P_UA = 0
P_QB, P_KB, P_VB = 256, 512, 768
P_QKVC = 1024
P_ZC = 1792
P_VD = 2048
P_GD = 2304
P_QD = 2560
P_KD = 2688
P_SMALL = 2816
SMALL_W = 128
P_COLS = 2944

ROW_TILE = 512
FF_CHUNK = 1024
Q_BLOCK = 256
K_WINDOW = Q_BLOCK + BAND_CHUNKS * CHUNK
VMEM_LIMIT = 56 * 1024 * 1024

f32 = jnp.float32
bf16 = jnp.bfloat16


def _params(n_axes):
    return pltpu.CompilerParams(dimension_semantics=("arbitrary",) * n_axes, vmem_limit_bytes=VMEM_LIMIT)


def _dot(a, b):
    return jnp.dot(a, b, preferred_element_type=f32)


def _dot_nt(a, b):
    return lax.dot_general(a, b, (((1,), (1,)), ((), ())), preferred_element_type=f32)


def _dot_tn(a, b):
    return lax.dot_general(a, b, (((0,), (0,)), ((), ())), preferred_element_type=f32)


def _bmm(a, b):
    return jnp.einsum('nij,njk->nik', a, b, preferred_element_type=f32)


def _bmm_nt(a, b):
    return jnp.einsum('nid,njd->nij', a, b, preferred_element_type=f32)


def _bmm_tn(a, b):
    return jnp.einsum('nji,njk->nik', a, b, preferred_element_type=f32)


def _split(x):
    hi = x.astype(bf16)
    return hi, (x - hi.astype(f32)).astype(bf16)


def _dot_sel(x, sel):
    hi, lo = _split(x)
    return _dot(hi, sel) + _dot(lo, sel)


def _bmm_sel(sel, x):
    hi, lo = _split(x)
    return _bmm(sel, hi) + _bmm(sel, lo)


def _iota(shape, dim):
    return lax.broadcasted_iota(jnp.int32, shape, dim)


def _seg_ones(n_in, seg_in, n_out, seg_out):
    return (_iota((n_in, n_out), 0) // seg_in == _iota((n_in, n_out), 1) // seg_out).astype(bf16)


def _silu(x):
    return x * jax.nn.sigmoid(x)


def _rms(x):
    return x * lax.rsqrt(jnp.mean(x * x, axis=-1, keepdims=True) + EPS)


def _inproj_kernel(x_ref, g_ref, w_ref, o_ref):
    h = _rms(x_ref[...]) * g_ref[...]
    o_ref[...] = _dot(h.astype(bf16), w_ref[...])


def _inproj(x, g, w, l):
    n = x.shape[0]
    return pl.pallas_call(
        _inproj_kernel,
        out_shape=jax.ShapeDtypeStruct((n, P_COLS), f32),
        grid=(n // ROW_TILE,),
        in_specs=[
            pl.BlockSpec((ROW_TILE, D_MODEL), lambda i: (i, 0)),
            pl.BlockSpec((None, 1, D_MODEL), lambda i: (l, 0, 0)),
            pl.BlockSpec((None, D_MODEL, P_COLS), lambda i: (l, 0, 0), pipeline_mode=pl.Buffered(1)),
        ],
        out_specs=pl.BlockSpec((ROW_TILE, P_COLS), lambda i: (i, 0)),
        compiler_params=_params(1),
        name="inproj",
    )(x, g, w)


def _outmlp_kernel(x_ref, ya_ref, yb_ref, yc_ref, yd_ref, wo_ref, g2_ref, wu_ref, wd_ref, gf_ref,
                   o_ref, *, final):
    acc = None
    for k, y_ref in enumerate((ya_ref, yb_ref, yc_ref, yd_ref)):
        d = _dot(y_ref[...].astype(bf16), wo_ref[k * 256:(k + 1) * 256, :])
        acc = d if acc is None else acc + d
    x = x_ref[...] + acc
    h2 = (_rms(x) * g2_ref[...]).astype(bf16)
    acc = None
    for c in range(D_FF // FF_CHUNK):
        up = _dot(h2, wu_ref[:, c * FF_CHUNK:(c + 1) * FF_CHUNK])
        up = jnp.square(jnp.maximum(up, 0.0)).astype(bf16)
        d = _dot(up, wd_ref[c * FF_CHUNK:(c + 1) * FF_CHUNK, :])
        acc = d if acc is None else acc + d
    x = x + acc
    if final:
        x = _rms(x) * gf_ref[...]
    o_ref[...] = x


def _outmlp(x, ys, wo, g2, wu, wd, gf, l, final):
    n = x.shape[0]
    row = lambda i: (i, 0)
    layer = lambda i: (l, 0, 0)
    single = pl.Buffered(1)
    return pl.pallas_call(
        functools.partial(_outmlp_kernel, final=final),
        out_shape=jax.ShapeDtypeStruct((n, D_MODEL), f32),
        grid=(n // ROW_TILE,),
        in_specs=[pl.BlockSpec((ROW_TILE, D_MODEL), row)]
        + [pl.BlockSpec((ROW_TILE, 256), row)] * 4
        + [pl.BlockSpec((None, D_MODEL, D_MODEL), layer, pipeline_mode=single),
           pl.BlockSpec((None, 1, D_MODEL), layer),
           pl.BlockSpec((None, D_MODEL, D_FF), layer, pipeline_mode=single),
           pl.BlockSpec((None, D_FF, D_MODEL), layer, pipeline_mode=single),
           pl.BlockSpec((1, D_MODEL), lambda i: (0, 0))],
        out_specs=pl.BlockSpec((ROW_TILE, D_MODEL), row),
        compiler_params=_params(1),
        name="outmlp_final" if final else "outmlp",
    )(x, *ys, wo, g2, wu, wd, gf)


def _pool_kernel(u_ref, hist_ref, w_ref, scale_ref, y_ref, st_ref, ext_ref, *, tb, pos0):
    t_idx = pl.program_id(1)

    @pl.when(t_idx == 0)
    def _():
        ext_ref[0:8, :] = jnp.zeros((8, W_A), f32)
        ext_ref[1:16, :] = hist_ref[...]

    x = u_ref[...]
    ext_ref[16:16 + tb, :] = x
    wsum = {}
    acc = x
    for k in range(1, 16):
        acc = acc + ext_ref[16 - k:16 - k + tb, :]
        if k + 1 in POOL_WINDOWS:
            wsum[k + 1] = acc
    ext_ref[1:16, :] = ext_ref[tb + 1:tb + 16, :]
    lane_group = _iota((tb, W_A), 1) // POOL_GW
    pos = pos0 + t_idx * tb + _iota((tb, W_A), 0)
    ws, win = wsum[POOL_WINDOWS[-1]], jnp.full((tb, W_A), POOL_WINDOWS[-1], jnp.int32)
    for gi in range(N_POOL - 2, -1, -1):
        ws = jnp.where(lane_group == gi, wsum[POOL_WINDOWS[gi]], ws)
        win = jnp.where(lane_group == gi, POOL_WINDOWS[gi], win)
    cnt = jnp.minimum(win, pos + 1).astype(f32)
    pooled = ws / cnt - x
    y_ref[...] = _dot(pooled, w_ref[...]) * scale_ref[...]

    @pl.when(t_idx == pl.num_programs(1) - 1)
    def _():
        st_ref[...] = ext_ref[1:16, :]


def _pool_call(p, b, t, pos0, hist, lh, wbd, scale, l):
    tb = min(t, ROW_TILE)
    nt = t // tb
    return pl.pallas_call(
        functools.partial(_pool_kernel, tb=tb, pos0=pos0),
        out_shape=(jax.ShapeDtypeStruct((b * t, W_A), f32),
                   jax.ShapeDtypeStruct((b, POOL_HIST, W_A), f32)),
        grid=(b, nt),
        in_specs=[
            pl.BlockSpec((tb, W_A), lambda i, j: (i * nt + j, P_UA // W_A)),
            pl.BlockSpec((None, None, POOL_HIST, W_A), lambda i, j: (lh, i, 0, 0)),
            pl.BlockSpec((None, W_A, W_A), lambda i, j: (l, 0, 0)),
            pl.BlockSpec((None, 1, W_A), lambda i, j: (l, 0, 0)),
        ],
        out_specs=(pl.BlockSpec((tb, W_A), lambda i, j: (i * nt + j, 0)),
                   pl.BlockSpec((None, POOL_HIST, W_A), lambda i, j: (i, 0, 0))),
        scratch_shapes=[pltpu.VMEM((tb + 16, W_A), f32)],
        compiler_params=_params(2),
        name=f"pool_t{t}",
    )(p, hist, wbd, scale)


def _softmax_av(s_parts, v_parts):
    m = functools.reduce(jnp.maximum, [jnp.max(s, axis=-1, keepdims=True) for s in s_parts])
    e_parts = [jnp.exp(s - m) for s in s_parts]
    l = functools.reduce(jnp.add, [jnp.sum(e, axis=-1, keepdims=True) for e in e_parts])
    o = functools.reduce(jnp.add, [_dot(e, v) for e, v in zip(e_parts, v_parts)])
    return o / l


def _band_attn_kernel(q_ref, k_ref, v_ref, bias_ref, y_ref, kc_ref, vc_ref, *, t, keep):
    qb = pl.program_id(1)
    start = pl.multiple_of(jnp.maximum(qb * Q_BLOCK - BAND_CHUNKS * CHUNK, 0), Q_BLOCK)
    outs = []
    for h in range(H_B):
        lanes = slice(h * DH_B, (h + 1) * DH_B)
        q = q_ref[:, lanes] * (DH_B ** -0.5)
        k = k_ref[pl.ds(start, K_WINDOW), lanes]
        v = v_ref[pl.ds(start, K_WINDOW), lanes]
        s = _dot_nt(q, k) + bias_ref[h]
        outs.append(_softmax_av([s], [v]))
    y_ref[...] = jnp.concatenate(outs, axis=1)

    @pl.when(qb == pl.num_programs(1) - 1)
    def _():
        for h in range(H_B):
            lanes = slice(h * DH_B, (h + 1) * DH_B)
            kc_ref[h] = k_ref[t - keep:t, lanes]
            vc_ref[h] = v_ref[t - keep:t, lanes]


def _rel_bias_rows(table, nq, nk, off):
    period = nq + nk
    m = np.arange(period)
    rel = np.where(m < nk, m, m - period) - off
    row = table.astype(f32)[..., np.clip(rel, -REL_CLIP, REL_CLIP) + REL_CLIP]
    lead = table.shape[:-1]
    flat = jnp.tile(row, (1,) * len(lead) + (nq,))[..., :nq * (period - 1)]
    return flat.reshape(lead + (nq, period - 1))[..., :nk]


def _band_bias(table):
    variants = []
    qi = np.arange(Q_BLOCK)[:, None]
    kj = np.arange(K_WINDOW)[None, :]
    for v in range(3):
        dchunk = (qi + v * Q_BLOCK) // CHUNK - kj // CHUNK
        vis = (dchunk >= 0) & (dchunk <= BAND_CHUNKS)
        bias = _rel_bias_rows(table, Q_BLOCK, K_WINDOW, v * Q_BLOCK)
        variants.append(jnp.where(vis, bias, NEG_INF))
    return jnp.stack(variants, axis=1)


def _band_attn_call(p, b, t, bias, l):
    nq = t // Q_BLOCK
    keep = min(ATTN_WINDOW, t)
    cache = jax.ShapeDtypeStruct((b, H_B, keep, DH_B), f32)
    cache_spec = pl.BlockSpec((None, H_B, keep, DH_B), lambda i, j: (i, 0, 0, 0))
    return pl.pallas_call(
        functools.partial(_band_attn_kernel, t=t, keep=keep),
        out_shape=(jax.ShapeDtypeStruct((b * t, W_B), f32), cache, cache),
        grid=(b, nq),
        in_specs=[
            pl.BlockSpec((Q_BLOCK, W_B), lambda i, j: (i * nq + j, P_QB // W_B)),
            pl.BlockSpec((t, W_B), lambda i, j: (i, P_KB // W_B)),
            pl.BlockSpec((t, W_B), lambda i, j: (i, P_VB // W_B)),
            pl.BlockSpec((None, None, H_B, Q_BLOCK, K_WINDOW), lambda i, j: (l, jnp.minimum(j, 2), 0, 0, 0)),
        ],
        out_specs=(pl.BlockSpec((Q_BLOCK, W_B), lambda i, j: (i * nq + j, 0)), cache_spec, cache_spec),
        compiler_params=_params(2),
        name="band_attn",
    )(p, p, p, bias)


def _sample_attn_kernel(q_ref, k_ref, v_ref, kc_ref, vc_ref, bias_ref, y_ref, kn_ref, vn_ref, *, n_cache):
    outs = []
    for h in range(H_B):
        lanes = slice(h * DH_B, (h + 1) * DH_B)
        q = q_ref[:, lanes] * (DH_B ** -0.5)
        s_old = _dot_nt(q, kc_ref[h]) + bias_ref[h, :, 0:n_cache]
        s_new = _dot_nt(q, k_ref[:, lanes]) + bias_ref[h, :, n_cache:]
        outs.append(_softmax_av([s_old, s_new], [vc_ref[h], v_ref[:, lanes]]))
        kn_ref[h] = k_ref[:, lanes]
        vn_ref[h] = v_ref[:, lanes]
    y_ref[...] = jnp.concatenate(outs, axis=1)


def _sample_attn_call(p, b, t, k_cache, v_cache, bias, l):
    n_cache = k_cache.shape[3]
    cache_spec = pl.BlockSpec((None, None, H_B, n_cache, DH_B), lambda i: (l, i, 0, 0, 0))
    new = jax.ShapeDtypeStruct((b, H_B, t, DH_B), f32)
    new_spec = pl.BlockSpec((None, H_B, t, DH_B), lambda i: (i, 0, 0, 0))
    return pl.pallas_call(
        functools.partial(_sample_attn_kernel, n_cache=n_cache),
        out_shape=(jax.ShapeDtypeStruct((b * t, W_B), f32), new, new),
        grid=(b,),
        in_specs=[
            pl.BlockSpec((t, W_B), lambda i: (i, P_QB // W_B)),
            pl.BlockSpec((t, W_B), lambda i: (i, P_KB // W_B)),
            pl.BlockSpec((t, W_B), lambda i: (i, P_VB // W_B)),
            cache_spec, cache_spec,
            pl.BlockSpec((None, H_B, t, n_cache + t), lambda i: (l, 0, 0, 0)),
        ],
        out_specs=(pl.BlockSpec((t, W_B), lambda i: (i, 0)), new_spec, new_spec),
        compiler_params=_params(1),
        name="sample_attn",
    )(p, p, p, k_cache, v_cache, bias)


def _unit_lower_inverse(a):
    n, c, _ = a.shape
    ri, ci = _iota((n, c, c), 1), _iota((n, c, c), 2)
    d = jnp.where(ri == ci, 1.0, 0.0) - jnp.where((ri // 2 == ci // 2) & (ri > ci), a, 0.0)
    s = 2
    while s < c:
        e = jnp.where((ri // (2 * s) == ci // (2 * s)) & (ri % (2 * s) >= s) & (ci % (2 * s) < s), a, 0.0)
        d = d - _bmm(d, _bmm(e, d))
        s *= 2
    return d


def _delta_kernel(q_ref, k_ref, v_ref, z_ref, sm_ref, hist_ref, s0_ref, cw_ref, alog_ref, dtb_ref, ng_ref,
                  y_ref, sout_ref, cout_ref, ext_ref, s_ref, *, ns, tl, c, carry):
    t_idx = pl.program_id(1)
    tb = ns * tl
    nch = tb // c
    nb = H_C * nch

    def heads(x):
        return jnp.concatenate(
            [x[:, h * 64:(h + 1) * 64].reshape(nch, c, 64) for h in range(H_C)], axis=0)

    if carry:
        @pl.when(t_idx == 0)
        def _():
            s_ref[...] = s0_ref[0]
            for s in range(3):
                ext_ref[s, :, 0:8, :] = jnp.zeros((ns, 8, 256), f32)
                ext_ref[s, :, 5:8, :] = hist_ref[:, :, s * 256:(s + 1) * 256]
    else:
        for s in range(3):
            ext_ref[s, :, 5:8, :] = hist_ref[:, :, s * 256:(s + 1) * 256]

    parts = []
    for s, ref in enumerate((q_ref, k_ref, v_ref)):
        ext_ref[s, :, 8:8 + tl, :] = ref[...].reshape(ns, tl, 256)
        acc = ext_ref[s, :, 5:5 + tl, :] * cw_ref[0:1, s * 256:(s + 1) * 256]
        for j in range(1, CONV_W):
            acc = acc + ext_ref[s, :, 5 + j:5 + j + tl, :] * cw_ref[j:j + 1, s * 256:(s + 1) * 256]
        parts.append(_silu(acc).reshape(tb, 256))
        last = ext_ref[s, :, tl + 5:tl + 8, :]
        if carry:
            ext_ref[s, :, 5:8, :] = last
        cout_ref[:, :, s * 256:(s + 1) * 256] = last
    cq, ck, cv = parts
    seg = _seg_ones(256, DK_C, 256, DK_C)
    qn = cq * lax.rsqrt(_dot_sel(cq * cq, seg) + EPS) * (DK_C ** -0.5)
    kn = ck * lax.rsqrt(_dot_sel(ck * ck, seg) + EPS)

    sm = sm_ref[...]
    g = -jnp.exp(alog_ref[...]) * jax.nn.softplus(sm + dtb_ref[...])
    beta = jax.nn.sigmoid(sm)
    ltri = (_iota((nch, c, c), 1) >= _iota((nch, c, c), 2)).astype(bf16)
    gcum = _bmm_sel(ltri, g.reshape(nch, c, SMALL_W)).reshape(tb, SMALL_W)
    lane_head = _iota((SMALL_W, 256), 1) // 64
    gb = _dot_sel(gcum, (_iota((SMALL_W, 256), 0) == lane_head).astype(bf16))
    betab = _dot_sel(beta, (_iota((SMALL_W, 256), 0) == lane_head + H_C).astype(bf16))
    eg = jnp.exp(gb)
    gl = jnp.broadcast_to(gb.reshape(nch, c, 256)[:, c - 1:c, :], (nch, c, 256)).reshape(tb, 256)

    k3, q3 = heads(kn), heads(qn)
    gcb = heads(gb)[:, :, :c]
    grb = jnp.swapaxes(gcb, 1, 2)
    ri, ci = _iota((nb, c, c), 1), _iota((nb, c, c), 2)
    incl = ri >= ci
    dec = jnp.where(incl, jnp.exp(jnp.where(incl, gcb - grb, 0.0)), 0.0)
    a = jnp.where(ri > ci, heads(betab)[:, :, :c] * _bmm_nt(k3, k3) * dec, 0.0)
    rhs = jnp.concatenate([heads(betab * cv), heads(betab * eg * kn)], axis=-1)
    sol = _bmm(_unit_lower_inverse(a), rhs)
    u, w = sol[:, :, :64], sol[:, :, 64:]
    qk = _bmm_nt(q3, k3) * dec
    qt3 = heads(qn * eg)
    kt3 = heads(kn * jnp.exp(gl - gb))
    egl = heads(jnp.exp(gl))[:, 0:1, :]

    if carry:
        o_heads = []
        for h in range(H_C):
            st = s_ref[h]
            o_chunks = []
            for ic in range(nch):
                n = h * nch + ic
                un = u[n] - _dot(w[n], st)
                o_chunks.append(_dot(qt3[n], st) + _dot(qk[n], un))
                st = egl[n] * st + _dot_tn(kt3[n], un)
            s_ref[h] = st
            o_heads.append(jnp.concatenate(o_chunks, axis=0))
        o = jnp.concatenate(o_heads, axis=1)

        @pl.when(t_idx == pl.num_programs(1) - 1)
        def _():
            sout_ref[0] = s_ref[...]
    else:
        st = jnp.concatenate([s0_ref[:, h] for h in range(H_C)], axis=0)
        un = u - _bmm(w, st)
        o3 = _bmm(qt3, st) + _bmm(qk, un)
        st = egl * st + _bmm_tn(kt3, un)
        for h in range(H_C):
            sout_ref[:, h] = st[h * nch:(h + 1) * nch]
        o = jnp.concatenate([o3[h * nch:(h + 1) * nch].reshape(tb, 64) for h in range(H_C)], axis=1)

    o = o * lax.rsqrt(_dot_sel(o * o, seg) * (1.0 / DV_C) + EPS) * ng_ref[...]
    y_ref[...] = o * _silu(z_ref[...])


def _delta_call(p, b, t, hist, s0, lh, conv_w, alog, dtb, ng, l):
    c = min(CHUNK, t)
    carry = t > c
    ns, tl = (1, min(t, ROW_TILE)) if carry else (ROW_TILE // (2 * t), t)
    tb = ns * tl
    nt = t // tl

    def col(cb):
        return lambda i, j: (i * nt + j, cb)

    layer = lambda i, j: (l, 0, 0)
    state_spec = pl.BlockSpec((ns, H_C, DK_C, DV_C), lambda i, j: (i, 0, 0, 0))
    return pl.pallas_call(
        functools.partial(_delta_kernel, ns=ns, tl=tl, c=c, carry=carry),
        out_shape=(jax.ShapeDtypeStruct((b * t, W_C), f32),
                   jax.ShapeDtypeStruct((b, H_C, DK_C, DV_C), f32),
                   jax.ShapeDtypeStruct((b, CONV_W - 1, QKV_C), f32)),
        grid=(b // ns, nt),
        in_specs=[
            pl.BlockSpec((tb, 256), col(P_QKVC // 256)),
            pl.BlockSpec((tb, 256), col(P_QKVC // 256 + 1)),
            pl.BlockSpec((tb, 256), col(P_QKVC // 256 + 2)),
            pl.BlockSpec((tb, 256), col(P_ZC // 256)),
            pl.BlockSpec((tb, SMALL_W), col(P_SMALL // SMALL_W)),
            pl.BlockSpec((None, ns, CONV_W - 1, QKV_C), lambda i, j: (lh, i, 0, 0)),
            pl.BlockSpec((None, ns, H_C, DK_C, DV_C), lambda i, j: (lh, i, 0, 0, 0)),
            pl.BlockSpec((None, CONV_W, QKV_C), layer),
            pl.BlockSpec((None, 1, SMALL_W), layer),
            pl.BlockSpec((None, 1, SMALL_W), layer),
            pl.BlockSpec((None, 1, W_C), layer),
        ],
        out_specs=(pl.BlockSpec((tb, W_C), lambda i, j: (i * nt + j, 0)),
                   state_spec,
                   pl.BlockSpec((ns, CONV_W - 1, QKV_C), lambda i, j: (i, 0, 0))),
        scratch_shapes=[
            pltpu.VMEM((3, ns, tl + 8, 256), f32),
            pltpu.VMEM((H_C, DK_C, DV_C), f32),
        ],
        compiler_params=_params(2),
        name=f"delta_t{t}",
    )(p, p, p, p, p, hist, s0, conv_w, alog, dtb, ng)


def _gla_kernel(q_ref, k_ref, v_ref, gate_ref, sm_ref, s0_ref, wgk_ref, bgk_ref, ng_ref,
                y_ref, sout_ref, qs_ref, g_ref, o_ref, s_ref, *, tb, c):
    t_idx = pl.program_id(1)
    blk = _iota((WK_D, W_D), 0) // DK_D == _iota((WK_D, W_D), 1) // DV_D
    seg_kv = blk.astype(bf16)

    @pl.when(t_idx == 0)
    def _():
        s_ref[...] = jnp.zeros((WK_D, W_D), f32)
        for h in range(H_D):
            s_ref[h * DK_D:(h + 1) * DK_D, h * DV_D:(h + 1) * DV_D] = s0_ref[h]

    gk = jax.nn.log_sigmoid(_dot(sm_ref[...], wgk_ref[...]) + bgk_ref[...]) * (1.0 / GLA_GATE_NORM)
    g_ref[...] = gk
    qs_ref[...] = q_ref[...] * (DK_D ** -0.5)
    ltri = (_iota((c, c), 0) >= _iota((c, c), 1)).astype(bf16)
    row8 = _iota((8, WK_D), 0)

    def chunk(ic, carry):
        rows = pl.ds(pl.multiple_of(ic * c, c), c)
        g_hi, g_lo = _split(g_ref[rows, :])
        g = _dot(ltri, g_hi) + _dot(ltri, g_lo)
        q, k, v = qs_ref[rows, :], k_ref[rows, :], v_ref[rows, :]
        st = s_ref[...]
        o = _dot(q * jnp.exp(g), st)
        for jb in range(c // 8):
            lo = 8 * jb
            m = c - lo
            e = []
            for j in range(lo, lo + 8):
                ej = jnp.exp(g[lo:] - g[j:j + 1]) * (q[lo:] * k[j:j + 1])
                first = jnp.where(row8 >= j - lo, ej[:8], 0.0)
                e.append(first if m == 8 else jnp.concatenate([first, ej[8:]], axis=0))
            att = _dot(jnp.concatenate(e, axis=0).astype(bf16), seg_kv)
            upd = att[0:m] * v[lo:lo + 1]
            for jj in range(1, 8):
                upd = upd + att[jj * m:(jj + 1) * m] * v[lo + jj:lo + jj + 1]
            o = o + upd if lo == 0 else jnp.concatenate([o[:lo], o[lo:] + upd], axis=0)
        o_ref[rows, :] = o
        gl = g[c - 1:c]
        egl_col = jnp.exp(g.T[:, c - 1:c])
        s_new = egl_col * st + _dot_tn(k * jnp.exp(gl - g), v)
        s_ref[...] = jnp.where(blk, s_new, 0.0)
        return carry

    lax.fori_loop(0, tb // c, chunk, 0)

    o = o_ref[...]
    o = o * lax.rsqrt(_dot_sel(o * o, _seg_ones(W_D, DV_D, W_D, DV_D)) * (1.0 / DV_D) + EPS) * ng_ref[...]
    y_ref[...] = o * _silu(gate_ref[...])

    @pl.when(t_idx == pl.num_programs(1) - 1)
    def _():
        for h in range(H_D):
            sout_ref[h] = s_ref[h * DK_D:(h + 1) * DK_D, h * DV_D:(h + 1) * DV_D]


def _gla_call(p, b, t, s0, lh, wgk, bgk, ng, l):
    tb = min(t, ROW_TILE)
    c = min(CHUNK, t)
    nt = t // tb

    def col(cb):
        return lambda i, j: (i * nt + j, cb)

    layer = lambda i, j: (l, 0, 0)
    return pl.pallas_call(
        functools.partial(_gla_kernel, tb=tb, c=c),
        out_shape=(jax.ShapeDtypeStruct((b * t, W_D), f32),
                   jax.ShapeDtypeStruct((b, H_D, DK_D, DV_D), f32)),
        grid=(b, nt),
        in_specs=[
            pl.BlockSpec((tb, WK_D), col(P_QD // WK_D)),
            pl.BlockSpec((tb, WK_D), col(P_KD // WK_D)),
            pl.BlockSpec((tb, W_D), col(P_VD // W_D)),
            pl.BlockSpec((tb, W_D), col(P_GD // W_D)),
            pl.BlockSpec((tb, SMALL_W), col(P_SMALL // SMALL_W)),
            pl.BlockSpec((None, None, H_D, DK_D, DV_D), lambda i, j: (lh, i, 0, 0, 0)),
            pl.BlockSpec((None, SMALL_W, WK_D), layer),
            pl.BlockSpec((None, 1, WK_D), layer),
            pl.BlockSpec((None, 1, W_D), layer),
        ],
        out_specs=(pl.BlockSpec((tb, W_D), lambda i, j: (i * nt + j, 0)),
                   pl.BlockSpec((None, H_D, DK_D, DV_D), lambda i, j: (i, 0, 0, 0))),
        scratch_shapes=[
            pltpu.VMEM((tb, WK_D), f32), pltpu.VMEM((tb, WK_D), f32),
            pltpu.VMEM((tb, W_D), f32),
            pltpu.VMEM((WK_D, W_D), f32),
        ],
        compiler_params=_params(2),
        name=f"gla_t{t}",
    )(p, p, p, p, p, s0, wgk, bgk, ng)


def _reorder_w_in(w_in):
    pad = jnp.zeros(w_in.shape[:-1] + (P_COLS - P_SMALL - 2 * H_C - GLA_RANK,), w_in.dtype)
    return jnp.concatenate([
        w_in[..., 0:2048],
        w_in[..., 2312:2568],
        w_in[..., 2568:2824],
        w_in[..., 2056:2184],
        w_in[..., 2184:2312],
        w_in[..., 2048:2056],
        w_in[..., 2824:2840],
        pad], axis=-1)


def _lane_pad(x, width):
    return jnp.pad(x, ((0, 0), (0, width - x.shape[-1])))[:, None, :]


def kernel(x_prompt, x_sample, cache_pool, cache_attn_k, cache_attn_v, state_conv, state_delta,
           state_gla, attn_norm_g, w_in, pool_w, pool_scale, rel_bias, conv_w, a_log, dt_bias,
           delta_norm_g, gla_w_gk, gla_b_gk, gla_norm_g, w_out, mlp_norm_g, w_up, w_down,
           final_norm_g):
    bp, tp, _ = x_prompt.shape
    bs, ts, _ = x_sample.shape
    w_in_r = _reorder_w_in(w_in).astype(bf16)
    w_out_b, w_up_b, w_down_b = w_out.astype(bf16), w_up.astype(bf16), w_down.astype(bf16)
    g1 = attn_norm_g[:, None, :]
    g2 = mlp_norm_g[:, None, :]
    gf = final_norm_g.reshape(1, D_MODEL)
    pool_wbd = jnp.zeros((DEPTH, W_A, W_A), f32)
    for gi in range(N_POOL):
        sl = slice(gi * POOL_GW, (gi + 1) * POOL_GW)
        pool_wbd = pool_wbd.at[:, sl, sl].set(pool_w[:, gi])
    pool_sc = pool_scale[:, None, :]
    band_bias = _band_bias(rel_bias)
    n_cache = cache_attn_k.shape[3]
    sample_bias = _rel_bias_rows(rel_bias, ts, n_cache + ts, n_cache)
    alog = _lane_pad(a_log, SMALL_W)
    dtb = _lane_pad(dt_bias, SMALL_W)
    ng_c = jnp.tile(delta_norm_g, (1, H_C))[:, None, :]
    wgk = jnp.zeros((DEPTH, SMALL_W, WK_D), f32).at[:, 2 * H_C:2 * H_C + GLA_RANK].set(gla_w_gk)
    bgk = gla_b_gk[:, None, :]
    ng_d = jnp.tile(gla_norm_g, (1, H_D))[:, None, :]
    zeros = (jnp.zeros((1, bp, POOL_HIST, W_A), f32), jnp.zeros((1, bp, CONV_W - 1, QKV_C), f32),
             jnp.zeros((1, bp, H_C, DK_C, DV_C), f32), jnp.zeros((1, bp, H_D, DK_D, DV_D), f32))
    carried = (cache_pool, state_conv, state_delta, state_gla)

    xs = [x_prompt.reshape(bp * tp, D_MODEL), x_sample.reshape(bs * ts, D_MODEL)]
    states = ([], [])
    for l in range(DEPTH):
        for s, (b, t, pos0) in enumerate(((bp, tp, 0), (bs, ts, tp))):
            pool_h, conv_h, sd_h, sg_h = zeros if s == 0 else carried
            lh = 0 if s == 0 else l
            p = _inproj(xs[s], g1, w_in_r, l)
            y_a, st_pool = _pool_call(p, b, t, pos0, pool_h, lh, pool_wbd, pool_sc, l)
            if s == 0:
                y_b, st_k, st_v = _band_attn_call(p, b, t, band_bias, l)
            else:
                y_b, st_k, st_v = _sample_attn_call(p, b, t, cache_attn_k, cache_attn_v, sample_bias, l)
            y_c, st_d, st_conv = _delta_call(p, b, t, conv_h, sd_h, lh, conv_w, alog, dtb, ng_c, l)
            y_d, st_g = _gla_call(p, b, t, sg_h, lh, wgk, bgk, ng_d, l)
            xs[s] = _outmlp(xs[s], (y_a, y_b, y_c, y_d), w_out_b, g2, w_up_b, w_down_b, gf, l,
                            final=(l == DEPTH - 1))
            states[s].append((st_pool, st_k, st_v, st_conv, st_d, st_g))
    y_prompt = xs[0].reshape(bp, tp, D_MODEL)
    y_sample = xs[1].reshape(bs, ts, D_MODEL)
    outs_p = tuple(jnp.stack([st[i] for st in states[0]]) for i in range(6))
    outs_s = tuple(jnp.stack([st[i] for st in states[1]]) for i in range(6))
    return (y_prompt, y_sample) + outs_p + outs_s
```

```python
import functools

import jax
import jax.numpy as jnp
import numpy as np
from jax import lax
from jax.experimental import pallas as pl
from jax.experimental.pallas import tpu as pltpu

D_MODEL = 1024
DEPTH = 4
CHUNK = 64
W_A = W_B = W_C = W_D = 256
POOL_WINDOWS = (2, 4, 8, 16)
N_POOL = 4
POOL_GW = 64
POOL_HIST = 15
H_B = 4
DH_B = 64
BAND_CHUNKS = 8
ATTN_WINDOW = 512
REL_CLIP = 256
H_C = 4
DK_C = 64
DV_C = 64
CONV_W = 4
QKV_C = 768
H_D = 4
DK_D = 32
DV_D = 64
WK_D = 128
GLA_RANK = 16
GLA_GATE_NORM = 16.0
D_FF = 4096
EPS = 1e-6
NEG_INF = -1e30

P_UA = 0
P_QB, P_KB, P_VB = 256, 512, 768
P_QKVC = 1024
P_ZC = 1792
P_VD = 2048
P_GD = 2304
P_QD = 2560
P_KD = 2688
P_SMALL = 2816
SMALL_W = 128
P_COLS = 2944

ROW_TILE = 512
FF_CHUNK = 1024
POOL_TILE = 2048
DELTA_ROWS = 1024
DELTA_SEQS = 4
Q_BLOCK = 256
K_WINDOW = Q_BLOCK + BAND_CHUNKS * CHUNK
VMEM_LIMIT = 56 * 1024 * 1024

f32 = jnp.float32
bf16 = jnp.bfloat16


def _params(n_axes):
    return pltpu.CompilerParams(dimension_semantics=("arbitrary",) * n_axes, vmem_limit_bytes=VMEM_LIMIT)


def _dot(a, b):
    return jnp.dot(a, b, preferred_element_type=f32)


def _dot_nt(a, b):
    return lax.dot_general(a, b, (((1,), (1,)), ((), ())), preferred_element_type=f32)


def _dot_tn(a, b):
    return lax.dot_general(a, b, (((0,), (0,)), ((), ())), preferred_element_type=f32)


def _bmm(a, b):
    return jnp.einsum('nij,njk->nik', a, b, preferred_element_type=f32)


def _bmm_nt(a, b):
    return jnp.einsum('nid,njd->nij', a, b, preferred_element_type=f32)


def _bmm_tn(a, b):
    return jnp.einsum('nji,njk->nik', a, b, preferred_element_type=f32)


def _split(x):
    hi = x.astype(bf16)
    return hi, (x - hi.astype(f32)).astype(bf16)


def _dot_sel(x, sel):
    hi, lo = _split(x)
    return _dot(hi, sel) + _dot(lo, sel)


def _bmm_sel(sel, x):
    hi, lo = _split(x)
    return _bmm(sel, hi) + _bmm(sel, lo)


def _iota(shape, dim):
    return lax.broadcasted_iota(jnp.int32, shape, dim)


def _seg_ones(n_in, seg_in, n_out, seg_out):
    return (_iota((n_in, n_out), 0) // seg_in == _iota((n_in, n_out), 1) // seg_out).astype(bf16)


def _silu(x):
    return x * jax.nn.sigmoid(x)


def _rms(x):
    return x * lax.rsqrt(jnp.mean(x * x, axis=-1, keepdims=True) + EPS)


def _inproj_kernel(x_ref, g_ref, w_ref, o_ref):
    h = _rms(x_ref[...]) * g_ref[...]
    o_ref[...] = _dot(h.astype(bf16), w_ref[...])


def _inproj(x, g, w, l):
    n = x.shape[0]
    return pl.pallas_call(
        _inproj_kernel,
        out_shape=jax.ShapeDtypeStruct((n, P_COLS), f32),
        grid=(n // ROW_TILE,),
        in_specs=[
            pl.BlockSpec((ROW_TILE, D_MODEL), lambda i: (i, 0)),
            pl.BlockSpec((None, 1, D_MODEL), lambda i: (l, 0, 0)),
            pl.BlockSpec((None, D_MODEL, P_COLS), lambda i: (l, 0, 0), pipeline_mode=pl.Buffered(1)),
        ],
        out_specs=pl.BlockSpec((ROW_TILE, P_COLS), lambda i: (i, 0)),
        compiler_params=_params(1),
        name="inproj",
    )(x, g, w)


def _outmlp_kernel(x_ref, ya_ref, yb_ref, yc_ref, yd_ref, wo_ref, g2_ref, wu_ref, wd_ref, gf_ref,
                   o_ref, *, final):
    acc = None
    for k, y_ref in enumerate((ya_ref, yb_ref, yc_ref, yd_ref)):
        d = _dot(y_ref[...].astype(bf16), wo_ref[k * 256:(k + 1) * 256, :])
        acc = d if acc is None else acc + d
    x = x_ref[...] + acc
    h2 = (_rms(x) * g2_ref[...]).astype(bf16)
    acc = None
    for c in range(D_FF // FF_CHUNK):
        up = _dot(h2, wu_ref[:, c * FF_CHUNK:(c + 1) * FF_CHUNK])
        up = jnp.square(jnp.maximum(up, 0.0)).astype(bf16)
        d = _dot(up, wd_ref[c * FF_CHUNK:(c + 1) * FF_CHUNK, :])
        acc = d if acc is None else acc + d
    x = x + acc
    if final:
        x = _rms(x) * gf_ref[...]
    o_ref[...] = x


def _outmlp(x, ys, wo, g2, wu, wd, gf, l, final):
    n = x.shape[0]
    row = lambda i: (i, 0)
    layer = lambda i: (l, 0, 0)
    single = pl.Buffered(1)
    return pl.pallas_call(
        functools.partial(_outmlp_kernel, final=final),
        out_shape=jax.ShapeDtypeStruct((n, D_MODEL), f32),
        grid=(n // ROW_TILE,),
        in_specs=[pl.BlockSpec((ROW_TILE, D_MODEL), row)]
        + [pl.BlockSpec((ROW_TILE, 256), row)] * 4
        + [pl.BlockSpec((None, D_MODEL, D_MODEL), layer, pipeline_mode=single),
           pl.BlockSpec((None, 1, D_MODEL), layer),
           pl.BlockSpec((None, D_MODEL, D_FF), layer, pipeline_mode=single),
           pl.BlockSpec((None, D_FF, D_MODEL), layer, pipeline_mode=single),
           pl.BlockSpec((1, D_MODEL), lambda i: (0, 0))],
        out_specs=pl.BlockSpec((ROW_TILE, D_MODEL), row),
        compiler_params=_params(1),
        name="outmlp_final" if final else "outmlp",
    )(x, *ys, wo, g2, wu, wd, gf)


def _pool_kernel(u_ref, hist_ref, w_ref, scale_ref, y_ref, st_ref, ext_ref, *, tb, pos0):
    t_idx = pl.program_id(1)

    @pl.when(t_idx == 0)
    def _():
        ext_ref[0:8, :] = jnp.zeros((8, W_A), f32)
        ext_ref[1:16, :] = hist_ref[...]

    x = u_ref[...]
    ext_ref[16:16 + tb, :] = x
    wsum = {}
    acc = x
    for k in range(1, 16):
        acc = acc + ext_ref[16 - k:16 - k + tb, :]
        if k + 1 in POOL_WINDOWS:
            wsum[k + 1] = acc
    ext_ref[1:16, :] = ext_ref[tb + 1:tb + 16, :]
    lane_group = _iota((tb, W_A), 1) // POOL_GW
    pos = pos0 + t_idx * tb + _iota((tb, W_A), 0)
    ws, win = wsum[POOL_WINDOWS[-1]], jnp.full((tb, W_A), POOL_WINDOWS[-1], jnp.int32)
    for gi in range(N_POOL - 2, -1, -1):
        ws = jnp.where(lane_group == gi, wsum[POOL_WINDOWS[gi]], ws)
        win = jnp.where(lane_group == gi, POOL_WINDOWS[gi], win)
    cnt = jnp.minimum(win, pos + 1).astype(f32)
    pooled = ws / cnt - x
    y_ref[...] = _dot(pooled, w_ref[...]) * scale_ref[...]

    @pl.when(t_idx == pl.num_programs(1) - 1)
    def _():
        st_ref[...] = ext_ref[1:16, :]


def _pool_call(p, b, t, pos0, hist, lh, wbd, scale, l):
    tb = min(t, POOL_TILE)
    nt = t // tb
    return pl.pallas_call(
        functools.partial(_pool_kernel, tb=tb, pos0=pos0),
        out_shape=(jax.ShapeDtypeStruct((b * t, W_A), f32),
                   jax.ShapeDtypeStruct((b, POOL_HIST, W_A), f32)),
        grid=(b, nt),
        in_specs=[
            pl.BlockSpec((tb, W_A), lambda i, j: (i * nt + j, P_UA // W_A)),
            pl.BlockSpec((None, None, POOL_HIST, W_A), lambda i, j: (lh, i, 0, 0)),
            pl.BlockSpec((None, W_A, W_A), lambda i, j: (l, 0, 0)),
            pl.BlockSpec((None, 1, W_A), lambda i, j: (l, 0, 0)),
        ],
        out_specs=(pl.BlockSpec((tb, W_A), lambda i, j: (i * nt + j, 0)),
                   pl.BlockSpec((None, POOL_HIST, W_A), lambda i, j: (i, 0, 0))),
        scratch_shapes=[pltpu.VMEM((tb + 16, W_A), f32)],
        compiler_params=_params(2),
        name=f"pool_t{t}",
    )(p, hist, wbd, scale)


def _softmax_av(s_parts, av_fns):
    m = functools.reduce(jnp.maximum, [jnp.max(s, axis=-1, keepdims=True) for s in s_parts])
    e_parts = [jnp.exp(s - m) for s in s_parts]
    l = functools.reduce(jnp.add, [jnp.sum(e, axis=-1, keepdims=True) for e in e_parts])
    o = functools.reduce(jnp.add, [av(e) for e, av in zip(e_parts, av_fns)])
    return o / l


BIAS_PERIOD = 2 * K_WINDOW


def _band_attn_kernel(q_ref, k_ref, v_ref, wrow_ref, y_ref, kc_ref, vc_ref, bias_ref, *, t, keep):
    qb = pl.program_id(1)

    @pl.when((pl.program_id(0) == 0) & (qb == 0))
    def _():
        qi = _iota((Q_BLOCK, K_WINDOW), 0)
        kj = _iota((Q_BLOCK, K_WINDOW), 1)
        for h in range(H_B):
            base = pltpu.roll(jnp.broadcast_to(wrow_ref[h], (Q_BLOCK, BIAS_PERIOD)), 0, 1,
                              stride=1, stride_axis=0)
            for v in range(3):
                dchunk = (qi + v * Q_BLOCK) // CHUNK - kj // CHUNK
                vis = (dchunk >= 0) & (dchunk <= BAND_CHUNKS)
                lo = (2 - v) * Q_BLOCK
                bias_ref[v, h] = jnp.where(vis, base[:, lo:lo + K_WINDOW], NEG_INF)

    start = pl.multiple_of(jnp.maximum(qb * Q_BLOCK - BAND_CHUNKS * CHUNK, 0), Q_BLOCK)
    variant = jnp.minimum(qb, 2)
    outs = []
    for h in range(H_B):
        lanes = slice(h * DH_B, (h + 1) * DH_B)
        q = q_ref[:, lanes] * (DH_B ** -0.5)
        k = k_ref[pl.ds(start, K_WINDOW), lanes]
        v = v_ref[pl.ds(start, K_WINDOW), lanes]
        s = _dot_nt(q, k) + bias_ref[variant, h]
        outs.append(_softmax_av([s], [lambda e, v=v: _dot(e, v)]))
    y_ref[...] = jnp.concatenate(outs, axis=1)

    @pl.when(qb == pl.num_programs(1) - 1)
    def _():
        for h in range(H_B):
            lanes = slice(h * DH_B, (h + 1) * DH_B)
            kc_ref[h] = k_ref[t - keep:t, lanes]
            vc_ref[h] = v_ref[t - keep:t, lanes]


def _rel_bias_rows(table, nq, nk, off):
    period = nq + nk
    m = np.arange(period)
    rel = np.where(m < nk, m, m - period) - off
    row = table.astype(f32)[..., np.clip(rel, -REL_CLIP, REL_CLIP) + REL_CLIP]
    lead = table.shape[:-1]
    flat = jnp.tile(row, (1,) * len(lead) + (nq,))[..., :nq * (period - 1)]
    return flat.reshape(lead + (nq, period - 1))[..., :nk]


def _band_bias_row(table):
    m = np.arange(BIAS_PERIOD)
    e = np.where(m < 2 * Q_BLOCK + K_WINDOW, m, m - BIAS_PERIOD)
    idx = np.clip(e - 2 * Q_BLOCK, -REL_CLIP, REL_CLIP) + REL_CLIP
    return table.astype(f32)[..., idx][:, :, None, :]


def _band_attn_call(p, b, t, wrow, l):
    nq = t // Q_BLOCK
    keep = min(ATTN_WINDOW, t)
    cache = jax.ShapeDtypeStruct((b, H_B, keep, DH_B), f32)
    cache_spec = pl.BlockSpec((None, H_B, keep, DH_B), lambda i, j: (i, 0, 0, 0))
    return pl.pallas_call(
        functools.partial(_band_attn_kernel, t=t, keep=keep),
        out_shape=(jax.ShapeDtypeStruct((b * t, W_B), f32), cache, cache),
        grid=(b, nq),
        in_specs=[
            pl.BlockSpec((Q_BLOCK, W_B), lambda i, j: (i * nq + j, P_QB // W_B)),
            pl.BlockSpec((t, W_B), lambda i, j: (i, P_KB // W_B)),
            pl.BlockSpec((t, W_B), lambda i, j: (i, P_VB // W_B)),
            pl.BlockSpec((None, H_B, 1, BIAS_PERIOD), lambda i, j: (l, 0, 0, 0)),
        ],
        out_specs=(pl.BlockSpec((Q_BLOCK, W_B), lambda i, j: (i * nq + j, 0)), cache_spec, cache_spec),
        scratch_shapes=[pltpu.VMEM((3, H_B, Q_BLOCK, K_WINDOW), f32)],
        compiler_params=_params(2),
        name="band_attn",
    )(p, p, p, wrow)


def _sample_attn_kernel(q_ref, k_ref, v_ref, kct_ref, vct_ref, bias_ref, y_ref, kn_ref, vn_ref, *, n_cache):
    outs = []
    for h in range(H_B):
        lanes = slice(h * DH_B, (h + 1) * DH_B)
        q = q_ref[:, lanes] * (DH_B ** -0.5)
        k_new, v_new = k_ref[:, lanes], v_ref[:, lanes]
        s_old = _dot(q, kct_ref[h]) + bias_ref[h, :, 0:n_cache]
        s_new = _dot_nt(q, k_new) + bias_ref[h, :, n_cache:]
        outs.append(_softmax_av([s_old, s_new],
                                [lambda e, h=h: _dot_nt(e, vct_ref[h]), lambda e, v=v_new: _dot(e, v)]))
        kn_ref[h] = k_new
        vn_ref[h] = v_new
    y_ref[...] = jnp.concatenate(outs, axis=1)


def _sample_attn_call(p, b, t, k_cache_t, v_cache_t, bias, l):
    n_cache = k_cache_t.shape[4]
    cache_spec = pl.BlockSpec((None, None, H_B, DH_B, n_cache), lambda i: (l, i, 0, 0, 0))
    new = jax.ShapeDtypeStruct((b, H_B, t, DH_B), f32)
    new_spec = pl.BlockSpec((None, H_B, t, DH_B), lambda i: (i, 0, 0, 0))
    return pl.pallas_call(
        functools.partial(_sample_attn_kernel, n_cache=n_cache),
        out_shape=(jax.ShapeDtypeStruct((b * t, W_B), f32), new, new),
        grid=(b,),
        in_specs=[
            pl.BlockSpec((t, W_B), lambda i: (i, P_QB // W_B)),
            pl.BlockSpec((t, W_B), lambda i: (i, P_KB // W_B)),
            pl.BlockSpec((t, W_B), lambda i: (i, P_VB // W_B)),
            cache_spec, cache_spec,
            pl.BlockSpec((None, H_B, t, n_cache + t), lambda i: (l, 0, 0, 0)),
        ],
        out_specs=(pl.BlockSpec((t, W_B), lambda i: (i, 0)), new_spec, new_spec),
        compiler_params=_params(1),
        name="sample_attn",
    )(p, p, p, k_cache_t, v_cache_t, bias)


def _unit_lower_inverse(a):
    n, c, _ = a.shape
    ri, ci = _iota((n, c, c), 1), _iota((n, c, c), 2)
    d = jnp.where(ri == ci, 1.0, 0.0) - jnp.where((ri // 2 == ci // 2) & (ri > ci), a, 0.0)
    s = 2
    while s < c:
        e = jnp.where((ri // (2 * s) == ci // (2 * s)) & (ri % (2 * s) >= s) & (ci % (2 * s) < s), a, 0.0)
        d = d - _bmm(d, _bmm(e, d))
        s *= 2
    return d


def _delta_kernel(q_ref, k_ref, v_ref, z_ref, sm_ref, hist_ref, s0_ref, cw_ref, alog_ref, dtb_ref, ng_ref,
                  y_ref, sout_ref, cout_ref, ext_ref, s_ref, *, ns, tl, c, carry):
    t_idx = pl.program_id(1)
    tb = ns * tl
    nch = tb // c
    nb = H_C * nch

    def heads(x):
        return jnp.concatenate(
            [x[:, h * 64:(h + 1) * 64].reshape(nch, c, 64) for h in range(H_C)], axis=0)

    if carry:
        @pl.when(t_idx == 0)
        def _():
            s_ref[...] = s0_ref[...]
            for s in range(3):
                ext_ref[s, :, 0:8, :] = jnp.zeros((ns, 8, 256), f32)
                ext_ref[s, :, 5:8, :] = hist_ref[:, :, s * 256:(s + 1) * 256]
    else:
        for s in range(3):
            ext_ref[s, :, 5:8, :] = hist_ref[:, :, s * 256:(s + 1) * 256]

    parts = []
    for s, ref in enumerate((q_ref, k_ref, v_ref)):
        ext_ref[s, :, 8:8 + tl, :] = ref[...]
        acc = ext_ref[s, :, 5:5 + tl, :] * cw_ref[0:1, s * 256:(s + 1) * 256]
        for j in range(1, CONV_W):
            acc = acc + ext_ref[s, :, 5 + j:5 + j + tl, :] * cw_ref[j:j + 1, s * 256:(s + 1) * 256]
        parts.append(_silu(acc).reshape(tb, 256))
        last = ext_ref[s, :, tl + 5:tl + 8, :]
        if carry:
            ext_ref[s, :, 5:8, :] = last
        cout_ref[:, :, s * 256:(s + 1) * 256] = last
    cq, ck, cv = parts
    seg = _seg_ones(256, DK_C, 256, DK_C)
    qn = cq * lax.rsqrt(_dot_sel(cq * cq, seg) + EPS) * (DK_C ** -0.5)
    kn = ck * lax.rsqrt(_dot_sel(ck * ck, seg) + EPS)

    sm = sm_ref[...].reshape(tb, SMALL_W)
    g = -jnp.exp(alog_ref[...]) * jax.nn.softplus(sm + dtb_ref[...])
    beta = jax.nn.sigmoid(sm)
    ltri = (_iota((nch, c, c), 1) >= _iota((nch, c, c), 2)).astype(bf16)
    gcum = _bmm_sel(ltri, g.reshape(nch, c, SMALL_W)).reshape(tb, SMALL_W)
    lane_head = _iota((SMALL_W, 256), 1) // 64
    gb = _dot_sel(gcum, (_iota((SMALL_W, 256), 0) == lane_head).astype(bf16))
    betab = _dot_sel(beta, (_iota((SMALL_W, 256), 0) == lane_head + H_C).astype(bf16))
    eg = jnp.exp(gb)
    gl = jnp.broadcast_to(gb.reshape(nch, c, 256)[:, c - 1:c, :], (nch, c, 256)).reshape(tb, 256)

    k3, q3 = heads(kn), heads(qn)
    gcb = heads(gb)[:, :, :c]
    grb = jnp.swapaxes(gcb, 1, 2)
    ri, ci = _iota((nb, c, c), 1), _iota((nb, c, c), 2)
    incl = ri >= ci
    dec = jnp.where(incl, jnp.exp(jnp.where(incl, gcb - grb, 0.0)), 0.0)
    a = jnp.where(ri > ci, heads(betab)[:, :, :c] * _bmm_nt(k3, k3) * dec, 0.0)
    rhs = jnp.concatenate([heads(betab * cv), heads(betab * eg * kn)], axis=-1)
    sol = _bmm(_unit_lower_inverse(a), rhs)
    u, w = sol[:, :, :64], sol[:, :, 64:]
    qk = _bmm_nt(q3, k3) * dec
    qt3 = heads(qn * eg)
    kt3 = heads(kn * jnp.exp(gl - gb))
    egl = heads(jnp.exp(gl))[:, 0:1, :]

    if carry:
        o_heads = []
        for h in range(H_C):
            o_chunks = []
            for s in range(ns):
                st = s_ref[s, h]
                for ic in range(s * (tl // c), (s + 1) * (tl // c)):
                    n = h * nch + ic
                    un = u[n] - _dot(w[n], st)
                    o_chunks.append(_dot(qt3[n], st) + _dot(qk[n], un))
                    st = egl[n] * st + _dot_tn(kt3[n], un)
                s_ref[s, h] = st
            o_heads.append(jnp.concatenate(o_chunks, axis=0))
        o = jnp.concatenate(o_heads, axis=1)

        @pl.when(t_idx == pl.num_programs(1) - 1)
        def _():
            sout_ref[...] = s_ref[...]
    else:
        st = jnp.concatenate([s0_ref[:, h] for h in range(H_C)], axis=0)
        un = u - _bmm(w, st)
        o3 = _bmm(qt3, st) + _bmm(qk, un)
        st = egl * st + _bmm_tn(kt3, un)
        for h in range(H_C):
            sout_ref[:, h] = st[h * nch:(h + 1) * nch]
        o = jnp.concatenate([o3[h * nch:(h + 1) * nch].reshape(tb, 64) for h in range(H_C)], axis=1)

    o = o * lax.rsqrt(_dot_sel(o * o, seg) * (1.0 / DV_C) + EPS) * ng_ref[...]
    y_ref[...] = (o * _silu(z_ref[...].reshape(tb, W_C))).reshape(ns, tl, W_C)


def _delta_call(p, b, t, hist, s0, lh, conv_w, alog, dtb, ng, l):
    c = min(CHUNK, t)
    carry = t > c
    ns, tl = (DELTA_SEQS, DELTA_ROWS // DELTA_SEQS) if carry else (DELTA_ROWS // (4 * t), t)
    nt = t // tl
    p3 = p.reshape(b, t, P_COLS)

    def col(cb):
        return lambda i, j: (i, j, cb)

    layer = lambda i, j: (l, 0, 0)
    state_spec = pl.BlockSpec((ns, H_C, DK_C, DV_C), lambda i, j: (i, 0, 0, 0))
    y, s_new, conv_new = pl.pallas_call(
        functools.partial(_delta_kernel, ns=ns, tl=tl, c=c, carry=carry),
        out_shape=(jax.ShapeDtypeStruct((b, t, W_C), f32),
                   jax.ShapeDtypeStruct((b, H_C, DK_C, DV_C), f32),
                   jax.ShapeDtypeStruct((b, CONV_W - 1, QKV_C), f32)),
        grid=(b // ns, nt),
        in_specs=[
            pl.BlockSpec((ns, tl, 256), col(P_QKVC // 256)),
            pl.BlockSpec((ns, tl, 256), col(P_QKVC // 256 + 1)),
            pl.BlockSpec((ns, tl, 256), col(P_QKVC // 256 + 2)),
            pl.BlockSpec((ns, tl, 256), col(P_ZC // 256)),
            pl.BlockSpec((ns, tl, SMALL_W), col(P_SMALL // SMALL_W)),
            pl.BlockSpec((None, ns, CONV_W - 1, QKV_C), lambda i, j: (lh, i, 0, 0)),
            pl.BlockSpec((None, ns, H_C, DK_C, DV_C), lambda i, j: (lh, i, 0, 0, 0)),
            pl.BlockSpec((None, CONV_W, QKV_C), layer),
            pl.BlockSpec((None, 1, SMALL_W), layer),
            pl.BlockSpec((None, 1, SMALL_W), layer),
            pl.BlockSpec((None, 1, W_C), layer),
        ],
        out_specs=(pl.BlockSpec((ns, tl, W_C), lambda i, j: (i, j, 0)),
                   state_spec,
                   pl.BlockSpec((ns, CONV_W - 1, QKV_C), lambda i, j: (i, 0, 0))),
        scratch_shapes=[
            pltpu.VMEM((3, ns, tl + 8, 256), f32),
            pltpu.VMEM((ns, H_C, DK_C, DV_C), f32),
        ],
        compiler_params=_params(2),
        name=f"delta_t{t}",
    )(p3, p3, p3, p3, p3, hist, s0, conv_w, alog, dtb, ng)
    return y.reshape(b * t, W_C), s_new, conv_new


def _gla_kernel(q_ref, k_ref, v_ref, gate_ref, sm_ref, s0_ref, wgk_ref, bgk_ref, ng_ref,
                y_ref, sout_ref, qs_ref, g_ref, o_ref, s_ref, *, tb, c):
    t_idx = pl.program_id(1)
    blk = _iota((WK_D, W_D), 0) // DK_D == _iota((WK_D, W_D), 1) // DV_D
    seg_kv = blk.astype(bf16)

    @pl.when(t_idx == 0)
    def _():
        s_ref[...] = jnp.zeros((WK_D, W_D), f32)
        for h in range(H_D):
            s_ref[h * DK_D:(h + 1) * DK_D, h * DV_D:(h + 1) * DV_D] = s0_ref[h]

    gk = jax.nn.log_sigmoid(_dot(sm_ref[...], wgk_ref[...]) + bgk_ref[...]) * (1.0 / GLA_GATE_NORM)
    g_ref[...] = gk
    qs_ref[...] = q_ref[...] * (DK_D ** -0.5)
    ltri = (_iota((c, c), 0) >= _iota((c, c), 1)).astype(bf16)
    row8 = _iota((8, WK_D), 0)

    def chunk(ic, carry):
        rows = pl.ds(pl.multiple_of(ic * c, c), c)
        g_hi, g_lo = _split(g_ref[rows, :])
        g = _dot(ltri, g_hi) + _dot(ltri, g_lo)
        q, k, v = qs_ref[rows, :], k_ref[rows, :], v_ref[rows, :]
        st = s_ref[...]
        o = _dot(q * jnp.exp(g), st)
        for jb in range(c // 8):
            lo = 8 * jb
            m = c - lo
            e = []
            for j in range(lo, lo + 8):
                ej = jnp.exp(g[lo:] - g[j:j + 1]) * (q[lo:] * k[j:j + 1])
                first = jnp.where(row8 >= j - lo, ej[:8], 0.0)
                e.append(first if m == 8 else jnp.concatenate([first, ej[8:]], axis=0))
            att = _dot(jnp.concatenate(e, axis=0).astype(bf16), seg_kv)
            upd = att[0:m] * v[lo:lo + 1]
            for jj in range(1, 8):
                upd = upd + att[jj * m:(jj + 1) * m] * v[lo + jj:lo + jj + 1]
            o = o + upd if lo == 0 else jnp.concatenate([o[:lo], o[lo:] + upd], axis=0)
        o_ref[rows, :] = o
        gl = g[c - 1:c]
        egl_col = jnp.exp(g.T[:, c - 1:c])
        s_new = egl_col * st + _dot_tn(k * jnp.exp(gl - g), v)
        s_ref[...] = jnp.where(blk, s_new, 0.0)
        return carry

    lax.fori_loop(0, tb // c, chunk, 0)

    o = o_ref[...]
    o = o * lax.rsqrt(_dot_sel(o * o, _seg_ones(W_D, DV_D, W_D, DV_D)) * (1.0 / DV_D) + EPS) * ng_ref[...]
    y_ref[...] = o * _silu(gate_ref[...])

    @pl.when(t_idx == pl.num_programs(1) - 1)
    def _():
        for h in range(H_D):
            sout_ref[h] = s_ref[h * DK_D:(h + 1) * DK_D, h * DV_D:(h + 1) * DV_D]


def _gla_call(p, b, t, s0, lh, wgk, bgk, ng, l):
    tb = min(t, ROW_TILE)
    c = min(CHUNK, t)
    nt = t // tb

    def col(cb):
        return lambda i, j: (i * nt + j, cb)

    layer = lambda i, j: (l, 0, 0)
    return pl.pallas_call(
        functools.partial(_gla_kernel, tb=tb, c=c),
        out_shape=(jax.ShapeDtypeStruct((b * t, W_D), f32),
                   jax.ShapeDtypeStruct((b, H_D, DK_D, DV_D), f32)),
        grid=(b, nt),
        in_specs=[
            pl.BlockSpec((tb, WK_D), col(P_QD // WK_D)),
            pl.BlockSpec((tb, WK_D), col(P_KD // WK_D)),
            pl.BlockSpec((tb, W_D), col(P_VD // W_D)),
            pl.BlockSpec((tb, W_D), col(P_GD // W_D)),
            pl.BlockSpec((tb, SMALL_W), col(P_SMALL // SMALL_W)),
            pl.BlockSpec((None, None, H_D, DK_D, DV_D), lambda i, j: (lh, i, 0, 0, 0)),
            pl.BlockSpec((None, SMALL_W, WK_D), layer),
            pl.BlockSpec((None, 1, WK_D), layer),
            pl.BlockSpec((None, 1, W_D), layer),
        ],
        out_specs=(pl.BlockSpec((tb, W_D), lambda i, j: (i * nt + j, 0)),
                   pl.BlockSpec((None, H_D, DK_D, DV_D), lambda i, j: (i, 0, 0, 0))),
        scratch_shapes=[
            pltpu.VMEM((tb, WK_D), f32), pltpu.VMEM((tb, WK_D), f32),
            pltpu.VMEM((tb, W_D), f32),
            pltpu.VMEM((WK_D, W_D), f32),
        ],
        compiler_params=_params(2),
        name=f"gla_t{t}",
    )(p, p, p, p, p, s0, wgk, bgk, ng)


def _reorder_w_in(w_in):
    pad = jnp.zeros(w_in.shape[:-1] + (P_COLS - P_SMALL - 2 * H_C - GLA_RANK,), w_in.dtype)
    return jnp.concatenate([
        w_in[..., 0:2048],
        w_in[..., 2312:2568],
        w_in[..., 2568:2824],
        w_in[..., 2056:2184],
        w_in[..., 2184:2312],
        w_in[..., 2048:2056],
        w_in[..., 2824:2840],
        pad], axis=-1)


def _lane_pad(x, width):
    return jnp.pad(x, ((0, 0), (0, width - x.shape[-1])))[:, None, :]


def kernel(x_prompt, x_sample, cache_pool, cache_attn_k, cache_attn_v, state_conv, state_delta,
           state_gla, attn_norm_g, w_in, pool_w, pool_scale, rel_bias, conv_w, a_log, dt_bias,
           delta_norm_g, gla_w_gk, gla_b_gk, gla_norm_g, w_out, mlp_norm_g, w_up, w_down,
           final_norm_g):
    bp, tp, _ = x_prompt.shape
    bs, ts, _ = x_sample.shape
    w_in_r = _reorder_w_in(w_in.astype(bf16))
    w_out_b, w_up_b, w_down_b = w_out.astype(bf16), w_up.astype(bf16), w_down.astype(bf16)
    g1 = attn_norm_g[:, None, :]
    g2 = mlp_norm_g[:, None, :]
    gf = final_norm_g.reshape(1, D_MODEL)
    pool_wbd = jnp.zeros((DEPTH, W_A, W_A), f32)
    for gi in range(N_POOL):
        sl = slice(gi * POOL_GW, (gi + 1) * POOL_GW)
        pool_wbd = pool_wbd.at[:, sl, sl].set(pool_w[:, gi])
    pool_sc = pool_scale[:, None, :]
    band_bias = _band_bias_row(rel_bias)
    n_cache = cache_attn_k.shape[3]
    sample_bias = _rel_bias_rows(rel_bias, ts, n_cache + ts, n_cache)
    cache_kt, cache_vt = jnp.swapaxes(cache_attn_k, 3, 4), jnp.swapaxes(cache_attn_v, 3, 4)
    alog = _lane_pad(a_log, SMALL_W)
    dtb = _lane_pad(dt_bias, SMALL_W)
    ng_c = jnp.tile(delta_norm_g, (1, H_C))[:, None, :]
    wgk = jnp.zeros((DEPTH, SMALL_W, WK_D), f32).at[:, 2 * H_C:2 * H_C + GLA_RANK].set(gla_w_gk)
    bgk = gla_b_gk[:, None, :]
    ng_d = jnp.tile(gla_norm_g, (1, H_D))[:, None, :]
    zeros = (jnp.zeros((1, bp, POOL_HIST, W_A), f32), jnp.zeros((1, bp, CONV_W - 1, QKV_C), f32),
             jnp.zeros((1, bp, H_C, DK_C, DV_C), f32), jnp.zeros((1, bp, H_D, DK_D, DV_D), f32))
    carried = (cache_pool, state_conv, state_delta, state_gla)

    xs = [x_prompt.reshape(bp * tp, D_MODEL), x_sample.reshape(bs * ts, D_MODEL)]
    states = ([], [])
    for l in range(DEPTH):
        for s, (b, t, pos0) in enumerate(((bp, tp, 0), (bs, ts, tp))):
            pool_h, conv_h, sd_h, sg_h = zeros if s == 0 else carried
            lh = 0 if s == 0 else l
            p = _inproj(xs[s], g1, w_in_r, l)
            y_a, st_pool = _pool_call(p, b, t, pos0, pool_h, lh, pool_wbd, pool_sc, l)
            if s == 0:
                y_b, st_k, st_v = _band_attn_call(p, b, t, band_bias, l)
            else:
                y_b, st_k, st_v = _sample_attn_call(p, b, t, cache_kt, cache_vt, sample_bias, l)
            y_c, st_d, st_conv = _delta_call(p, b, t, conv_h, sd_h, lh, conv_w, alog, dtb, ng_c, l)
            y_d, st_g = _gla_call(p, b, t, sg_h, lh, wgk, bgk, ng_d, l)
            xs[s] = _outmlp(xs[s], (y_a, y_b, y_c, y_d), w_out_b, g2, w_up_b, w_down_b, gf, l,
                            final=(l == DEPTH - 1))
            states[s].append((st_pool, st_k, st_v, st_conv, st_d, st_g))
    y_prompt = xs[0].reshape(bp, tp, D_MODEL)
    y_sample = xs[1].reshape(bs, ts, D_MODEL)
    outs_p = tuple(jnp.stack([st[i] for st in states[0]]) for i in range(6))
    outs_s = tuple(jnp.stack([st[i] for st in states[1]]) for i in range(6))
    return (y_prompt, y_sample) + outs_p + outs_s
```

```python
import functools

import jax
import jax.numpy as jnp
import numpy as np
from jax import lax
from jax.experimental import pallas as pl
from jax.experimental.pallas import tpu as pltpu

D_MODEL = 1024
DEPTH = 4
CHUNK = 64
W_A = W_B = W_C = W_D = 256
POOL_WINDOWS = (2, 4, 8, 16)
N_POOL = 4
POOL_GW = 64
POOL_HIST = 15
H_B = 4
DH_B = 64
BAND_CHUNKS = 8
ATTN_WINDOW = 512
REL_CLIP = 256
H_C = 4
DK_C = 64
DV_C = 64
CONV_W = 4
QKV_C = 768
H_D = 4
DK_D = 32
DV_D = 64
WK_D = 128
GLA_RANK = 16
GLA_GATE_NORM = 16.0
D_FF = 4096
EPS = 1e-6
NEG_INF = -1e30

P_UA = 0
P_QB, P_KB, P_VB = 256, 512, 768
P_QKVC = 1024
P_ZC = 1792
P_VD = 2048
P_GD = 2304
P_QD = 2560
P_KD = 2688
P_SMALL = 2816
SMALL_W = 128
P_COLS = 2944

ROW_TILE = 512
FF_CHUNK = 1024
POOL_TILE = 2048
GLA_SUB = 16
GLA_UNROLL = 4
DELTA_ROWS = 1024
DELTA_SEQS = 4
Q_BLOCK = 256
K_WINDOW = Q_BLOCK + BAND_CHUNKS * CHUNK
VMEM_LIMIT = 56 * 1024 * 1024

f32 = jnp.float32
bf16 = jnp.bfloat16


def _params(n_axes):
    return pltpu.CompilerParams(dimension_semantics=("arbitrary",) * n_axes, vmem_limit_bytes=VMEM_LIMIT)


def _dot(a, b):
    return jnp.dot(a, b, preferred_element_type=f32)


def _dot_nt(a, b):
    return lax.dot_general(a, b, (((1,), (1,)), ((), ())), preferred_element_type=f32)


def _dot_tn(a, b):
    return lax.dot_general(a, b, (((0,), (0,)), ((), ())), preferred_element_type=f32)


def _bmm(a, b):
    return jnp.einsum('nij,njk->nik', a, b, preferred_element_type=f32)


def _bmm_nt(a, b):
    return jnp.einsum('nid,njd->nij', a, b, preferred_element_type=f32)


def _bmm_tn(a, b):
    return jnp.einsum('nji,njk->nik', a, b, preferred_element_type=f32)


def _split(x):
    hi = x.astype(bf16)
    return hi, (x - hi.astype(f32)).astype(bf16)


def _dot_sel(x, sel):
    hi, lo = _split(x)
    return _dot(hi, sel) + _dot(lo, sel)


def _bmm_sel(sel, x):
    hi, lo = _split(x)
    return _bmm(sel, hi) + _bmm(sel, lo)


def _iota(shape, dim):
    return lax.broadcasted_iota(jnp.int32, shape, dim)


def _seg_ones(n_in, seg_in, n_out, seg_out):
    return (_iota((n_in, n_out), 0) // seg_in == _iota((n_in, n_out), 1) // seg_out).astype(bf16)


def _silu(x):
    return x * jax.nn.sigmoid(x)


def _rms(x):
    return x * lax.rsqrt(jnp.mean(x * x, axis=-1, keepdims=True) + EPS)


def _inproj_kernel(x_ref, g_ref, w_ref, o_ref):
    h = _rms(x_ref[...]) * g_ref[...]
    o_ref[...] = _dot(h.astype(bf16), w_ref[...])


def _inproj(x, g, w, l):
    n = x.shape[0]
    return pl.pallas_call(
        _inproj_kernel,
        out_shape=jax.ShapeDtypeStruct((n, P_COLS), f32),
        grid=(n // ROW_TILE,),
        in_specs=[
            pl.BlockSpec((ROW_TILE, D_MODEL), lambda i: (i, 0)),
            pl.BlockSpec((None, 1, D_MODEL), lambda i: (l, 0, 0)),
            pl.BlockSpec((None, D_MODEL, P_COLS), lambda i: (l, 0, 0), pipeline_mode=pl.Buffered(1)),
        ],
        out_specs=pl.BlockSpec((ROW_TILE, P_COLS), lambda i: (i, 0)),
        compiler_params=_params(1),
        name="inproj",
    )(x, g, w)


def _outmlp_kernel(x_ref, ya_ref, yb_ref, yc_ref, yd_ref, wo_ref, g2_ref, wu_ref, wd_ref, gf_ref,
                   o_ref, *, final):
    acc = None
    for k, y_ref in enumerate((ya_ref, yb_ref, yc_ref, yd_ref)):
        d = _dot(y_ref[...].astype(bf16), wo_ref[k * 256:(k + 1) * 256, :])
        acc = d if acc is None else acc + d
    x = x_ref[...] + acc
    h2 = (_rms(x) * g2_ref[...]).astype(bf16)
    acc = None
    for c in range(D_FF // FF_CHUNK):
        up = _dot(h2, wu_ref[:, c * FF_CHUNK:(c + 1) * FF_CHUNK])
        up = jnp.square(jnp.maximum(up, 0.0)).astype(bf16)
        d = _dot(up, wd_ref[c * FF_CHUNK:(c + 1) * FF_CHUNK, :])
        acc = d if acc is None else acc + d
    x = x + acc
    if final:
        x = _rms(x) * gf_ref[...]
    o_ref[...] = x


def _outmlp(x, ys, wo, g2, wu, wd, gf, l, final):
    n = x.shape[0]
    row = lambda i: (i, 0)
    layer = lambda i: (l, 0, 0)
    single = pl.Buffered(1)
    return pl.pallas_call(
        functools.partial(_outmlp_kernel, final=final),
        out_shape=jax.ShapeDtypeStruct((n, D_MODEL), f32),
        grid=(n // ROW_TILE,),
        in_specs=[pl.BlockSpec((ROW_TILE, D_MODEL), row)]
        + [pl.BlockSpec((ROW_TILE, 256), row)] * 4
        + [pl.BlockSpec((None, D_MODEL, D_MODEL), layer, pipeline_mode=single),
           pl.BlockSpec((None, 1, D_MODEL), layer),
           pl.BlockSpec((None, D_MODEL, D_FF), layer, pipeline_mode=single),
           pl.BlockSpec((None, D_FF, D_MODEL), layer, pipeline_mode=single),
           pl.BlockSpec((1, D_MODEL), lambda i: (0, 0))],
        out_specs=pl.BlockSpec((ROW_TILE, D_MODEL), row),
        compiler_params=_params(1),
        name="outmlp_final" if final else "outmlp",
    )(x, *ys, wo, g2, wu, wd, gf)


def _pool_kernel(u_ref, hist_ref, w_ref, scale_ref, y_ref, st_ref, ext_ref, *, tb, pos0):
    t_idx = pl.program_id(1)

    @pl.when(t_idx == 0)
    def _():
        ext_ref[0:8, :] = jnp.zeros((8, W_A), f32)
        ext_ref[1:16, :] = hist_ref[...]

    x = u_ref[...]
    ext_ref[16:16 + tb, :] = x
    wsum = {}
    acc = x
    for k in range(1, 16):
        acc = acc + ext_ref[16 - k:16 - k + tb, :]
        if k + 1 in POOL_WINDOWS:
            wsum[k + 1] = acc
    ext_ref[1:16, :] = ext_ref[tb + 1:tb + 16, :]
    lane_group = _iota((tb, W_A), 1) // POOL_GW
    pos = pos0 + t_idx * tb + _iota((tb, W_A), 0)
    ws, win = wsum[POOL_WINDOWS[-1]], jnp.full((tb, W_A), POOL_WINDOWS[-1], jnp.int32)
    for gi in range(N_POOL - 2, -1, -1):
        ws = jnp.where(lane_group == gi, wsum[POOL_WINDOWS[gi]], ws)
        win = jnp.where(lane_group == gi, POOL_WINDOWS[gi], win)
    cnt = jnp.minimum(win, pos + 1).astype(f32)
    pooled = ws / cnt - x
    y_ref[...] = _dot(pooled, w_ref[...]) * scale_ref[...]

    @pl.when(t_idx == pl.num_programs(1) - 1)
    def _():
        st_ref[...] = ext_ref[1:16, :]


def _pool_call(p, b, t, pos0, hist, lh, wbd, scale, l):
    tb = min(t, POOL_TILE)
    nt = t // tb
    return pl.pallas_call(
        functools.partial(_pool_kernel, tb=tb, pos0=pos0),
        out_shape=(jax.ShapeDtypeStruct((b * t, W_A), f32),
                   jax.ShapeDtypeStruct((b, POOL_HIST, W_A), f32)),
        grid=(b, nt),
        in_specs=[
            pl.BlockSpec((tb, W_A), lambda i, j: (i * nt + j, P_UA // W_A)),
            pl.BlockSpec((None, None, POOL_HIST, W_A), lambda i, j: (lh, i, 0, 0)),
            pl.BlockSpec((None, W_A, W_A), lambda i, j: (l, 0, 0)),
            pl.BlockSpec((None, 1, W_A), lambda i, j: (l, 0, 0)),
        ],
        out_specs=(pl.BlockSpec((tb, W_A), lambda i, j: (i * nt + j, 0)),
                   pl.BlockSpec((None, POOL_HIST, W_A), lambda i, j: (i, 0, 0))),
        scratch_shapes=[pltpu.VMEM((tb + 16, W_A), f32)],
        compiler_params=_params(2),
        name=f"pool_t{t}",
    )(p, hist, wbd, scale)


def _softmax_av(s_parts, av_fns):
    m = functools.reduce(jnp.maximum, [jnp.max(s, axis=-1, keepdims=True) for s in s_parts])
    e_parts = [jnp.exp(s - m) for s in s_parts]
    l = functools.reduce(jnp.add, [jnp.sum(e, axis=-1, keepdims=True) for e in e_parts])
    o = functools.reduce(jnp.add, [av(e) for e, av in zip(e_parts, av_fns)])
    return o / l


BIAS_PERIOD = 2 * K_WINDOW


def _band_attn_kernel(q_ref, k_ref, v_ref, wrow_ref, y_ref, kc_ref, vc_ref, bias_ref, *, t, keep):
    qb = pl.program_id(1)

    @pl.when((pl.program_id(0) == 0) & (qb == 0))
    def _():
        qi = _iota((Q_BLOCK, K_WINDOW), 0)
        kj = _iota((Q_BLOCK, K_WINDOW), 1)
        for h in range(H_B):
            base = pltpu.roll(jnp.broadcast_to(wrow_ref[h], (Q_BLOCK, BIAS_PERIOD)), 0, 1,
                              stride=1, stride_axis=0)
            for v in range(3):
                dchunk = (qi + v * Q_BLOCK) // CHUNK - kj // CHUNK
                vis = (dchunk >= 0) & (dchunk <= BAND_CHUNKS)
                lo = (2 - v) * Q_BLOCK
                bias_ref[v, h] = jnp.where(vis, base[:, lo:lo + K_WINDOW], NEG_INF)

    start = pl.multiple_of(jnp.maximum(qb * Q_BLOCK - BAND_CHUNKS * CHUNK, 0), Q_BLOCK)
    variant = jnp.minimum(qb, 2)
    outs = []
    for h in range(H_B):
        lanes = slice(h * DH_B, (h + 1) * DH_B)
        q = q_ref[:, lanes] * (DH_B ** -0.5)
        k = k_ref[pl.ds(start, K_WINDOW), lanes]
        v = v_ref[pl.ds(start, K_WINDOW), lanes]
        s = _dot_nt(q, k) + bias_ref[variant, h]
        outs.append(_softmax_av([s], [lambda e, v=v: _dot(e, v)]))
    y_ref[...] = jnp.concatenate(outs, axis=1)

    @pl.when(qb == pl.num_programs(1) - 1)
    def _():
        for h in range(H_B):
            lanes = slice(h * DH_B, (h + 1) * DH_B)
            kc_ref[h] = k_ref[t - keep:t, lanes]
            vc_ref[h] = v_ref[t - keep:t, lanes]


def _rel_bias_rows(table, nq, nk, off):
    period = nq + nk
    m = np.arange(period)
    rel = np.where(m < nk, m, m - period) - off
    row = table.astype(f32)[..., np.clip(rel, -REL_CLIP, REL_CLIP) + REL_CLIP]
    lead = table.shape[:-1]
    flat = jnp.tile(row, (1,) * len(lead) + (nq,))[..., :nq * (period - 1)]
    return flat.reshape(lead + (nq, period - 1))[..., :nk]


def _band_bias_row(table):
    m = np.arange(BIAS_PERIOD)
    e = np.where(m < 2 * Q_BLOCK + K_WINDOW, m, m - BIAS_PERIOD)
    idx = np.clip(e - 2 * Q_BLOCK, -REL_CLIP, REL_CLIP) + REL_CLIP
    return table.astype(f32)[..., idx][:, :, None, :]


def _band_attn_call(p, b, t, wrow, l):
    nq = t // Q_BLOCK
    keep = min(ATTN_WINDOW, t)
    cache = jax.ShapeDtypeStruct((b, H_B, keep, DH_B), f32)
    cache_spec = pl.BlockSpec((None, H_B, keep, DH_B), lambda i, j: (i, 0, 0, 0))
    return pl.pallas_call(
        functools.partial(_band_attn_kernel, t=t, keep=keep),
        out_shape=(jax.ShapeDtypeStruct((b * t, W_B), f32), cache, cache),
        grid=(b, nq),
        in_specs=[
            pl.BlockSpec((Q_BLOCK, W_B), lambda i, j: (i * nq + j, P_QB // W_B)),
            pl.BlockSpec((t, W_B), lambda i, j: (i, P_KB // W_B)),
            pl.BlockSpec((t, W_B), lambda i, j: (i, P_VB // W_B)),
            pl.BlockSpec((None, H_B, 1, BIAS_PERIOD), lambda i, j: (l, 0, 0, 0)),
        ],
        out_specs=(pl.BlockSpec((Q_BLOCK, W_B), lambda i, j: (i * nq + j, 0)), cache_spec, cache_spec),
        scratch_shapes=[pltpu.VMEM((3, H_B, Q_BLOCK, K_WINDOW), f32)],
        compiler_params=_params(2),
        name="band_attn",
    )(p, p, p, wrow)


def _sample_attn_kernel(q_ref, k_ref, v_ref, kct_ref, vct_ref, bias_ref, y_ref, kn_ref, vn_ref, *, n_cache):
    outs = []
    for h in range(H_B):
        lanes = slice(h * DH_B, (h + 1) * DH_B)
        q = q_ref[:, lanes] * (DH_B ** -0.5)
        k_new, v_new = k_ref[:, lanes], v_ref[:, lanes]
        s_old = _dot(q, kct_ref[h]) + bias_ref[h, :, 0:n_cache]
        s_new = _dot_nt(q, k_new) + bias_ref[h, :, n_cache:]
        outs.append(_softmax_av([s_old, s_new],
                                [lambda e, h=h: _dot_nt(e, vct_ref[h]), lambda e, v=v_new: _dot(e, v)]))
        kn_ref[h] = k_new
        vn_ref[h] = v_new
    y_ref[...] = jnp.concatenate(outs, axis=1)


def _sample_attn_call(p, b, t, k_cache_t, v_cache_t, bias, l):
    n_cache = k_cache_t.shape[4]
    cache_spec = pl.BlockSpec((None, None, H_B, DH_B, n_cache), lambda i: (l, i, 0, 0, 0))
    new = jax.ShapeDtypeStruct((b, H_B, t, DH_B), f32)
    new_spec = pl.BlockSpec((None, H_B, t, DH_B), lambda i: (i, 0, 0, 0))
    return pl.pallas_call(
        functools.partial(_sample_attn_kernel, n_cache=n_cache),
        out_shape=(jax.ShapeDtypeStruct((b * t, W_B), f32), new, new),
        grid=(b,),
        in_specs=[
            pl.BlockSpec((t, W_B), lambda i: (i, P_QB // W_B)),
            pl.BlockSpec((t, W_B), lambda i: (i, P_KB // W_B)),
            pl.BlockSpec((t, W_B), lambda i: (i, P_VB // W_B)),
            cache_spec, cache_spec,
            pl.BlockSpec((None, H_B, t, n_cache + t), lambda i: (l, 0, 0, 0)),
        ],
        out_specs=(pl.BlockSpec((t, W_B), lambda i: (i, 0)), new_spec, new_spec),
        compiler_params=_params(1),
        name="sample_attn",
    )(p, p, p, k_cache_t, v_cache_t, bias)


def _unit_lower_inverse(a):
    n, c, _ = a.shape
    ri, ci = _iota((c, c), 0), _iota((c, c), 1)
    d = jnp.where(ri == ci, 1.0, 0.0)[None] - jnp.where(((ri // 2 == ci // 2) & (ri > ci))[None], a, 0.0)
    s = 2
    while s < c:
        lower_left = (ri // (2 * s) == ci // (2 * s)) & (ri % (2 * s) >= s) & (ci % (2 * s) < s)
        e = jnp.where(lower_left[None], a, 0.0)
        d = d - _bmm(d, _bmm(e, d))
        s *= 2
    return d


def _delta_kernel(q_ref, k_ref, v_ref, z_ref, sm_ref, hist_ref, s0_ref, cw_ref, alog_ref, dtb_ref, ng_ref,
                  y_ref, sout_ref, cout_ref, ext_ref, s_ref, *, ns, tl, c, carry):
    t_idx = pl.program_id(1)
    tb = ns * tl
    nch = tb // c
    nb = H_C * nch

    def heads(x):
        return jnp.concatenate(
            [x[:, h * 64:(h + 1) * 64].reshape(nch, c, 64) for h in range(H_C)], axis=0)

    if carry:
        @pl.when(t_idx == 0)
        def _():
            s_ref[...] = s0_ref[...]
            for s in range(3):
                ext_ref[s, :, 0:8, :] = jnp.zeros((ns, 8, 256), f32)
                ext_ref[s, :, 5:8, :] = hist_ref[:, :, s * 256:(s + 1) * 256]
    else:
        for s in range(3):
            ext_ref[s, :, 5:8, :] = hist_ref[:, :, s * 256:(s + 1) * 256]

    parts = []
    for s, ref in enumerate((q_ref, k_ref, v_ref)):
        ext_ref[s, :, 8:8 + tl, :] = ref[...]
        acc = ext_ref[s, :, 5:5 + tl, :] * cw_ref[0:1, s * 256:(s + 1) * 256]
        for j in range(1, CONV_W):
            acc = acc + ext_ref[s, :, 5 + j:5 + j + tl, :] * cw_ref[j:j + 1, s * 256:(s + 1) * 256]
        parts.append(_silu(acc).reshape(tb, 256))
        last = ext_ref[s, :, tl + 5:tl + 8, :]
        if carry:
            ext_ref[s, :, 5:8, :] = last
        cout_ref[:, :, s * 256:(s + 1) * 256] = last
    cq, ck, cv = parts
    seg = _seg_ones(256, DK_C, 256, DK_C)
    qn = cq * lax.rsqrt(_dot_sel(cq * cq, seg) + EPS) * (DK_C ** -0.5)
    kn = ck * lax.rsqrt(_dot_sel(ck * ck, seg) + EPS)

    sm = sm_ref[...].reshape(tb, SMALL_W)
    g = -jnp.exp(alog_ref[...]) * jax.nn.softplus(sm + dtb_ref[...])
    beta = jax.nn.sigmoid(sm)
    ri, ci = _iota((c, c), 0), _iota((c, c), 1)
    incl = (ri >= ci)[None]
    ltri = jnp.broadcast_to(incl.astype(bf16), (nch, c, c))
    gcum = _bmm_sel(ltri, g.reshape(nch, c, SMALL_W)).reshape(tb, SMALL_W)
    lane_head = _iota((SMALL_W, 256), 1) // 64
    gb = _dot_sel(gcum, (_iota((SMALL_W, 256), 0) == lane_head).astype(bf16))
    betab = _dot_sel(beta, (_iota((SMALL_W, 256), 0) == lane_head + H_C).astype(bf16))
    eg = jnp.exp(gb)
    gl = jnp.broadcast_to(gb.reshape(nch, c, 256)[:, c - 1:c, :], (nch, c, 256)).reshape(tb, 256)

    k3, q3 = heads(kn), heads(qn)
    gcb = heads(gb)[:, :, :c]
    grb = jnp.swapaxes(gcb, 1, 2)
    dec = jnp.where(incl, jnp.exp(jnp.where(incl, gcb - grb, 0.0)), 0.0)
    a = jnp.where((ri > ci)[None], heads(betab)[:, :, :c] * _bmm_nt(k3, k3) * dec, 0.0)
    rhs = jnp.concatenate([heads(betab * cv), heads(betab * eg * kn)], axis=-1)
    sol = _bmm(_unit_lower_inverse(a), rhs)
    u, w = sol[:, :, :64], sol[:, :, 64:]
    qk = _bmm_nt(q3, k3) * dec
    qt3 = heads(qn * eg)
    kt3 = heads(kn * jnp.exp(gl - gb))
    egl = heads(jnp.exp(gl))[:, 0:1, :]

    if carry:
        o_heads = []
        for h in range(H_C):
            o_chunks = []
            for s in range(ns):
                st = s_ref[s, h]
                for ic in range(s * (tl // c), (s + 1) * (tl // c)):
                    n = h * nch + ic
                    un = u[n] - _dot(w[n], st)
                    o_chunks.append(_dot(qt3[n], st) + _dot(qk[n], un))
                    st = egl[n] * st + _dot_tn(kt3[n], un)
                s_ref[s, h] = st
            o_heads.append(jnp.concatenate(o_chunks, axis=0))
        o = jnp.concatenate(o_heads, axis=1)

        @pl.when(t_idx == pl.num_programs(1) - 1)
        def _():
            sout_ref[...] = s_ref[...]
    else:
        st = jnp.concatenate([s0_ref[:, h] for h in range(H_C)], axis=0)
        un = u - _bmm(w, st)
        o3 = _bmm(qt3, st) + _bmm(qk, un)
        st = egl * st + _bmm_tn(kt3, un)
        for h in range(H_C):
            sout_ref[:, h] = st[h * nch:(h + 1) * nch]
        o = jnp.concatenate([o3[h * nch:(h + 1) * nch].reshape(tb, 64) for h in range(H_C)], axis=1)

    o = o * lax.rsqrt(_dot_sel(o * o, seg) * (1.0 / DV_C) + EPS) * ng_ref[...]
    y_ref[...] = (o * _silu(z_ref[...].reshape(tb, W_C))).reshape(ns, tl, W_C)


def _delta_call(p, b, t, hist, s0, lh, conv_w, alog, dtb, ng, l):
    c = min(CHUNK, t)
    carry = t > c
    ns, tl = (DELTA_SEQS, DELTA_ROWS // DELTA_SEQS) if carry else (DELTA_ROWS // (4 * t), t)
    nt = t // tl
    p3 = p.reshape(b, t, P_COLS)

    def col(cb):
        return lambda i, j: (i, j, cb)

    layer = lambda i, j: (l, 0, 0)
    state_spec = pl.BlockSpec((ns, H_C, DK_C, DV_C), lambda i, j: (i, 0, 0, 0))
    y, s_new, conv_new = pl.pallas_call(
        functools.partial(_delta_kernel, ns=ns, tl=tl, c=c, carry=carry),
        out_shape=(jax.ShapeDtypeStruct((b, t, W_C), f32),
                   jax.ShapeDtypeStruct((b, H_C, DK_C, DV_C), f32),
                   jax.ShapeDtypeStruct((b, CONV_W - 1, QKV_C), f32)),
        grid=(b // ns, nt),
        in_specs=[
            pl.BlockSpec((ns, tl, 256), col(P_QKVC // 256)),
            pl.BlockSpec((ns, tl, 256), col(P_QKVC // 256 + 1)),
            pl.BlockSpec((ns, tl, 256), col(P_QKVC // 256 + 2)),
            pl.BlockSpec((ns, tl, 256), col(P_ZC // 256)),
            pl.BlockSpec((ns, tl, SMALL_W), col(P_SMALL // SMALL_W)),
            pl.BlockSpec((None, ns, CONV_W - 1, QKV_C), lambda i, j: (lh, i, 0, 0)),
            pl.BlockSpec((None, ns, H_C, DK_C, DV_C), lambda i, j: (lh, i, 0, 0, 0)),
            pl.BlockSpec((None, CONV_W, QKV_C), layer),
            pl.BlockSpec((None, 1, SMALL_W), layer),
            pl.BlockSpec((None, 1, SMALL_W), layer),
            pl.BlockSpec((None, 1, W_C), layer),
        ],
        out_specs=(pl.BlockSpec((ns, tl, W_C), lambda i, j: (i, j, 0)),
                   state_spec,
                   pl.BlockSpec((ns, CONV_W - 1, QKV_C), lambda i, j: (i, 0, 0))),
        scratch_shapes=[
            pltpu.VMEM((3, ns, tl + 8, 256), f32),
            pltpu.VMEM((ns, H_C, DK_C, DV_C), f32),
        ],
        compiler_params=_params(2),
        name=f"delta_t{t}",
    )(p3, p3, p3, p3, p3, hist, s0, conv_w, alog, dtb, ng)
    return y.reshape(b * t, W_C), s_new, conv_new


def _gla_kernel(q_ref, k_ref, v_ref, gate_ref, sm_ref, s0_ref, wgk_ref, bgk_ref, ng_ref,
                y_ref, sout_ref, qs_ref, g_ref, o_ref, s_ref, *, tb, c):
    t_idx = pl.program_id(1)
    blk = _iota((WK_D, W_D), 0) // DK_D == _iota((WK_D, W_D), 1) // DV_D
    seg_kv = blk.astype(bf16)

    @pl.when(t_idx == 0)
    def _():
        s_ref[...] = jnp.zeros((WK_D, W_D), f32)
        for h in range(H_D):
            s_ref[h * DK_D:(h + 1) * DK_D, h * DV_D:(h + 1) * DV_D] = s0_ref[h]

    gk = jax.nn.log_sigmoid(_dot(sm_ref[...], wgk_ref[...]) + bgk_ref[...]) * (1.0 / GLA_GATE_NORM)
    g_ref[...] = gk
    qs_ref[...] = q_ref[...] * (DK_D ** -0.5)
    ltri = (_iota((c, c), 0) >= _iota((c, c), 1)).astype(bf16)
    row8 = _iota((8, WK_D), 0)
    rowid = _iota((c, WK_D), 0)
    lane_head_k = _iota((GLA_SUB, WK_D), 1) // DK_D
    lane_head_v = _iota((GLA_SUB, W_D), 1) // DV_D

    def chunk(ic, st):
        rows = pl.ds(pl.multiple_of(ic * c, c), c)
        g_hi, g_lo = _split(g_ref[rows, :])
        g = _dot(ltri, g_hi) + _dot(ltri, g_lo)
        q, k, v = qs_ref[rows, :], k_ref[rows, :], v_ref[rows, :]
        o_state = _dot(q * jnp.exp(g), st)
        o8 = [None] * (c // 8)
        for jb in range(c // 8):
            lo = 8 * jb
            hi = GLA_SUB * (lo // GLA_SUB + 1)
            m = hi - lo
            e = []
            for j in range(lo, lo + 8):
                ej = jnp.exp(g[lo:hi] - g[j:j + 1]) * (q[lo:hi] * k[j:j + 1])
                first = jnp.where(row8 >= j - lo, ej[:8], 0.0)
                e.append(first if m == 8 else jnp.concatenate([first, ej[8:]], axis=0))
            att = _dot(jnp.concatenate(e, axis=0).astype(bf16), seg_kv)
            upd = att[0:m] * v[lo:lo + 1]
            for jj in range(1, 8):
                upd = upd + att[jj * m:(jj + 1) * m] * v[lo + jj:lo + jj + 1]
            for r in range(m // 8):
                piece = upd[8 * r:8 * (r + 1)]
                o8[jb + r] = piece if o8[jb + r] is None else o8[jb + r] + piece
        for sb in range(1, c // GLA_SUB):
            r0 = sb * GLA_SUB
            ref = g[r0 - 1:r0]
            qx = q[r0:r0 + GLA_SUB] * jnp.exp(g[r0:r0 + GLA_SUB] - ref)
            kx = jnp.where(rowid < r0, k * jnp.exp(jnp.minimum(ref - g, 0.0)), 0.0)
            qh = jnp.concatenate([jnp.where(lane_head_k == h, qx, 0.0) for h in range(H_D)], axis=0)
            res = _dot(_dot_nt(qh, kx), v)
            upd = None
            for h in range(H_D):
                part = jnp.where(lane_head_v == h, res[h * GLA_SUB:(h + 1) * GLA_SUB], 0.0)
                upd = part if upd is None else upd + part
            for r in range(GLA_SUB // 8):
                o8[r0 // 8 + r] = o8[r0 // 8 + r] + upd[8 * r:8 * (r + 1)]
        o_ref[rows, :] = o_state + jnp.concatenate(o8, axis=0)
        gl = g[c - 1:c]
        egl_col = jnp.exp(g.T[:, c - 1:c])
        s_new = egl_col * st + _dot_tn(k * jnp.exp(gl - g), v)
        return jnp.where(blk, s_new, 0.0)

    s_ref[...] = lax.fori_loop(0, tb // c, chunk, s_ref[...], unroll=min(GLA_UNROLL, tb // c))

    o = o_ref[...]
    o = o * lax.rsqrt(_dot_sel(o * o, _seg_ones(W_D, DV_D, W_D, DV_D)) * (1.0 / DV_D) + EPS) * ng_ref[...]
    y_ref[...] = o * _silu(gate_ref[...])

    @pl.when(t_idx == pl.num_programs(1) - 1)
    def _():
        for h in range(H_D):
            sout_ref[h] = s_ref[h * DK_D:(h + 1) * DK_D, h * DV_D:(h + 1) * DV_D]


def _gla_call(p, b, t, s0, lh, wgk, bgk, ng, l):
    tb = min(t, ROW_TILE)
    c = min(CHUNK, t)
    nt = t // tb

    def col(cb):
        return lambda i, j: (i * nt + j, cb)

    layer = lambda i, j: (l, 0, 0)
    return pl.pallas_call(
        functools.partial(_gla_kernel, tb=tb, c=c),
        out_shape=(jax.ShapeDtypeStruct((b * t, W_D), f32),
                   jax.ShapeDtypeStruct((b, H_D, DK_D, DV_D), f32)),
        grid=(b, nt),
        in_specs=[
            pl.BlockSpec((tb, WK_D), col(P_QD // WK_D)),
            pl.BlockSpec((tb, WK_D), col(P_KD // WK_D)),
            pl.BlockSpec((tb, W_D), col(P_VD // W_D)),
            pl.BlockSpec((tb, W_D), col(P_GD // W_D)),
            pl.BlockSpec((tb, SMALL_W), col(P_SMALL // SMALL_W)),
            pl.BlockSpec((None, None, H_D, DK_D, DV_D), lambda i, j: (lh, i, 0, 0, 0)),
            pl.BlockSpec((None, SMALL_W, WK_D), layer),
            pl.BlockSpec((None, 1, WK_D), layer),
            pl.BlockSpec((None, 1, W_D), layer),
        ],
        out_specs=(pl.BlockSpec((tb, W_D), lambda i, j: (i * nt + j, 0)),
                   pl.BlockSpec((None, H_D, DK_D, DV_D), lambda i, j: (i, 0, 0, 0))),
        scratch_shapes=[
            pltpu.VMEM((tb, WK_D), f32), pltpu.VMEM((tb, WK_D), f32),
            pltpu.VMEM((tb, W_D), f32),
            pltpu.VMEM((WK_D, W_D), f32),
        ],
        compiler_params=_params(2),
        name=f"gla_t{t}",
    )(p, p, p, p, p, s0, wgk, bgk, ng)


def _reorder_w_in(w_in):
    pad = jnp.zeros(w_in.shape[:-1] + (P_COLS - P_SMALL - 2 * H_C - GLA_RANK,), w_in.dtype)
    return jnp.concatenate([
        w_in[..., 0:2048],
        w_in[..., 2312:2568],
        w_in[..., 2568:2824],
        w_in[..., 2056:2184],
        w_in[..., 2184:2312],
        w_in[..., 2048:2056],
        w_in[..., 2824:2840],
        pad], axis=-1)


def _lane_pad(x, width):
    return jnp.pad(x, ((0, 0), (0, width - x.shape[-1])))[:, None, :]


def kernel(x_prompt, x_sample, cache_pool, cache_attn_k, cache_attn_v, state_conv, state_delta,
           state_gla, attn_norm_g, w_in, pool_w, pool_scale, rel_bias, conv_w, a_log, dt_bias,
           delta_norm_g, gla_w_gk, gla_b_gk, gla_norm_g, w_out, mlp_norm_g, w_up, w_down,
           final_norm_g):
    bp, tp, _ = x_prompt.shape
    bs, ts, _ = x_sample.shape
    w_in_r = _reorder_w_in(w_in.astype(bf16))
    w_out_b, w_up_b, w_down_b = w_out.astype(bf16), w_up.astype(bf16), w_down.astype(bf16)
    g1 = attn_norm_g[:, None, :]
    g2 = mlp_norm_g[:, None, :]
    gf = final_norm_g.reshape(1, D_MODEL)
    pool_wbd = jnp.zeros((DEPTH, W_A, W_A), f32)
    for gi in range(N_POOL):
        sl = slice(gi * POOL_GW, (gi + 1) * POOL_GW)
        pool_wbd = pool_wbd.at[:, sl, sl].set(pool_w[:, gi])
    pool_sc = pool_scale[:, None, :]
    band_bias = _band_bias_row(rel_bias)
    n_cache = cache_attn_k.shape[3]
    sample_bias = _rel_bias_rows(rel_bias, ts, n_cache + ts, n_cache)
    cache_kt, cache_vt = jnp.swapaxes(cache_attn_k, 3, 4), jnp.swapaxes(cache_attn_v, 3, 4)
    alog = _lane_pad(a_log, SMALL_W)
    dtb = _lane_pad(dt_bias, SMALL_W)
    ng_c = jnp.tile(delta_norm_g, (1, H_C))[:, None, :]
    wgk = jnp.zeros((DEPTH, SMALL_W, WK_D), f32).at[:, 2 * H_C:2 * H_C + GLA_RANK].set(gla_w_gk)
    bgk = gla_b_gk[:, None, :]
    ng_d = jnp.tile(gla_norm_g, (1, H_D))[:, None, :]
    zeros = (jnp.zeros((1, bp, POOL_HIST, W_A), f32), jnp.zeros((1, bp, CONV_W - 1, QKV_C), f32),
             jnp.zeros((1, bp, H_C, DK_C, DV_C), f32), jnp.zeros((1, bp, H_D, DK_D, DV_D), f32))
    carried = (cache_pool, state_conv, state_delta, state_gla)

    xs = [x_prompt.reshape(bp * tp, D_MODEL), x_sample.reshape(bs * ts, D_MODEL)]
    states = ([], [])
    for l in range(DEPTH):
        for s, (b, t, pos0) in enumerate(((bp, tp, 0), (bs, ts, tp))):
            pool_h, conv_h, sd_h, sg_h = zeros if s == 0 else carried
            lh = 0 if s == 0 else l
            p = _inproj(xs[s], g1, w_in_r, l)
            y_a, st_pool = _pool_call(p, b, t, pos0, pool_h, lh, pool_wbd, pool_sc, l)
            if s == 0:
                y_b, st_k, st_v = _band_attn_call(p, b, t, band_bias, l)
            else:
                y_b, st_k, st_v = _sample_attn_call(p, b, t, cache_kt, cache_vt, sample_bias, l)
            y_c, st_d, st_conv = _delta_call(p, b, t, conv_h, sd_h, lh, conv_w, alog, dtb, ng_c, l)
            y_d, st_g = _gla_call(p, b, t, sg_h, lh, wgk, bgk, ng_d, l)
            xs[s] = _outmlp(xs[s], (y_a, y_b, y_c, y_d), w_out_b, g2, w_up_b, w_down_b, gf, l,
                            final=(l == DEPTH - 1))
            states[s].append((st_pool, st_k, st_v, st_conv, st_d, st_g))
    y_prompt = xs[0].reshape(bp, tp, D_MODEL)
    y_sample = xs[1].reshape(bs, ts, D_MODEL)
    outs_p = tuple(jnp.stack([st[i] for st in states[0]]) for i in range(6))
    outs_s = tuple(jnp.stack([st[i] for st in states[1]]) for i in range(6))
    return (y_prompt, y_sample) + outs_p + outs_s
```

```python
import functools

import jax
import jax.numpy as jnp
import numpy as np
from jax import lax
from jax.experimental import pallas as pl
from jax.experimental.pallas import tpu as pltpu

D_MODEL = 1024
DEPTH = 4
CHUNK = 64
W_A = W_B = W_C = W_D = 256
POOL_WINDOWS = (2, 4, 8, 16)
N_POOL = 4
POOL_GW = 64
POOL_HIST = 15
H_B = 4
DH_B = 64
BAND_CHUNKS = 8
ATTN_WINDOW = 512
REL_CLIP = 256
H_C = 4
DK_C = 64
DV_C = 64
CONV_W = 4
QKV_C = 768
H_D = 4
DK_D = 32
DV_D = 64
WK_D = 128
GLA_RANK = 16
GLA_GATE_NORM = 16.0
D_FF = 4096
EPS = 1e-6
NEG_INF = -1e30

P_UA = 0
P_QB, P_KB, P_VB = 256, 512, 768
P_QKVC = 1024
P_ZC = 1792
P_VD = 2048
P_GD = 2304
P_QD = 2560
P_KD = 2688
P_SMALL = 2816
SMALL_W = 128
P_COLS = 2944

ROW_TILE = 512
FF_CHUNK = 1024
POOL_TILE = 2048
SAMPLE_SEQS = 8
GLA_SUB = 16
GLA_UNROLL = 4
DELTA_ROWS = 1024
DELTA_SEQS = 8
Q_BLOCK = 256
K_WINDOW = Q_BLOCK + BAND_CHUNKS * CHUNK
VMEM_LIMIT = 56 * 1024 * 1024

f32 = jnp.float32
bf16 = jnp.bfloat16


def _params(n_axes):
    return pltpu.CompilerParams(dimension_semantics=("arbitrary",) * n_axes, vmem_limit_bytes=VMEM_LIMIT)


def _dot(a, b):
    return jnp.dot(a, b, preferred_element_type=f32)


def _dot_nt(a, b):
    return lax.dot_general(a, b, (((1,), (1,)), ((), ())), preferred_element_type=f32)


def _dot_tn(a, b):
    return lax.dot_general(a, b, (((0,), (0,)), ((), ())), preferred_element_type=f32)


def _bmm(a, b):
    return jnp.einsum('nij,njk->nik', a, b, preferred_element_type=f32)


def _bmm_nt(a, b):
    return jnp.einsum('nid,njd->nij', a, b, preferred_element_type=f32)


def _bmm_tn(a, b):
    return jnp.einsum('nji,njk->nik', a, b, preferred_element_type=f32)


def _split(x):
    hi = x.astype(bf16)
    return hi, (x - hi.astype(f32)).astype(bf16)


def _dot_sel(x, sel):
    hi, lo = _split(x)
    return _dot(hi, sel) + _dot(lo, sel)


def _bmm_sel(sel, x):
    hi, lo = _split(x)
    return _bmm(sel, hi) + _bmm(sel, lo)


def _iota(shape, dim):
    return lax.broadcasted_iota(jnp.int32, shape, dim)


def _seg_ones(n_in, seg_in, n_out, seg_out):
    return (_iota((n_in, n_out), 0) // seg_in == _iota((n_in, n_out), 1) // seg_out).astype(bf16)


def _silu(x):
    return x * jax.nn.sigmoid(x)


def _rms(x):
    return x * lax.rsqrt(jnp.mean(x * x, axis=-1, keepdims=True) + EPS)


def _inproj_kernel(x_ref, g_ref, w_ref, o_ref):
    h = _rms(x_ref[...]) * g_ref[...]
    o_ref[...] = _dot(h.astype(bf16), w_ref[...])


def _inproj(x, g, w, l):
    n = x.shape[0]
    return pl.pallas_call(
        _inproj_kernel,
        out_shape=jax.ShapeDtypeStruct((n, P_COLS), f32),
        grid=(n // ROW_TILE,),
        in_specs=[
            pl.BlockSpec((ROW_TILE, D_MODEL), lambda i: (i, 0)),
            pl.BlockSpec((None, 1, D_MODEL), lambda i: (l, 0, 0)),
            pl.BlockSpec((None, D_MODEL, P_COLS), lambda i: (l, 0, 0), pipeline_mode=pl.Buffered(1)),
        ],
        out_specs=pl.BlockSpec((ROW_TILE, P_COLS), lambda i: (i, 0)),
        compiler_params=_params(1),
        name="inproj",
    )(x, g, w)


def _outmlp_kernel(x_ref, ya_ref, yb_ref, yc_ref, yd_ref, wo_ref, g2_ref, wu_ref, wd_ref, gf_ref,
                   o_ref, *, final):
    acc = None
    for k, y_ref in enumerate((ya_ref, yb_ref, yc_ref, yd_ref)):
        d = _dot(y_ref[...].astype(bf16), wo_ref[k * 256:(k + 1) * 256, :])
        acc = d if acc is None else acc + d
    x = x_ref[...] + acc
    h2 = (_rms(x) * g2_ref[...]).astype(bf16)
    acc = None
    for c in range(D_FF // FF_CHUNK):
        up = _dot(h2, wu_ref[:, c * FF_CHUNK:(c + 1) * FF_CHUNK])
        up = jnp.square(jnp.maximum(up, 0.0)).astype(bf16)
        d = _dot(up, wd_ref[c * FF_CHUNK:(c + 1) * FF_CHUNK, :])
        acc = d if acc is None else acc + d
    x = x + acc
    if final:
        x = _rms(x) * gf_ref[...]
    o_ref[...] = x


def _outmlp(x, ys, wo, g2, wu, wd, gf, l, final):
    n = x.shape[0]
    row = lambda i: (i, 0)
    layer = lambda i: (l, 0, 0)
    single = pl.Buffered(1)
    return pl.pallas_call(
        functools.partial(_outmlp_kernel, final=final),
        out_shape=jax.ShapeDtypeStruct((n, D_MODEL), f32),
        grid=(n // ROW_TILE,),
        in_specs=[pl.BlockSpec((ROW_TILE, D_MODEL), row)]
        + [pl.BlockSpec((ROW_TILE, 256), row)] * 4
        + [pl.BlockSpec((None, D_MODEL, D_MODEL), layer, pipeline_mode=single),
           pl.BlockSpec((None, 1, D_MODEL), layer),
           pl.BlockSpec((None, D_MODEL, D_FF), layer, pipeline_mode=single),
           pl.BlockSpec((None, D_FF, D_MODEL), layer, pipeline_mode=single),
           pl.BlockSpec((1, D_MODEL), lambda i: (0, 0))],
        out_specs=pl.BlockSpec((ROW_TILE, D_MODEL), row),
        compiler_params=_params(1),
        name="outmlp_final" if final else "outmlp",
    )(x, *ys, wo, g2, wu, wd, gf)


def _pool_kernel(u_ref, hist_ref, w_ref, scale_ref, y_ref, st_ref, ext_ref, *, ns, tl, pos0):
    t_idx = pl.program_id(1)

    @pl.when(t_idx == 0)
    def _():
        ext_ref[:, 0:8, :] = jnp.zeros((ns, 8, W_A), f32)
        ext_ref[:, 1:16, :] = hist_ref[...]

    x = u_ref[...]
    ext_ref[:, 16:16 + tl, :] = x
    wsum = {}
    acc = x
    for k in range(1, 16):
        acc = acc + ext_ref[:, 16 - k:16 - k + tl, :]
        if k + 1 in POOL_WINDOWS:
            wsum[k + 1] = acc
    ext_ref[:, 1:16, :] = ext_ref[:, tl + 1:tl + 16, :]
    lane_group = _iota((ns, tl, W_A), 2) // POOL_GW
    pos = pos0 + t_idx * tl + _iota((ns, tl, W_A), 1)
    ws, win = wsum[POOL_WINDOWS[-1]], jnp.full((ns, tl, W_A), POOL_WINDOWS[-1], jnp.int32)
    for gi in range(N_POOL - 2, -1, -1):
        ws = jnp.where(lane_group == gi, wsum[POOL_WINDOWS[gi]], ws)
        win = jnp.where(lane_group == gi, POOL_WINDOWS[gi], win)
    cnt = jnp.minimum(win, pos + 1).astype(f32)
    pooled = (ws / cnt - x).reshape(ns * tl, W_A)
    y_ref[...] = (_dot(pooled, w_ref[...]) * scale_ref[...]).reshape(ns, tl, W_A)

    @pl.when(t_idx == pl.num_programs(1) - 1)
    def _():
        st_ref[...] = ext_ref[:, 1:16, :]


def _pool_call(p, b, t, pos0, hist, lh, wbd, scale, l):
    tl = min(t, POOL_TILE)
    ns = min(b, POOL_TILE // tl)
    nt = t // tl
    y, st = pl.pallas_call(
        functools.partial(_pool_kernel, ns=ns, tl=tl, pos0=pos0),
        out_shape=(jax.ShapeDtypeStruct((b, t, W_A), f32),
                   jax.ShapeDtypeStruct((b, POOL_HIST, W_A), f32)),
        grid=(b // ns, nt),
        in_specs=[
            pl.BlockSpec((ns, tl, W_A), lambda i, j: (i, j, P_UA // W_A)),
            pl.BlockSpec((None, ns, POOL_HIST, W_A), lambda i, j: (lh, i, 0, 0)),
            pl.BlockSpec((None, W_A, W_A), lambda i, j: (l, 0, 0)),
            pl.BlockSpec((None, 1, W_A), lambda i, j: (l, 0, 0)),
        ],
        out_specs=(pl.BlockSpec((ns, tl, W_A), lambda i, j: (i, j, 0)),
                   pl.BlockSpec((ns, POOL_HIST, W_A), lambda i, j: (i, 0, 0))),
        scratch_shapes=[pltpu.VMEM((ns, tl + 16, W_A), f32)],
        compiler_params=_params(2),
        name=f"pool_t{t}",
    )(p.reshape(b, t, P_COLS), hist, wbd, scale)
    return y.reshape(b * t, W_A), st


def _softmax_av(s_parts, av_fns):
    m = functools.reduce(jnp.maximum, [jnp.max(s, axis=-1, keepdims=True) for s in s_parts])
    e_parts = [jnp.exp(s - m) for s in s_parts]
    l = functools.reduce(jnp.add, [jnp.sum(e, axis=-1, keepdims=True) for e in e_parts])
    o = functools.reduce(jnp.add, [av(e) for e, av in zip(e_parts, av_fns)])
    return o / l


BIAS_PERIOD = 2 * K_WINDOW


def _band_attn_kernel(q_ref, k_ref, v_ref, wrow_ref, y_ref, kc_ref, vc_ref, bias_ref, *, t, keep):
    qb = pl.program_id(1)

    @pl.when((pl.program_id(0) == 0) & (qb == 0))
    def _():
        qi = _iota((Q_BLOCK, K_WINDOW), 0)
        kj = _iota((Q_BLOCK, K_WINDOW), 1)
        for h in range(H_B):
            base = pltpu.roll(jnp.broadcast_to(wrow_ref[h], (Q_BLOCK, BIAS_PERIOD)), 0, 1,
                              stride=1, stride_axis=0)
            for v in range(3):
                dchunk = (qi + v * Q_BLOCK) // CHUNK - kj // CHUNK
                vis = (dchunk >= 0) & (dchunk <= BAND_CHUNKS)
                lo = (2 - v) * Q_BLOCK
                bias_ref[v, h] = jnp.where(vis, base[:, lo:lo + K_WINDOW], NEG_INF)

    start = pl.multiple_of(jnp.maximum(qb * Q_BLOCK - BAND_CHUNKS * CHUNK, 0), Q_BLOCK)
    variant = jnp.minimum(qb, 2)
    outs = []
    for h in range(H_B):
        lanes = slice(h * DH_B, (h + 1) * DH_B)
        q = q_ref[:, lanes] * (DH_B ** -0.5)
        k = k_ref[pl.ds(start, K_WINDOW), lanes]
        v = v_ref[pl.ds(start, K_WINDOW), lanes]
        s = _dot_nt(q, k) + bias_ref[variant, h]
        outs.append(_softmax_av([s], [lambda e, v=v: _dot(e, v)]))
    y_ref[...] = jnp.concatenate(outs, axis=1)

    @pl.when(qb == pl.num_programs(1) - 1)
    def _():
        for h in range(H_B):
            lanes = slice(h * DH_B, (h + 1) * DH_B)
            kc_ref[h] = k_ref[t - keep:t, lanes]
            vc_ref[h] = v_ref[t - keep:t, lanes]


def _rel_bias_rows(table, nq, nk, off):
    period = nq + nk
    m = np.arange(period)
    rel = np.where(m < nk, m, m - period) - off
    row = table.astype(f32)[..., np.clip(rel, -REL_CLIP, REL_CLIP) + REL_CLIP]
    lead = table.shape[:-1]
    flat = jnp.tile(row, (1,) * len(lead) + (nq,))[..., :nq * (period - 1)]
    return flat.reshape(lead + (nq, period - 1))[..., :nk]


def _band_bias_row(table):
    m = np.arange(BIAS_PERIOD)
    e = np.where(m < 2 * Q_BLOCK + K_WINDOW, m, m - BIAS_PERIOD)
    idx = np.clip(e - 2 * Q_BLOCK, -REL_CLIP, REL_CLIP) + REL_CLIP
    return table.astype(f32)[..., idx][:, :, None, :]


def _band_attn_call(p, b, t, wrow, l):
    nq = t // Q_BLOCK
    keep = min(ATTN_WINDOW, t)
    cache = jax.ShapeDtypeStruct((b, H_B, keep, DH_B), f32)
    cache_spec = pl.BlockSpec((None, H_B, keep, DH_B), lambda i, j: (i, 0, 0, 0))
    return pl.pallas_call(
        functools.partial(_band_attn_kernel, t=t, keep=keep),
        out_shape=(jax.ShapeDtypeStruct((b * t, W_B), f32), cache, cache),
        grid=(b, nq),
        in_specs=[
            pl.BlockSpec((Q_BLOCK, W_B), lambda i, j: (i * nq + j, P_QB // W_B)),
            pl.BlockSpec((t, W_B), lambda i, j: (i, P_KB // W_B)),
            pl.BlockSpec((t, W_B), lambda i, j: (i, P_VB // W_B)),
            pl.BlockSpec((None, H_B, 1, BIAS_PERIOD), lambda i, j: (l, 0, 0, 0)),
        ],
        out_specs=(pl.BlockSpec((Q_BLOCK, W_B), lambda i, j: (i * nq + j, 0)), cache_spec, cache_spec),
        scratch_shapes=[pltpu.VMEM((3, H_B, Q_BLOCK, K_WINDOW), f32)],
        compiler_params=_params(2),
        name="band_attn",
    )(p, p, p, wrow)


def _sample_attn_kernel(q_ref, k_ref, v_ref, kct_ref, vct_ref, bias_ref, y_ref, kn_ref, vn_ref,
                        *, ns, n_cache):
    def one_sequence(s, carry):
        outs = []
        for h in range(H_B):
            lanes = slice(h * DH_B, (h + 1) * DH_B)
            q = q_ref[s, :, lanes] * (DH_B ** -0.5)
            k_new, v_new = k_ref[s, :, lanes], v_ref[s, :, lanes]
            s_old = _dot(q, kct_ref[s, h]) + bias_ref[h, :, 0:n_cache]
            s_new = _dot_nt(q, k_new) + bias_ref[h, :, n_cache:]
            outs.append(_softmax_av(
                [s_old, s_new],
                [lambda e, h=h: _dot_nt(e, vct_ref[s, h]), lambda e, v=v_new: _dot(e, v)]))
            kn_ref[s, h] = k_new
            vn_ref[s, h] = v_new
        y_ref[s] = jnp.concatenate(outs, axis=1)
        return carry

    lax.fori_loop(0, ns, one_sequence, 0, unroll=4)


def _sample_attn_call(p, b, t, k_cache_t, v_cache_t, bias, l):
    n_cache = k_cache_t.shape[4]
    ns = min(b, SAMPLE_SEQS)
    cache_spec = pl.BlockSpec((None, ns, H_B, DH_B, n_cache), lambda i: (l, i, 0, 0, 0))
    new = jax.ShapeDtypeStruct((b, H_B, t, DH_B), f32)
    new_spec = pl.BlockSpec((ns, H_B, t, DH_B), lambda i: (i, 0, 0, 0))
    p3 = p.reshape(b, t, P_COLS)
    y, k_new, v_new = pl.pallas_call(
        functools.partial(_sample_attn_kernel, ns=ns, n_cache=n_cache),
        out_shape=(jax.ShapeDtypeStruct((b, t, W_B), f32), new, new),
        grid=(b // ns,),
        in_specs=[
            pl.BlockSpec((ns, t, W_B), lambda i: (i, 0, P_QB // W_B)),
            pl.BlockSpec((ns, t, W_B), lambda i: (i, 0, P_KB // W_B)),
            pl.BlockSpec((ns, t, W_B), lambda i: (i, 0, P_VB // W_B)),
            cache_spec, cache_spec,
            pl.BlockSpec((None, H_B, t, n_cache + t), lambda i: (l, 0, 0, 0)),
        ],
        out_specs=(pl.BlockSpec((ns, t, W_B), lambda i: (i, 0, 0)), new_spec, new_spec),
        compiler_params=_params(1),
        name="sample_attn",
    )(p3, p3, p3, k_cache_t, v_cache_t, bias)
    return y.reshape(b * t, W_B), k_new, v_new


def _unit_lower_inverse(a):
    n, c, _ = a.shape
    ri, ci = _iota((c, c), 0), _iota((c, c), 1)
    d = jnp.where(ri == ci, 1.0, 0.0)[None] - jnp.where(((ri // 2 == ci // 2) & (ri > ci))[None], a, 0.0)
    s = 2
    while s < c:
        lower_left = (ri // (2 * s) == ci // (2 * s)) & (ri % (2 * s) >= s) & (ci % (2 * s) < s)
        e = jnp.where(lower_left[None], a, 0.0)
        d = d - _bmm(d, _bmm(e, d))
        s *= 2
    return d


def _delta_kernel(q_ref, k_ref, v_ref, z_ref, sm_ref, hist_ref, s0_ref, cw_ref, alog_ref, dtb_ref, ng_ref,
                  y_ref, sout_ref, cout_ref, ext_ref, s_ref, *, ns, tl, c, carry):
    t_idx = pl.program_id(1)
    tb = ns * tl
    nch = tb // c
    nb = H_C * nch

    def heads(x):
        return jnp.concatenate(
            [x[:, h * 64:(h + 1) * 64].reshape(nch, c, 64) for h in range(H_C)], axis=0)

    if carry:
        @pl.when(t_idx == 0)
        def _():
            s_ref[...] = s0_ref[...]
            for s in range(3):
                ext_ref[s, :, 0:8, :] = jnp.zeros((ns, 8, 256), f32)
                ext_ref[s, :, 5:8, :] = hist_ref[:, :, s * 256:(s + 1) * 256]
    else:
        for s in range(3):
            ext_ref[s, :, 5:8, :] = hist_ref[:, :, s * 256:(s + 1) * 256]

    parts = []
    for s, ref in enumerate((q_ref, k_ref, v_ref)):
        ext_ref[s, :, 8:8 + tl, :] = ref[...]
        acc = ext_ref[s, :, 5:5 + tl, :] * cw_ref[0:1, s * 256:(s + 1) * 256]
        for j in range(1, CONV_W):
            acc = acc + ext_ref[s, :, 5 + j:5 + j + tl, :] * cw_ref[j:j + 1, s * 256:(s + 1) * 256]
        parts.append(_silu(acc).reshape(tb, 256))
        last = ext_ref[s, :, tl + 5:tl + 8, :]
        if carry:
            ext_ref[s, :, 5:8, :] = last
        cout_ref[:, :, s * 256:(s + 1) * 256] = last
    cq, ck, cv = parts
    seg = _seg_ones(256, DK_C, 256, DK_C)
    qn = cq * lax.rsqrt(_dot_sel(cq * cq, seg) + EPS) * (DK_C ** -0.5)
    kn = ck * lax.rsqrt(_dot_sel(ck * ck, seg) + EPS)

    sm = sm_ref[...].reshape(tb, SMALL_W)
    g = -jnp.exp(alog_ref[...]) * jax.nn.softplus(sm + dtb_ref[...])
    beta = jax.nn.sigmoid(sm)
    ri, ci = _iota((c, c), 0), _iota((c, c), 1)
    incl = (ri >= ci)[None]
    ltri = jnp.broadcast_to(incl.astype(bf16), (nch, c, c))
    gcum = _bmm_sel(ltri, g.reshape(nch, c, SMALL_W)).reshape(tb, SMALL_W)
    lane_head = _iota((SMALL_W, 256), 1) // 64
    gb = _dot_sel(gcum, (_iota((SMALL_W, 256), 0) == lane_head).astype(bf16))
    betab = _dot_sel(beta, (_iota((SMALL_W, 256), 0) == lane_head + H_C).astype(bf16))
    eg = jnp.exp(gb)
    gl = jnp.broadcast_to(gb.reshape(nch, c, 256)[:, c - 1:c, :], (nch, c, 256)).reshape(tb, 256)

    k3, q3 = heads(kn), heads(qn)
    gcb = heads(gb)[:, :, :c]
    grb = jnp.swapaxes(gcb, 1, 2)
    dec = jnp.where(incl, jnp.exp(jnp.where(incl, gcb - grb, 0.0)), 0.0)
    a = jnp.where((ri > ci)[None], heads(betab)[:, :, :c] * _bmm_nt(k3, k3) * dec, 0.0)
    rhs = jnp.concatenate([heads(betab * cv), heads(betab * eg * kn)], axis=-1)
    sol = _bmm(_unit_lower_inverse(a), rhs)
    u, w = sol[:, :, :64], sol[:, :, 64:]
    qk = _bmm_nt(q3, k3) * dec
    qt3 = heads(qn * eg)
    kt3 = heads(kn * jnp.exp(gl - gb))
    egl = heads(jnp.exp(gl))[:, 0:1, :]

    if carry:
        o_heads = []
        for h in range(H_C):
            o_chunks = []
            for s in range(ns):
                st = s_ref[s, h]
                for ic in range(s * (tl // c), (s + 1) * (tl // c)):
                    n = h * nch + ic
                    un = u[n] - _dot(w[n], st)
                    o_chunks.append(_dot(qt3[n], st) + _dot(qk[n], un))
                    st = egl[n] * st + _dot_tn(kt3[n], un)
                s_ref[s, h] = st
            o_heads.append(jnp.concatenate(o_chunks, axis=0))
        o = jnp.concatenate(o_heads, axis=1)

        @pl.when(t_idx == pl.num_programs(1) - 1)
        def _():
            sout_ref[...] = s_ref[...]
    else:
        st = jnp.concatenate([s0_ref[:, h] for h in range(H_C)], axis=0)
        un = u - _bmm(w, st)
        o3 = _bmm(qt3, st) + _bmm(qk, un)
        st = egl * st + _bmm_tn(kt3, un)
        for h in range(H_C):
            sout_ref[:, h] = st[h * nch:(h + 1) * nch]
        o = jnp.concatenate([o3[h * nch:(h + 1) * nch].reshape(tb, 64) for h in range(H_C)], axis=1)

    o = o * lax.rsqrt(_dot_sel(o * o, seg) * (1.0 / DV_C) + EPS) * ng_ref[...]
    y_ref[...] = (o * _silu(z_ref[...].reshape(tb, W_C))).reshape(ns, tl, W_C)


def _delta_call(p, b, t, hist, s0, lh, conv_w, alog, dtb, ng, l):
    c = min(CHUNK, t)
    carry = t > c
    ns, tl = (DELTA_SEQS, DELTA_ROWS // DELTA_SEQS) if carry else (DELTA_ROWS // (4 * t), t)
    nt = t // tl
    p3 = p.reshape(b, t, P_COLS)

    def col(cb):
        return lambda i, j: (i, j, cb)

    layer = lambda i, j: (l, 0, 0)
    state_spec = pl.BlockSpec((ns, H_C, DK_C, DV_C), lambda i, j: (i, 0, 0, 0))
    y, s_new, conv_new = pl.pallas_call(
        functools.partial(_delta_kernel, ns=ns, tl=tl, c=c, carry=carry),
        out_shape=(jax.ShapeDtypeStruct((b, t, W_C), f32),
                   jax.ShapeDtypeStruct((b, H_C, DK_C, DV_C), f32),
                   jax.ShapeDtypeStruct((b, CONV_W - 1, QKV_C), f32)),
        grid=(b // ns, nt),
        in_specs=[
            pl.BlockSpec((ns, tl, 256), col(P_QKVC // 256)),
            pl.BlockSpec((ns, tl, 256), col(P_QKVC // 256 + 1)),
            pl.BlockSpec((ns, tl, 256), col(P_QKVC // 256 + 2)),
            pl.BlockSpec((ns, tl, 256), col(P_ZC // 256)),
            pl.BlockSpec((ns, tl, SMALL_W), col(P_SMALL // SMALL_W)),
            pl.BlockSpec((None, ns, CONV_W - 1, QKV_C), lambda i, j: (lh, i, 0, 0)),
            pl.BlockSpec((None, ns, H_C, DK_C, DV_C), lambda i, j: (lh, i, 0, 0, 0)),
            pl.BlockSpec((None, CONV_W, QKV_C), layer),
            pl.BlockSpec((None, 1, SMALL_W), layer),
            pl.BlockSpec((None, 1, SMALL_W), layer),
            pl.BlockSpec((None, 1, W_C), layer),
        ],
        out_specs=(pl.BlockSpec((ns, tl, W_C), lambda i, j: (i, j, 0)),
                   state_spec,
                   pl.BlockSpec((ns, CONV_W - 1, QKV_C), lambda i, j: (i, 0, 0))),
        scratch_shapes=[
            pltpu.VMEM((3, ns, tl + 8, 256), f32),
            pltpu.VMEM((ns, H_C, DK_C, DV_C), f32),
        ],
        compiler_params=_params(2),
        name=f"delta_t{t}",
    )(p3, p3, p3, p3, p3, hist, s0, conv_w, alog, dtb, ng)
    return y.reshape(b * t, W_C), s_new, conv_new


def _gla_kernel(q_ref, k_ref, v_ref, gate_ref, sm_ref, s0_ref, wgk_ref, bgk_ref, ng_ref,
                y_ref, sout_ref, qs_ref, g_ref, o_ref, s_ref, *, ns, tl, c, carry):
    t_idx = pl.program_id(1)
    tb = ns * tl
    blk = _iota((WK_D, W_D), 0) // DK_D == _iota((WK_D, W_D), 1) // DV_D
    seg_kv = blk.astype(bf16)

    def block_diag(s4):
        zero = jnp.zeros((DK_D, DV_D), f32)
        return jnp.concatenate(
            [jnp.concatenate([s4[h] if h2 == h else zero for h2 in range(H_D)], axis=1)
             for h in range(H_D)], axis=0)

    if carry:
        @pl.when(t_idx == 0)
        def _():
            s_ref[...] = block_diag(s0_ref[0])

    gk = jax.nn.log_sigmoid(_dot(sm_ref[...].reshape(tb, SMALL_W), wgk_ref[...]) + bgk_ref[...])
    g_ref[...] = gk * (1.0 / GLA_GATE_NORM)
    qs_ref[...] = q_ref[...].reshape(tb, WK_D) * (DK_D ** -0.5)
    ltri = (_iota((c, c), 0) >= _iota((c, c), 1)).astype(bf16)
    row8 = _iota((8, WK_D), 0)
    rowid = _iota((c, WK_D), 0)
    lane_head_k = _iota((GLA_SUB, WK_D), 1) // DK_D
    lane_head_v = _iota((GLA_SUB, W_D), 1) // DV_D

    def chunk(ic, st):
        rows = pl.ds(pl.multiple_of(ic * c, c), c)
        g_hi, g_lo = _split(g_ref[rows, :])
        g = _dot(ltri, g_hi) + _dot(ltri, g_lo)
        q = qs_ref[rows, :]
        if carry:
            k, v = k_ref[0, rows, :], v_ref[0, rows, :]
        else:
            k, v, st = k_ref[ic], v_ref[ic], block_diag(s0_ref[ic])
        o_state = _dot(q * jnp.exp(g), st)
        o8 = [None] * (c // 8)
        for jb in range(c // 8):
            lo = 8 * jb
            hi = GLA_SUB * (lo // GLA_SUB + 1)
            m = hi - lo
            e = []
            for j in range(lo, lo + 8):
                ej = jnp.exp(g[lo:hi] - g[j:j + 1]) * (q[lo:hi] * k[j:j + 1])
                first = jnp.where(row8 >= j - lo, ej[:8], 0.0)
                e.append(first if m == 8 else jnp.concatenate([first, ej[8:]], axis=0))
            att = _dot(jnp.concatenate(e, axis=0).astype(bf16), seg_kv)
            upd = att[0:m] * v[lo:lo + 1]
            for jj in range(1, 8):
                upd = upd + att[jj * m:(jj + 1) * m] * v[lo + jj:lo + jj + 1]
            for r in range(m // 8):
                piece = upd[8 * r:8 * (r + 1)]
                o8[jb + r] = piece if o8[jb + r] is None else o8[jb + r] + piece
        for sb in range(1, c // GLA_SUB):
            r0 = sb * GLA_SUB
            ref = g[r0 - 1:r0]
            qx = q[r0:r0 + GLA_SUB] * jnp.exp(g[r0:r0 + GLA_SUB] - ref)
            kx = jnp.where(rowid < r0, k * jnp.exp(jnp.minimum(ref - g, 0.0)), 0.0)
            qh = jnp.concatenate([jnp.where(lane_head_k == h, qx, 0.0) for h in range(H_D)], axis=0)
            res = _dot(_dot_nt(qh, kx), v)
            upd = None
            for h in range(H_D):
                part = jnp.where(lane_head_v == h, res[h * GLA_SUB:(h + 1) * GLA_SUB], 0.0)
                upd = part if upd is None else upd + part
            for r in range(GLA_SUB // 8):
                o8[r0 // 8 + r] = o8[r0 // 8 + r] + upd[8 * r:8 * (r + 1)]
        o_ref[rows, :] = o_state + jnp.concatenate(o8, axis=0)
        gl = g[c - 1:c]
        egl_col = jnp.exp(g.T[:, c - 1:c])
        s_new = jnp.where(blk, egl_col * st + _dot_tn(k * jnp.exp(gl - g), v), 0.0)
        if not carry:
            for h in range(H_D):
                sout_ref[ic, h] = s_new[h * DK_D:(h + 1) * DK_D, h * DV_D:(h + 1) * DV_D]
        return s_new

    s_first = s_ref[...] if carry else jnp.zeros((WK_D, W_D), f32)
    s_last = lax.fori_loop(0, tb // c, chunk, s_first, unroll=min(GLA_UNROLL, tb // c))

    o = o_ref[...]
    o = o * lax.rsqrt(_dot_sel(o * o, _seg_ones(W_D, DV_D, W_D, DV_D)) * (1.0 / DV_D) + EPS) * ng_ref[...]
    y_ref[...] = (o * _silu(gate_ref[...].reshape(tb, W_D))).reshape(ns, tl, W_D)

    if carry:
        s_ref[...] = s_last

        @pl.when(t_idx == pl.num_programs(1) - 1)
        def _():
            for h in range(H_D):
                sout_ref[0, h] = s_ref[h * DK_D:(h + 1) * DK_D, h * DV_D:(h + 1) * DV_D]


def _gla_call(p, b, t, s0, lh, wgk, bgk, ng, l):
    c = min(CHUNK, t)
    carry = t > c
    ns, tl = (1, min(t, ROW_TILE)) if carry else (min(b, SAMPLE_SEQS), t)
    tb = ns * tl
    nt = t // tl
    p3 = p.reshape(b, t, P_COLS)

    def col(cb):
        return lambda i, j: (i, j, cb)

    layer = lambda i, j: (l, 0, 0)
    y, s_new = pl.pallas_call(
        functools.partial(_gla_kernel, ns=ns, tl=tl, c=c, carry=carry),
        out_shape=(jax.ShapeDtypeStruct((b, t, W_D), f32),
                   jax.ShapeDtypeStruct((b, H_D, DK_D, DV_D), f32)),
        grid=(b // ns, nt),
        in_specs=[
            pl.BlockSpec((ns, tl, WK_D), col(P_QD // WK_D)),
            pl.BlockSpec((ns, tl, WK_D), col(P_KD // WK_D)),
            pl.BlockSpec((ns, tl, W_D), col(P_VD // W_D)),
            pl.BlockSpec((ns, tl, W_D), col(P_GD // W_D)),
            pl.BlockSpec((ns, tl, SMALL_W), col(P_SMALL // SMALL_W)),
            pl.BlockSpec((None, ns, H_D, DK_D, DV_D), lambda i, j: (lh, i, 0, 0, 0)),
            pl.BlockSpec((None, SMALL_W, WK_D), layer),
            pl.BlockSpec((None, 1, WK_D), layer),
            pl.BlockSpec((None, 1, W_D), layer),
        ],
        out_specs=(pl.BlockSpec((ns, tl, W_D), lambda i, j: (i, j, 0)),
                   pl.BlockSpec((ns, H_D, DK_D, DV_D), lambda i, j: (i, 0, 0, 0))),
        scratch_shapes=[
            pltpu.VMEM((tb, WK_D), f32), pltpu.VMEM((tb, WK_D), f32),
            pltpu.VMEM((tb, W_D), f32),
            pltpu.VMEM((WK_D, W_D), f32),
        ],
        compiler_params=_params(2),
        name=f"gla_t{t}",
    )(p3, p3, p3, p3, p3, s0, wgk, bgk, ng)
    return y.reshape(b * t, W_D), s_new


def _reorder_w_in(w_in):
    pad = jnp.zeros(w_in.shape[:-1] + (P_COLS - P_SMALL - 2 * H_C - GLA_RANK,), w_in.dtype)
    return jnp.concatenate([
        w_in[..., 0:2048],
        w_in[..., 2312:2568],
        w_in[..., 2568:2824],
        w_in[..., 2056:2184],
        w_in[..., 2184:2312],
        w_in[..., 2048:2056],
        w_in[..., 2824:2840],
        pad], axis=-1)


def _lane_pad(x, width):
    return jnp.pad(x, ((0, 0), (0, width - x.shape[-1])))[:, None, :]


def kernel(x_prompt, x_sample, cache_pool, cache_attn_k, cache_attn_v, state_conv, state_delta,
           state_gla, attn_norm_g, w_in, pool_w, pool_scale, rel_bias, conv_w, a_log, dt_bias,
           delta_norm_g, gla_w_gk, gla_b_gk, gla_norm_g, w_out, mlp_norm_g, w_up, w_down,
           final_norm_g):
    bp, tp, _ = x_prompt.shape
    bs, ts, _ = x_sample.shape
    w_in_r = _reorder_w_in(w_in.astype(bf16))
    w_out_b, w_up_b, w_down_b = w_out.astype(bf16), w_up.astype(bf16), w_down.astype(bf16)
    g1 = attn_norm_g[:, None, :]
    g2 = mlp_norm_g[:, None, :]
    gf = final_norm_g.reshape(1, D_MODEL)
    pool_wbd = jnp.zeros((DEPTH, W_A, W_A), f32)
    for gi in range(N_POOL):
        sl = slice(gi * POOL_GW, (gi + 1) * POOL_GW)
        pool_wbd = pool_wbd.at[:, sl, sl].set(pool_w[:, gi])
    pool_sc = pool_scale[:, None, :]
    band_bias = _band_bias_row(rel_bias)
    n_cache = cache_attn_k.shape[3]
    sample_bias = _rel_bias_rows(rel_bias, ts, n_cache + ts, n_cache)
    cache_kt, cache_vt = jnp.swapaxes(cache_attn_k, 3, 4), jnp.swapaxes(cache_attn_v, 3, 4)
    alog = _lane_pad(a_log, SMALL_W)
    dtb = _lane_pad(dt_bias, SMALL_W)
    ng_c = jnp.tile(delta_norm_g, (1, H_C))[:, None, :]
    wgk = jnp.zeros((DEPTH, SMALL_W, WK_D), f32).at[:, 2 * H_C:2 * H_C + GLA_RANK].set(gla_w_gk)
    bgk = gla_b_gk[:, None, :]
    ng_d = jnp.tile(gla_norm_g, (1, H_D))[:, None, :]
    zeros = (jnp.zeros((1, bp, POOL_HIST, W_A), f32), jnp.zeros((1, bp, CONV_W - 1, QKV_C), f32),
             jnp.zeros((1, bp, H_C, DK_C, DV_C), f32), jnp.zeros((1, bp, H_D, DK_D, DV_D), f32))
    carried = (cache_pool, state_conv, state_delta, state_gla)

    xs = [x_prompt.reshape(bp * tp, D_MODEL), x_sample.reshape(bs * ts, D_MODEL)]
    states = ([], [])
    for l in range(DEPTH):
        for s, (b, t, pos0) in enumerate(((bp, tp, 0), (bs, ts, tp))):
            pool_h, conv_h, sd_h, sg_h = zeros if s == 0 else carried
            lh = 0 if s == 0 else l
            p = _inproj(xs[s], g1, w_in_r, l)
            y_a, st_pool = _pool_call(p, b, t, pos0, pool_h, lh, pool_wbd, pool_sc, l)
            if s == 0:
                y_b, st_k, st_v = _band_attn_call(p, b, t, band_bias, l)
            else:
                y_b, st_k, st_v = _sample_attn_call(p, b, t, cache_kt, cache_vt, sample_bias, l)
            y_c, st_d, st_conv = _delta_call(p, b, t, conv_h, sd_h, lh, conv_w, alog, dtb, ng_c, l)
            y_d, st_g = _gla_call(p, b, t, sg_h, lh, wgk, bgk, ng_d, l)
            xs[s] = _outmlp(xs[s], (y_a, y_b, y_c, y_d), w_out_b, g2, w_up_b, w_down_b, gf, l,
                            final=(l == DEPTH - 1))
            states[s].append((st_pool, st_k, st_v, st_conv, st_d, st_g))
    y_prompt = xs[0].reshape(bp, tp, D_MODEL)
    y_sample = xs[1].reshape(bs, ts, D_MODEL)
    outs_p = tuple(jnp.stack([st[i] for st in states[0]]) for i in range(6))
    outs_s = tuple(jnp.stack([st[i] for st in states[1]]) for i in range(6))
    return (y_prompt, y_sample) + outs_p + outs_s
```

```python
import functools

import jax
import jax.numpy as jnp
import numpy as np
from jax import lax
from jax.experimental import pallas as pl
from jax.experimental.pallas import tpu as pltpu

D_MODEL = 1024
DEPTH = 4
CHUNK = 64
W_A = W_B = W_C = W_D = 256
POOL_WINDOWS = (2, 4, 8, 16)
N_POOL = 4
POOL_GW = 64
POOL_HIST = 15
H_B = 4
DH_B = 64
BAND_CHUNKS = 8
ATTN_WINDOW = 512
REL_CLIP = 256
H_C = 4
DK_C = 64
DV_C = 64
CONV_W = 4
QKV_C = 768
H_D = 4
DK_D = 32
DV_D = 64
WK_D = 128
GLA_RANK = 16
GLA_GATE_NORM = 16.0
D_FF = 4096
EPS = 1e-6
NEG_INF = -1e30

P_UA = 0
P_QB, P_KB, P_VB = 256, 512, 768
P_QKVC = 1024
P_ZC = 1792
P_VD = 2048
P_GD = 2304
P_QD = 2560
P_KD = 2688
P_SMALL = 2816
SMALL_W = 128
P_COLS = 2944

ROW_TILE = 512
FF_CHUNK = 1024
POOL_TILE = 2048
SAMPLE_SEQS = 8
GLA_SUB = 16
GLA_UNROLL = 4
DELTA_ROWS = 1024
DELTA_SEQS = 8
Q_BLOCK = 256
K_WINDOW = Q_BLOCK + BAND_CHUNKS * CHUNK
VMEM_LIMIT = 56 * 1024 * 1024

f32 = jnp.float32
bf16 = jnp.bfloat16


def _params(n_axes):
    return pltpu.CompilerParams(dimension_semantics=("arbitrary",) * n_axes, vmem_limit_bytes=VMEM_LIMIT)


def _dot(a, b):
    return jnp.dot(a, b, preferred_element_type=f32)


def _dot_nt(a, b):
    return lax.dot_general(a, b, (((1,), (1,)), ((), ())), preferred_element_type=f32)


def _dot_tn(a, b):
    return lax.dot_general(a, b, (((0,), (0,)), ((), ())), preferred_element_type=f32)


def _bmm(a, b):
    return jnp.einsum('nij,njk->nik', a, b, preferred_element_type=f32)


def _split(x):
    hi = x.astype(bf16)
    return hi, (x - hi.astype(f32)).astype(bf16)


def _dot_sel(x, sel):
    hi, lo = _split(x)
    return _dot(hi, sel) + _dot(lo, sel)


def _bmm_sel(sel, x):
    hi, lo = _split(x)
    return _bmm(sel, hi) + _bmm(sel, lo)


def _iota(shape, dim):
    return lax.broadcasted_iota(jnp.int32, shape, dim)


def _seg_ones(n_in, seg_in, n_out, seg_out):
    return (_iota((n_in, n_out), 0) // seg_in == _iota((n_in, n_out), 1) // seg_out).astype(bf16)


def _silu(x):
    return x * jax.nn.sigmoid(x)


def _rms(x):
    return x * lax.rsqrt(jnp.mean(x * x, axis=-1, keepdims=True) + EPS)


def _inproj_kernel(x_ref, g_ref, w_ref, o_ref):
    h = _rms(x_ref[...]) * g_ref[...]
    o_ref[...] = _dot(h.astype(bf16), w_ref[...])


def _inproj(x, g, w, l):
    n = x.shape[0]
    return pl.pallas_call(
        _inproj_kernel,
        out_shape=jax.ShapeDtypeStruct((n, P_COLS), f32),
        grid=(n // ROW_TILE,),
        in_specs=[
            pl.BlockSpec((ROW_TILE, D_MODEL), lambda i: (i, 0)),
            pl.BlockSpec((None, 1, D_MODEL), lambda i: (l, 0, 0)),
            pl.BlockSpec((None, D_MODEL, P_COLS), lambda i: (l, 0, 0), pipeline_mode=pl.Buffered(1)),
        ],
        out_specs=pl.BlockSpec((ROW_TILE, P_COLS), lambda i: (i, 0)),
        compiler_params=_params(1),
        name="inproj",
    )(x, g, w)


def _outmlp_kernel(x_ref, ya_ref, yb_ref, yc_ref, yd_ref, wo_ref, g2_ref, wu_ref, wd_ref, gf_ref,
                   o_ref, *, final):
    acc = None
    for k, y_ref in enumerate((ya_ref, yb_ref, yc_ref, yd_ref)):
        d = _dot(y_ref[...].astype(bf16), wo_ref[k * 256:(k + 1) * 256, :])
        acc = d if acc is None else acc + d
    x = x_ref[...] + acc
    h2 = (_rms(x) * g2_ref[...]).astype(bf16)
    acc = None
    for c in range(D_FF // FF_CHUNK):
        up = _dot(h2, wu_ref[:, c * FF_CHUNK:(c + 1) * FF_CHUNK])
        up = jnp.square(jnp.maximum(up, 0.0)).astype(bf16)
        d = _dot(up, wd_ref[c * FF_CHUNK:(c + 1) * FF_CHUNK, :])
        acc = d if acc is None else acc + d
    x = x + acc
    if final:
        x = _rms(x) * gf_ref[...]
    o_ref[...] = x


def _outmlp(x, ys, wo, g2, wu, wd, gf, l, final):
    n = x.shape[0]
    row = lambda i: (i, 0)
    layer = lambda i: (l, 0, 0)
    single = pl.Buffered(1)
    return pl.pallas_call(
        functools.partial(_outmlp_kernel, final=final),
        out_shape=jax.ShapeDtypeStruct((n, D_MODEL), f32),
        grid=(n // ROW_TILE,),
        in_specs=[pl.BlockSpec((ROW_TILE, D_MODEL), row)]
        + [pl.BlockSpec((ROW_TILE, 256), row)] * 4
        + [pl.BlockSpec((None, D_MODEL, D_MODEL), layer, pipeline_mode=single),
           pl.BlockSpec((None, 1, D_MODEL), layer),
           pl.BlockSpec((None, D_MODEL, D_FF), layer, pipeline_mode=single),
           pl.BlockSpec((None, D_FF, D_MODEL), layer, pipeline_mode=single),
           pl.BlockSpec((1, D_MODEL), lambda i: (0, 0))],
        out_specs=pl.BlockSpec((ROW_TILE, D_MODEL), row),
        compiler_params=_params(1),
        name="outmlp_final" if final else "outmlp",
    )(x, *ys, wo, g2, wu, wd, gf)


def _pool_kernel(u_ref, hist_ref, w_ref, scale_ref, y_ref, st_ref, ext_ref, *, ns, tl, pos0):
    t_idx = pl.program_id(1)

    @pl.when(t_idx == 0)
    def _():
        ext_ref[:, 0:8, :] = jnp.zeros((ns, 8, W_A), f32)
        ext_ref[:, 1:16, :] = hist_ref[...]

    x = u_ref[...]
    ext_ref[:, 16:16 + tl, :] = x
    wsum = {}
    acc = x
    for k in range(1, 16):
        acc = acc + ext_ref[:, 16 - k:16 - k + tl, :]
        if k + 1 in POOL_WINDOWS:
            wsum[k + 1] = acc
    ext_ref[:, 1:16, :] = ext_ref[:, tl + 1:tl + 16, :]
    lane_group = _iota((ns, tl, W_A), 2) // POOL_GW
    pos = pos0 + t_idx * tl + _iota((ns, tl, W_A), 1)
    ws, win = wsum[POOL_WINDOWS[-1]], jnp.full((ns, tl, W_A), POOL_WINDOWS[-1], jnp.int32)
    for gi in range(N_POOL - 2, -1, -1):
        ws = jnp.where(lane_group == gi, wsum[POOL_WINDOWS[gi]], ws)
        win = jnp.where(lane_group == gi, POOL_WINDOWS[gi], win)
    cnt = jnp.minimum(win, pos + 1).astype(f32)
    pooled = (ws / cnt - x).reshape(ns * tl, W_A)
    y_ref[...] = (_dot(pooled, w_ref[...]) * scale_ref[...]).reshape(ns, tl, W_A)

    @pl.when(t_idx == pl.num_programs(1) - 1)
    def _():
        st_ref[...] = ext_ref[:, 1:16, :]


def _pool_call(p, b, t, pos0, hist, lh, wbd, scale, l):
    tl = min(t, POOL_TILE)
    ns = min(b, POOL_TILE // tl)
    nt = t // tl
    y, st = pl.pallas_call(
        functools.partial(_pool_kernel, ns=ns, tl=tl, pos0=pos0),
        out_shape=(jax.ShapeDtypeStruct((b, t, W_A), f32),
                   jax.ShapeDtypeStruct((b, POOL_HIST, W_A), f32)),
        grid=(b // ns, nt),
        in_specs=[
            pl.BlockSpec((ns, tl, W_A), lambda i, j: (i, j, P_UA // W_A)),
            pl.BlockSpec((None, ns, POOL_HIST, W_A), lambda i, j: (lh, i, 0, 0)),
            pl.BlockSpec((None, W_A, W_A), lambda i, j: (l, 0, 0)),
            pl.BlockSpec((None, 1, W_A), lambda i, j: (l, 0, 0)),
        ],
        out_specs=(pl.BlockSpec((ns, tl, W_A), lambda i, j: (i, j, 0)),
                   pl.BlockSpec((ns, POOL_HIST, W_A), lambda i, j: (i, 0, 0))),
        scratch_shapes=[pltpu.VMEM((ns, tl + 16, W_A), f32)],
        compiler_params=_params(2),
        name=f"pool_t{t}",
    )(p.reshape(b, t, P_COLS), hist, wbd, scale)
    return y.reshape(b * t, W_A), st


def _softmax_av(s_parts, av_fns):
    m = functools.reduce(jnp.maximum, [jnp.max(s, axis=-1, keepdims=True) for s in s_parts])
    e_parts = [jnp.exp(s - m) for s in s_parts]
    l = functools.reduce(jnp.add, [jnp.sum(e, axis=-1, keepdims=True) for e in e_parts])
    o = functools.reduce(jnp.add, [av(e) for e, av in zip(e_parts, av_fns)])
    return o / l


BIAS_PERIOD = 2 * K_WINDOW


def _band_attn_kernel(q_ref, k_ref, v_ref, wrow_ref, y_ref, kc_ref, vc_ref, bias_ref, *, t, keep):
    qb = pl.program_id(1)

    @pl.when((pl.program_id(0) == 0) & (qb == 0))
    def _():
        qi = _iota((Q_BLOCK, K_WINDOW), 0)
        kj = _iota((Q_BLOCK, K_WINDOW), 1)
        for h in range(H_B):
            base = pltpu.roll(jnp.broadcast_to(wrow_ref[h], (Q_BLOCK, BIAS_PERIOD)), 0, 1,
                              stride=1, stride_axis=0)
            for v in range(3):
                dchunk = (qi + v * Q_BLOCK) // CHUNK - kj // CHUNK
                vis = (dchunk >= 0) & (dchunk <= BAND_CHUNKS)
                lo = (2 - v) * Q_BLOCK
                bias_ref[v, h] = jnp.where(vis, base[:, lo:lo + K_WINDOW], NEG_INF)

    start = pl.multiple_of(jnp.maximum(qb * Q_BLOCK - BAND_CHUNKS * CHUNK, 0), Q_BLOCK)
    variant = jnp.minimum(qb, 2)
    outs = []
    for h in range(H_B):
        lanes = slice(h * DH_B, (h + 1) * DH_B)
        q = q_ref[:, lanes] * (DH_B ** -0.5)
        k = k_ref[pl.ds(start, K_WINDOW), lanes]
        v = v_ref[pl.ds(start, K_WINDOW), lanes]
        s = _dot_nt(q, k) + bias_ref[variant, h]
        outs.append(_softmax_av([s], [lambda e, v=v: _dot(e, v)]))
    y_ref[...] = jnp.concatenate(outs, axis=1)

    @pl.when(qb == pl.num_programs(1) - 1)
    def _():
        for h in range(H_B):
            lanes = slice(h * DH_B, (h + 1) * DH_B)
            kc_ref[h] = k_ref[t - keep:t, lanes]
            vc_ref[h] = v_ref[t - keep:t, lanes]


def _rel_bias_rows(table, nq, nk, off):
    period = nq + nk
    m = np.arange(period)
    rel = np.where(m < nk, m, m - period) - off
    row = table.astype(f32)[..., np.clip(rel, -REL_CLIP, REL_CLIP) + REL_CLIP]
    lead = table.shape[:-1]
    flat = jnp.tile(row, (1,) * len(lead) + (nq,))[..., :nq * (period - 1)]
    return flat.reshape(lead + (nq, period - 1))[..., :nk]


def _band_bias_row(table):
    m = np.arange(BIAS_PERIOD)
    e = np.where(m < 2 * Q_BLOCK + K_WINDOW, m, m - BIAS_PERIOD)
    idx = np.clip(e - 2 * Q_BLOCK, -REL_CLIP, REL_CLIP) + REL_CLIP
    return table.astype(f32)[..., idx][:, :, None, :]


def _band_attn_call(p, b, t, wrow, l):
    nq = t // Q_BLOCK
    keep = min(ATTN_WINDOW, t)
    cache = jax.ShapeDtypeStruct((b, H_B, keep, DH_B), f32)
    cache_spec = pl.BlockSpec((None, H_B, keep, DH_B), lambda i, j: (i, 0, 0, 0))
    return pl.pallas_call(
        functools.partial(_band_attn_kernel, t=t, keep=keep),
        out_shape=(jax.ShapeDtypeStruct((b * t, W_B), f32), cache, cache),
        grid=(b, nq),
        in_specs=[
            pl.BlockSpec((Q_BLOCK, W_B), lambda i, j: (i * nq + j, P_QB // W_B)),
            pl.BlockSpec((t, W_B), lambda i, j: (i, P_KB // W_B)),
            pl.BlockSpec((t, W_B), lambda i, j: (i, P_VB // W_B)),
            pl.BlockSpec((None, H_B, 1, BIAS_PERIOD), lambda i, j: (l, 0, 0, 0)),
        ],
        out_specs=(pl.BlockSpec((Q_BLOCK, W_B), lambda i, j: (i * nq + j, 0)), cache_spec, cache_spec),
        scratch_shapes=[pltpu.VMEM((3, H_B, Q_BLOCK, K_WINDOW), f32)],
        compiler_params=_params(2),
        name="band_attn",
    )(p, p, p, wrow)


def _sample_attn_kernel(q_ref, k_ref, v_ref, kct_ref, vct_ref, bias_ref, y_ref, kn_ref, vn_ref,
                        *, ns, n_cache):
    def one_sequence(s, carry):
        outs = []
        for h in range(H_B):
            lanes = slice(h * DH_B, (h + 1) * DH_B)
            q = q_ref[s, :, lanes] * (DH_B ** -0.5)
            k_new, v_new = k_ref[s, :, lanes], v_ref[s, :, lanes]
            s_old = _dot(q, kct_ref[s, h]) + bias_ref[h, :, 0:n_cache]
            s_new = _dot_nt(q, k_new) + bias_ref[h, :, n_cache:]
            outs.append(_softmax_av(
                [s_old, s_new],
                [lambda e, h=h: _dot_nt(e, vct_ref[s, h]), lambda e, v=v_new: _dot(e, v)]))
            kn_ref[s, h] = k_new
            vn_ref[s, h] = v_new
        y_ref[s] = jnp.concatenate(outs, axis=1)
        return carry

    lax.fori_loop(0, ns, one_sequence, 0, unroll=4)


def _sample_attn_call(p, b, t, k_cache_t, v_cache_t, bias, l):
    n_cache = k_cache_t.shape[4]
    ns = min(b, SAMPLE_SEQS)
    cache_spec = pl.BlockSpec((None, ns, H_B, DH_B, n_cache), lambda i: (l, i, 0, 0, 0))
    new = jax.ShapeDtypeStruct((b, H_B, t, DH_B), f32)
    new_spec = pl.BlockSpec((ns, H_B, t, DH_B), lambda i: (i, 0, 0, 0))
    p3 = p.reshape(b, t, P_COLS)
    y, k_new, v_new = pl.pallas_call(
        functools.partial(_sample_attn_kernel, ns=ns, n_cache=n_cache),
        out_shape=(jax.ShapeDtypeStruct((b, t, W_B), f32), new, new),
        grid=(b // ns,),
        in_specs=[
            pl.BlockSpec((ns, t, W_B), lambda i: (i, 0, P_QB // W_B)),
            pl.BlockSpec((ns, t, W_B), lambda i: (i, 0, P_KB // W_B)),
            pl.BlockSpec((ns, t, W_B), lambda i: (i, 0, P_VB // W_B)),
            cache_spec, cache_spec,
            pl.BlockSpec((None, H_B, t, n_cache + t), lambda i: (l, 0, 0, 0)),
        ],
        out_specs=(pl.BlockSpec((ns, t, W_B), lambda i: (i, 0, 0)), new_spec, new_spec),
        compiler_params=_params(1),
        name="sample_attn",
    )(p3, p3, p3, k_cache_t, v_cache_t, bias)
    return y.reshape(b * t, W_B), k_new, v_new


def _delta_kernel(q_ref, k_ref, v_ref, z_ref, sm_ref, hist_ref, s0_ref, cw_ref, alog_ref, dtb_ref, ng_ref,
                  y_ref, sout_ref, cout_ref, ext_ref, s_ref, *, ns, tl, c, carry):
    t_idx = pl.program_id(1)
    tb = ns * tl
    nch = tb // c
    cw = H_C * c

    def chunks(x):
        return x.reshape(nch, c, x.shape[-1])

    def pack_cols(x3):
        if c == DK_C:
            return x3
        return jnp.concatenate([x3[:, :, h * DK_C:h * DK_C + c] for h in range(H_C)], axis=-1)

    def block_t(x3, width):
        return jnp.concatenate(
            [jnp.swapaxes(x3[:, :, h * width:(h + 1) * width], 1, 2) for h in range(H_C)], axis=-1)

    def head_mask(rows, row_w, cols, col_w):
        return (_iota((rows, cols), 0) // row_w == _iota((rows, cols), 1) // col_w)[None]

    def block_diag(y3, mask):
        return jnp.where(mask, jnp.concatenate([y3] * H_C, axis=1), 0.0)

    if carry:
        @pl.when(t_idx == 0)
        def _():
            s_ref[...] = jnp.concatenate([s0_ref[:, h] for h in range(H_C)], axis=-1)
            for s in range(3):
                ext_ref[s, :, 0:8, :] = jnp.zeros((ns, 8, 256), f32)
                ext_ref[s, :, 5:8, :] = hist_ref[:, :, s * 256:(s + 1) * 256]
    else:
        for s in range(3):
            ext_ref[s, :, 5:8, :] = hist_ref[:, :, s * 256:(s + 1) * 256]

    parts = []
    for s, ref in enumerate((q_ref, k_ref, v_ref)):
        ext_ref[s, :, 8:8 + tl, :] = ref[...]
        acc = ext_ref[s, :, 5:5 + tl, :] * cw_ref[0:1, s * 256:(s + 1) * 256]
        for j in range(1, CONV_W):
            acc = acc + ext_ref[s, :, 5 + j:5 + j + tl, :] * cw_ref[j:j + 1, s * 256:(s + 1) * 256]
        parts.append(_silu(acc).reshape(tb, 256))
        last = ext_ref[s, :, tl + 5:tl + 8, :]
        if carry:
            ext_ref[s, :, 5:8, :] = last
        cout_ref[:, :, s * 256:(s + 1) * 256] = last
    cq, ck, cv = parts
    seg = _seg_ones(256, DK_C, 256, DK_C)
    qn = cq * lax.rsqrt(_dot_sel(cq * cq, seg) + EPS) * (DK_C ** -0.5)
    kn = ck * lax.rsqrt(_dot_sel(ck * ck, seg) + EPS)

    sm = sm_ref[...].reshape(tb, SMALL_W)
    g = -jnp.exp(alog_ref[...]) * jax.nn.softplus(sm + dtb_ref[...])
    beta = jax.nn.sigmoid(sm)
    ltri = jnp.broadcast_to((_iota((c, c), 0) >= _iota((c, c), 1)).astype(bf16)[None], (nch, c, c))
    gcum = _bmm_sel(ltri, g.reshape(nch, c, SMALL_W)).reshape(tb, SMALL_W)
    lane_head = _iota((SMALL_W, 256), 1) // 64
    gb = _dot_sel(gcum, (_iota((SMALL_W, 256), 0) == lane_head).astype(bf16))
    betab = _dot_sel(beta, (_iota((SMALL_W, 256), 0) == lane_head + H_C).astype(bf16))
    eg = jnp.exp(gb)
    gl = jnp.broadcast_to(gb.reshape(nch, c, 256)[:, c - 1:c, :], (nch, c, 256)).reshape(tb, 256)

    m_k2c = head_mask(W_C, DK_C, cw, c)
    m_c2c = head_mask(cw, c, cw, c)
    m_c2v = head_mask(cw, c, W_C, DV_C)
    m_k2v = head_mask(W_C, DK_C, W_C, DV_C)
    kn3, qn3 = chunks(kn), chunks(qn)
    bd_kt = jnp.where(m_k2c, jnp.concatenate([jnp.swapaxes(kn3, 1, 2)] * H_C, axis=-1), 0.0)
    gcb = pack_cols(chunks(gb))
    grb = block_t(gcb, c)
    rp, cp = _iota((c, cw), 0), _iota((c, cw), 1) % c
    incl_p = (rp >= cp)[None]
    dec = jnp.where(incl_p, jnp.exp(jnp.where(incl_p, gcb - grb, 0.0)), 0.0)
    a = jnp.where((rp > cp)[None], pack_cols(chunks(betab)) * _bmm(kn3, bd_kt) * dec, 0.0)
    d = jnp.where(rp == cp, 1.0, 0.0)[None] - jnp.where(((rp // 2 == cp // 2) & (rp > cp))[None], a, 0.0)
    s = 2
    while s < c:
        lower_left = (rp // (2 * s) == cp // (2 * s)) & (rp % (2 * s) >= s) & (cp % (2 * s) < s)
        e = jnp.where(lower_left[None], a, 0.0)
        d = d - _bmm(d, block_diag(_bmm(e, block_diag(d, m_c2c)), m_c2c))
        s *= 2
    u = _bmm(d, block_diag(chunks(betab * cv), m_c2v))
    w = _bmm(d, block_diag(chunks(betab * eg * kn), m_c2v))
    qk = _bmm(qn3, bd_kt) * dec
    qt3 = chunks(qn * eg)
    ktt = block_t(chunks(kn * jnp.exp(gl - gb)), DK_C)
    egl = chunks(jnp.exp(gl))[:, 0:1, :]

    if carry:
        o_chunks = [None] * nch
        for s in range(ns):
            st = s_ref[s]
            for ic in range(s * (tl // c), (s + 1) * (tl // c)):
                bds = jnp.where(m_k2v[0], jnp.concatenate([st] * H_C, axis=0), 0.0)
                un = u[ic] - _dot(w[ic], bds)
                bdun = jnp.where(m_c2v[0], jnp.concatenate([un] * H_C, axis=0), 0.0)
                o_chunks[ic] = _dot(qt3[ic], bds) + _dot(qk[ic], bdun)
                st = egl[ic] * st + _dot(ktt[ic], bdun)
            s_ref[s] = st
        o = jnp.concatenate(o_chunks, axis=0)

        @pl.when(t_idx == pl.num_programs(1) - 1)
        def _():
            for h in range(H_C):
                sout_ref[:, h] = s_ref[:, :, h * DV_C:(h + 1) * DV_C]
    else:
        st = jnp.concatenate([s0_ref[:, h] for h in range(H_C)], axis=-1)
        bds = block_diag(st, m_k2v)
        un = u - _bmm(w, bds)
        bdun = block_diag(un, m_c2v)
        o = (_bmm(qt3, bds) + _bmm(qk, bdun)).reshape(tb, W_C)
        st = egl * st + _bmm(ktt, bdun)
        for h in range(H_C):
            sout_ref[:, h] = st[:, :, h * DV_C:(h + 1) * DV_C]

    o = o * lax.rsqrt(_dot_sel(o * o, seg) * (1.0 / DV_C) + EPS) * ng_ref[...]
    y_ref[...] = (o * _silu(z_ref[...].reshape(tb, W_C))).reshape(ns, tl, W_C)


def _delta_call(p, b, t, hist, s0, lh, conv_w, alog, dtb, ng, l):
    c = min(CHUNK, t)
    carry = t > c
    ns = min(b, DELTA_SEQS) if carry else min(b, DELTA_ROWS // (4 * t))
    tl = min(t, DELTA_ROWS // ns) if carry else t
    nt = t // tl
    p3 = p.reshape(b, t, P_COLS)

    def col(cb):
        return lambda i, j: (i, j, cb)

    layer = lambda i, j: (l, 0, 0)
    state_spec = pl.BlockSpec((ns, H_C, DK_C, DV_C), lambda i, j: (i, 0, 0, 0))
    y, s_new, conv_new = pl.pallas_call(
        functools.partial(_delta_kernel, ns=ns, tl=tl, c=c, carry=carry),
        out_shape=(jax.ShapeDtypeStruct((b, t, W_C), f32),
                   jax.ShapeDtypeStruct((b, H_C, DK_C, DV_C), f32),
                   jax.ShapeDtypeStruct((b, CONV_W - 1, QKV_C), f32)),
        grid=(b // ns, nt),
        in_specs=[
            pl.BlockSpec((ns, tl, 256), col(P_QKVC // 256)),
            pl.BlockSpec((ns, tl, 256), col(P_QKVC // 256 + 1)),
            pl.BlockSpec((ns, tl, 256), col(P_QKVC // 256 + 2)),
            pl.BlockSpec((ns, tl, 256), col(P_ZC // 256)),
            pl.BlockSpec((ns, tl, SMALL_W), col(P_SMALL // SMALL_W)),
            pl.BlockSpec((None, ns, CONV_W - 1, QKV_C), lambda i, j: (lh, i, 0, 0)),
            pl.BlockSpec((None, ns, H_C, DK_C, DV_C), lambda i, j: (lh, i, 0, 0, 0)),
            pl.BlockSpec((None, CONV_W, QKV_C), layer),
            pl.BlockSpec((None, 1, SMALL_W), layer),
            pl.BlockSpec((None, 1, SMALL_W), layer),
            pl.BlockSpec((None, 1, W_C), layer),
        ],
        out_specs=(pl.BlockSpec((ns, tl, W_C), lambda i, j: (i, j, 0)),
                   state_spec,
                   pl.BlockSpec((ns, CONV_W - 1, QKV_C), lambda i, j: (i, 0, 0))),
        scratch_shapes=[
            pltpu.VMEM((3, ns, tl + 8, 256), f32),
            pltpu.VMEM((ns, DK_C, W_C), f32),
        ],
        compiler_params=_params(2),
        name=f"delta_t{t}",
    )(p3, p3, p3, p3, p3, hist, s0, conv_w, alog, dtb, ng)
    return y.reshape(b * t, W_C), s_new, conv_new


def _gla_kernel(q_ref, k_ref, v_ref, gate_ref, sm_ref, s0_ref, wgk_ref, bgk_ref, ng_ref,
                y_ref, sout_ref, qs_ref, g_ref, o_ref, s_ref, *, ns, tl, c, carry):
    t_idx = pl.program_id(1)
    tb = ns * tl
    blk = _iota((WK_D, W_D), 0) // DK_D == _iota((WK_D, W_D), 1) // DV_D
    seg_kv = blk.astype(bf16)

    def block_diag(s4):
        zero = jnp.zeros((DK_D, DV_D), f32)
        return jnp.concatenate(
            [jnp.concatenate([s4[h] if h2 == h else zero for h2 in range(H_D)], axis=1)
             for h in range(H_D)], axis=0)

    if carry:
        @pl.when(t_idx == 0)
        def _():
            s_ref[...] = block_diag(s0_ref[0])

    gk = jax.nn.log_sigmoid(_dot(sm_ref[...].reshape(tb, SMALL_W), wgk_ref[...]) + bgk_ref[...])
    g_ref[...] = gk * (1.0 / GLA_GATE_NORM)
    qs_ref[...] = q_ref[...].reshape(tb, WK_D) * (DK_D ** -0.5)
    ltri = (_iota((c, c), 0) >= _iota((c, c), 1)).astype(bf16)
    row8 = _iota((8, WK_D), 0)
    rowid = _iota((c, WK_D), 0)
    lane_head_k = _iota((GLA_SUB, WK_D), 1) // DK_D
    lane_head_v = _iota((GLA_SUB, W_D), 1) // DV_D

    def chunk(ic, st):
        rows = pl.ds(pl.multiple_of(ic * c, c), c)
        g_hi, g_lo = _split(g_ref[rows, :])
        g = _dot(ltri, g_hi) + _dot(ltri, g_lo)
        q = qs_ref[rows, :]
        if carry:
            k, v = k_ref[0, rows, :], v_ref[0, rows, :]
        else:
            k, v, st = k_ref[ic], v_ref[ic], block_diag(s0_ref[ic])
        o_state = _dot(q * jnp.exp(g), st)
        o8 = [None] * (c // 8)
        for jb in range(c // 8):
            lo = 8 * jb
            hi = GLA_SUB * (lo // GLA_SUB + 1)
            m = hi - lo
            e = []
            for j in range(lo, lo + 8):
                ej = jnp.exp(g[lo:hi] - g[j:j + 1]) * (q[lo:hi] * k[j:j + 1])
                first = jnp.where(row8 >= j - lo, ej[:8], 0.0)
                e.append(first if m == 8 else jnp.concatenate([first, ej[8:]], axis=0))
            att = _dot(jnp.concatenate(e, axis=0).astype(bf16), seg_kv)
            upd = att[0:m] * v[lo:lo + 1]
            for jj in range(1, 8):
                upd = upd + att[jj * m:(jj + 1) * m] * v[lo + jj:lo + jj + 1]
            for r in range(m // 8):
                piece = upd[8 * r:8 * (r + 1)]
                o8[jb + r] = piece if o8[jb + r] is None else o8[jb + r] + piece
        for sb in range(1, c // GLA_SUB):
            r0 = sb * GLA_SUB
            ref = g[r0 - 1:r0]
            qx = q[r0:r0 + GLA_SUB] * jnp.exp(g[r0:r0 + GLA_SUB] - ref)
            kx = jnp.where(rowid < r0, k * jnp.exp(jnp.minimum(ref - g, 0.0)), 0.0)
            qh = jnp.concatenate([jnp.where(lane_head_k == h, qx, 0.0) for h in range(H_D)], axis=0)
            res = _dot(_dot_nt(qh, kx), v)
            upd = None
            for h in range(H_D):
                part = jnp.where(lane_head_v == h, res[h * GLA_SUB:(h + 1) * GLA_SUB], 0.0)
                upd = part if upd is None else upd + part
            for r in range(GLA_SUB // 8):
                o8[r0 // 8 + r] = o8[r0 // 8 + r] + upd[8 * r:8 * (r + 1)]
        o_ref[rows, :] = o_state + jnp.concatenate(o8, axis=0)
        gl = g[c - 1:c]
        egl_col = jnp.exp(g.T[:, c - 1:c])
        s_new = jnp.where(blk, egl_col * st + _dot_tn(k * jnp.exp(gl - g), v), 0.0)
        if not carry:
            for h in range(H_D):
                sout_ref[ic, h] = s_new[h * DK_D:(h + 1) * DK_D, h * DV_D:(h + 1) * DV_D]
        return s_new

    s_first = s_ref[...] if carry else jnp.zeros((WK_D, W_D), f32)
    s_last = lax.fori_loop(0, tb // c, chunk, s_first, unroll=min(GLA_UNROLL, tb // c))

    o = o_ref[...]
    o = o * lax.rsqrt(_dot_sel(o * o, _seg_ones(W_D, DV_D, W_D, DV_D)) * (1.0 / DV_D) + EPS) * ng_ref[...]
    y_ref[...] = (o * _silu(gate_ref[...].reshape(tb, W_D))).reshape(ns, tl, W_D)

    if carry:
        s_ref[...] = s_last

        @pl.when(t_idx == pl.num_programs(1) - 1)
        def _():
            for h in range(H_D):
                sout_ref[0, h] = s_ref[h * DK_D:(h + 1) * DK_D, h * DV_D:(h + 1) * DV_D]


def _gla_call(p, b, t, s0, lh, wgk, bgk, ng, l):
    c = min(CHUNK, t)
    carry = t > c
    ns, tl = (1, min(t, ROW_TILE)) if carry else (min(b, SAMPLE_SEQS), t)
    tb = ns * tl
    nt = t // tl
    p3 = p.reshape(b, t, P_COLS)

    def col(cb):
        return lambda i, j: (i, j, cb)

    layer = lambda i, j: (l, 0, 0)
    y, s_new = pl.pallas_call(
        functools.partial(_gla_kernel, ns=ns, tl=tl, c=c, carry=carry),
        out_shape=(jax.ShapeDtypeStruct((b, t, W_D), f32),
                   jax.ShapeDtypeStruct((b, H_D, DK_D, DV_D), f32)),
        grid=(b // ns, nt),
        in_specs=[
            pl.BlockSpec((ns, tl, WK_D), col(P_QD // WK_D)),
            pl.BlockSpec((ns, tl, WK_D), col(P_KD // WK_D)),
            pl.BlockSpec((ns, tl, W_D), col(P_VD // W_D)),
            pl.BlockSpec((ns, tl, W_D), col(P_GD // W_D)),
            pl.BlockSpec((ns, tl, SMALL_W), col(P_SMALL // SMALL_W)),
            pl.BlockSpec((None, ns, H_D, DK_D, DV_D), lambda i, j: (lh, i, 0, 0, 0)),
            pl.BlockSpec((None, SMALL_W, WK_D), layer),
            pl.BlockSpec((None, 1, WK_D), layer),
            pl.BlockSpec((None, 1, W_D), layer),
        ],
        out_specs=(pl.BlockSpec((ns, tl, W_D), lambda i, j: (i, j, 0)),
                   pl.BlockSpec((ns, H_D, DK_D, DV_D), lambda i, j: (i, 0, 0, 0))),
        scratch_shapes=[
            pltpu.VMEM((tb, WK_D), f32), pltpu.VMEM((tb, WK_D), f32),
            pltpu.VMEM((tb, W_D), f32),
            pltpu.VMEM((WK_D, W_D), f32),
        ],
        compiler_params=_params(2),
        name=f"gla_t{t}",
    )(p3, p3, p3, p3, p3, s0, wgk, bgk, ng)
    return y.reshape(b * t, W_D), s_new


def _reorder_w_in(w_in):
    pad = jnp.zeros(w_in.shape[:-1] + (P_COLS - P_SMALL - 2 * H_C - GLA_RANK,), w_in.dtype)
    return jnp.concatenate([
        w_in[..., 0:2048],
        w_in[..., 2312:2568],
        w_in[..., 2568:2824],
        w_in[..., 2056:2184],
        w_in[..., 2184:2312],
        w_in[..., 2048:2056],
        w_in[..., 2824:2840],
        pad], axis=-1)


def _lane_pad(x, width):
    return jnp.pad(x, ((0, 0), (0, width - x.shape[-1])))[:, None, :]


def kernel(x_prompt, x_sample, cache_pool, cache_attn_k, cache_attn_v, state_conv, state_delta,
           state_gla, attn_norm_g, w_in, pool_w, pool_scale, rel_bias, conv_w, a_log, dt_bias,
           delta_norm_g, gla_w_gk, gla_b_gk, gla_norm_g, w_out, mlp_norm_g, w_up, w_down,
           final_norm_g):
    bp, tp, _ = x_prompt.shape
    bs, ts, _ = x_sample.shape
    w_in_r = _reorder_w_in(w_in.astype(bf16))
    w_out_b, w_up_b, w_down_b = w_out.astype(bf16), w_up.astype(bf16), w_down.astype(bf16)
    g1 = attn_norm_g[:, None, :]
    g2 = mlp_norm_g[:, None, :]
    gf = final_norm_g.reshape(1, D_MODEL)
    pool_wbd = jnp.zeros((DEPTH, W_A, W_A), f32)
    for gi in range(N_POOL):
        sl = slice(gi * POOL_GW, (gi + 1) * POOL_GW)
        pool_wbd = pool_wbd.at[:, sl, sl].set(pool_w[:, gi])
    pool_sc = pool_scale[:, None, :]
    band_bias = _band_bias_row(rel_bias)
    n_cache = cache_attn_k.shape[3]
    sample_bias = _rel_bias_rows(rel_bias, ts, n_cache + ts, n_cache)
    cache_kt, cache_vt = jnp.swapaxes(cache_attn_k, 3, 4), jnp.swapaxes(cache_attn_v, 3, 4)
    alog = _lane_pad(a_log, SMALL_W)
    dtb = _lane_pad(dt_bias, SMALL_W)
    ng_c = jnp.tile(delta_norm_g, (1, H_C))[:, None, :]
    wgk = jnp.zeros((DEPTH, SMALL_W, WK_D), f32).at[:, 2 * H_C:2 * H_C + GLA_RANK].set(gla_w_gk)
    bgk = gla_b_gk[:, None, :]
    ng_d = jnp.tile(gla_norm_g, (1, H_D))[:, None, :]
    zeros = (jnp.zeros((1, bp, POOL_HIST, W_A), f32), jnp.zeros((1, bp, CONV_W - 1, QKV_C), f32),
             jnp.zeros((1, bp, H_C, DK_C, DV_C), f32), jnp.zeros((1, bp, H_D, DK_D, DV_D), f32))
    carried = (cache_pool, state_conv, state_delta, state_gla)

    xs = [x_prompt.reshape(bp * tp, D_MODEL), x_sample.reshape(bs * ts, D_MODEL)]
    states = ([], [])
    for l in range(DEPTH):
        for s, (b, t, pos0) in enumerate(((bp, tp, 0), (bs, ts, tp))):
            pool_h, conv_h, sd_h, sg_h = zeros if s == 0 else carried
            lh = 0 if s == 0 else l
            p = _inproj(xs[s], g1, w_in_r, l)
            y_a, st_pool = _pool_call(p, b, t, pos0, pool_h, lh, pool_wbd, pool_sc, l)
            if s == 0:
                y_b, st_k, st_v = _band_attn_call(p, b, t, band_bias, l)
            else:
                y_b, st_k, st_v = _sample_attn_call(p, b, t, cache_kt, cache_vt, sample_bias, l)
            y_c, st_d, st_conv = _delta_call(p, b, t, conv_h, sd_h, lh, conv_w, alog, dtb, ng_c, l)
            y_d, st_g = _gla_call(p, b, t, sg_h, lh, wgk, bgk, ng_d, l)
            xs[s] = _outmlp(xs[s], (y_a, y_b, y_c, y_d), w_out_b, g2, w_up_b, w_down_b, gf, l,
                            final=(l == DEPTH - 1))
            states[s].append((st_pool, st_k, st_v, st_conv, st_d, st_g))
    y_prompt = xs[0].reshape(bp, tp, D_MODEL)
    y_sample = xs[1].reshape(bs, ts, D_MODEL)
    outs_p = tuple(jnp.stack([st[i] for st in states[0]]) for i in range(6))
    outs_s = tuple(jnp.stack([st[i] for st in states[1]]) for i in range(6))
    return (y_prompt, y_sample) + outs_p + outs_s
```

```python
import functools

import jax
import jax.numpy as jnp
import numpy as np
from jax import lax
from jax.experimental import pallas as pl
from jax.experimental.pallas import tpu as pltpu

D_MODEL = 1024
DEPTH = 4
CHUNK = 64
W_A = W_B = W_C = W_D = 256
POOL_WINDOWS = (2, 4, 8, 16)
N_POOL = 4
POOL_GW = 64
POOL_HIST = 15
H_B = 4
DH_B = 64
BAND_CHUNKS = 8
ATTN_WINDOW = 512
REL_CLIP = 256
H_C = 4
DK_C = 64
DV_C = 64
CONV_W = 4
QKV_C = 768
H_D = 4
DK_D = 32
DV_D = 64
WK_D = 128
GLA_RANK = 16
GLA_GATE_NORM = 16.0
D_FF = 4096
EPS = 1e-6
NEG_INF = -1e30

P_UA = 0
P_QB, P_KB, P_VB = 256, 512, 768
P_QKVC = 1024
P_ZC = 1792
P_VD = 2048
P_GD = 2304
P_QD = 2560
P_KD = 2688
P_SMALL = 2816
SMALL_W = 128
P_COLS = 2944

ROW_TILE = 512
FF_CHUNK = 1024
POOL_TILE = 2048
SAMPLE_SEQS = 8
GLA_SUB = 16
GLA_UNROLL = 8
DELTA_ROWS = 1024
DELTA_SEQS = 8
Q_BLOCK = 256
K_WINDOW = Q_BLOCK + BAND_CHUNKS * CHUNK
VMEM_LIMIT = 56 * 1024 * 1024

f32 = jnp.float32
bf16 = jnp.bfloat16


def _params(n_axes):
    return pltpu.CompilerParams(dimension_semantics=("arbitrary",) * n_axes, vmem_limit_bytes=VMEM_LIMIT)


def _dot(a, b):
    return jnp.dot(a, b, preferred_element_type=f32)


def _dot_nt(a, b):
    return lax.dot_general(a, b, (((1,), (1,)), ((), ())), preferred_element_type=f32)


def _dot_tn(a, b):
    return lax.dot_general(a, b, (((0,), (0,)), ((), ())), preferred_element_type=f32)


def _bmm(a, b):
    return jnp.einsum('nij,njk->nik', a, b, preferred_element_type=f32)


def _split(x):
    hi = x.astype(bf16)
    return hi, (x - hi.astype(f32)).astype(bf16)


def _dot_sel(x, sel):
    hi, lo = _split(x)
    return _dot(hi, sel) + _dot(lo, sel)


def _seg_sum(x, sel):
    return _dot(x.astype(bf16), sel)


def _bmm_sel(sel, x):
    hi, lo = _split(x)
    return _bmm(sel, hi) + _bmm(sel, lo)


def _iota(shape, dim):
    return lax.broadcasted_iota(jnp.int32, shape, dim)


def _seg_ones(n_in, seg_in, n_out, seg_out):
    return (_iota((n_in, n_out), 0) // seg_in == _iota((n_in, n_out), 1) // seg_out).astype(bf16)


def _silu(x):
    return x * jax.nn.sigmoid(x)


def _rms(x):
    return x * lax.rsqrt(jnp.mean(x * x, axis=-1, keepdims=True) + EPS)


def _inproj_kernel(x_ref, g_ref, w_ref, o_ref):
    h = _rms(x_ref[...]) * g_ref[...]
    o_ref[...] = _dot(h.astype(bf16), w_ref[...])


def _inproj(x, g, w, l):
    n = x.shape[0]
    return pl.pallas_call(
        _inproj_kernel,
        out_shape=jax.ShapeDtypeStruct((n, P_COLS), f32),
        grid=(n // ROW_TILE,),
        in_specs=[
            pl.BlockSpec((ROW_TILE, D_MODEL), lambda i: (i, 0)),
            pl.BlockSpec((None, 1, D_MODEL), lambda i: (l, 0, 0)),
            pl.BlockSpec((None, D_MODEL, P_COLS), lambda i: (l, 0, 0), pipeline_mode=pl.Buffered(1)),
        ],
        out_specs=pl.BlockSpec((ROW_TILE, P_COLS), lambda i: (i, 0)),
        compiler_params=_params(1),
        name="inproj",
    )(x, g, w)


def _outmlp_kernel(x_ref, ya_ref, yb_ref, yc_ref, yd_ref, wo_ref, g2_ref, wu_ref, wd_ref, gf_ref,
                   o_ref, *, final):
    acc = None
    for k, y_ref in enumerate((ya_ref, yb_ref, yc_ref, yd_ref)):
        d = _dot(y_ref[...].astype(bf16), wo_ref[k * 256:(k + 1) * 256, :])
        acc = d if acc is None else acc + d
    x = x_ref[...] + acc
    h2 = (_rms(x) * g2_ref[...]).astype(bf16)
    acc = None
    for c in range(D_FF // FF_CHUNK):
        up = _dot(h2, wu_ref[:, c * FF_CHUNK:(c + 1) * FF_CHUNK])
        up = jnp.square(jnp.maximum(up, 0.0)).astype(bf16)
        d = _dot(up, wd_ref[c * FF_CHUNK:(c + 1) * FF_CHUNK, :])
        acc = d if acc is None else acc + d
    x = x + acc
    if final:
        x = _rms(x) * gf_ref[...]
    o_ref[...] = x


def _outmlp(x, ys, wo, g2, wu, wd, gf, l, final):
    n = x.shape[0]
    row = lambda i: (i, 0)
    layer = lambda i: (l, 0, 0)
    single = pl.Buffered(1)
    return pl.pallas_call(
        functools.partial(_outmlp_kernel, final=final),
        out_shape=jax.ShapeDtypeStruct((n, D_MODEL), f32),
        grid=(n // ROW_TILE,),
        in_specs=[pl.BlockSpec((ROW_TILE, D_MODEL), row)]
        + [pl.BlockSpec((ROW_TILE, 256), row)] * 4
        + [pl.BlockSpec((None, D_MODEL, D_MODEL), layer, pipeline_mode=single),
           pl.BlockSpec((None, 1, D_MODEL), layer),
           pl.BlockSpec((None, D_MODEL, D_FF), layer, pipeline_mode=single),
           pl.BlockSpec((None, D_FF, D_MODEL), layer, pipeline_mode=single),
           pl.BlockSpec((1, D_MODEL), lambda i: (0, 0))],
        out_specs=pl.BlockSpec((ROW_TILE, D_MODEL), row),
        compiler_params=_params(1),
        name="outmlp_final" if final else "outmlp",
    )(x, *ys, wo, g2, wu, wd, gf)


def _pool_kernel(u_ref, hist_ref, w_ref, scale_ref, y_ref, st_ref, ext_ref, *, ns, tl, pos0):
    t_idx = pl.program_id(1)

    @pl.when(t_idx == 0)
    def _():
        ext_ref[:, 0:8, :] = jnp.zeros((ns, 8, W_A), f32)
        ext_ref[:, 1:16, :] = hist_ref[...]

    x = u_ref[...]
    ext_ref[:, 16:16 + tl, :] = x
    wsum = {}
    acc = x
    for k in range(1, 16):
        acc = acc + ext_ref[:, 16 - k:16 - k + tl, :]
        if k + 1 in POOL_WINDOWS:
            wsum[k + 1] = acc
    ext_ref[:, 1:16, :] = ext_ref[:, tl + 1:tl + 16, :]
    lane_group = _iota((ns, tl, W_A), 2) // POOL_GW
    pos = pos0 + t_idx * tl + _iota((ns, tl, W_A), 1)
    ws, win = wsum[POOL_WINDOWS[-1]], jnp.full((ns, tl, W_A), POOL_WINDOWS[-1], jnp.int32)
    for gi in range(N_POOL - 2, -1, -1):
        ws = jnp.where(lane_group == gi, wsum[POOL_WINDOWS[gi]], ws)
        win = jnp.where(lane_group == gi, POOL_WINDOWS[gi], win)
    cnt = jnp.minimum(win, pos + 1).astype(f32)
    pooled = (ws / cnt - x).reshape(ns * tl, W_A)
    y_ref[...] = (_dot(pooled, w_ref[...]) * scale_ref[...]).reshape(ns, tl, W_A)

    @pl.when(t_idx == pl.num_programs(1) - 1)
    def _():
        st_ref[...] = ext_ref[:, 1:16, :]


def _pool_call(p, b, t, pos0, hist, lh, wbd, scale, l):
    tl = min(t, POOL_TILE)
    ns = min(b, POOL_TILE // tl)
    nt = t // tl
    y, st = pl.pallas_call(
        functools.partial(_pool_kernel, ns=ns, tl=tl, pos0=pos0),
        out_shape=(jax.ShapeDtypeStruct((b, t, W_A), f32),
                   jax.ShapeDtypeStruct((b, POOL_HIST, W_A), f32)),
        grid=(b // ns, nt),
        in_specs=[
            pl.BlockSpec((ns, tl, W_A), lambda i, j: (i, j, P_UA // W_A)),
            pl.BlockSpec((None, ns, POOL_HIST, W_A), lambda i, j: (lh, i, 0, 0)),
            pl.BlockSpec((None, W_A, W_A), lambda i, j: (l, 0, 0)),
            pl.BlockSpec((None, 1, W_A), lambda i, j: (l, 0, 0)),
        ],
        out_specs=(pl.BlockSpec((ns, tl, W_A), lambda i, j: (i, j, 0)),
                   pl.BlockSpec((ns, POOL_HIST, W_A), lambda i, j: (i, 0, 0))),
        scratch_shapes=[pltpu.VMEM((ns, tl + 16, W_A), f32)],
        compiler_params=_params(2),
        name=f"pool_t{t}",
    )(p.reshape(b, t, P_COLS), hist, wbd, scale)
    return y.reshape(b * t, W_A), st


LOG2E = 1.4426950408889634
Q_SCALE = DH_B ** -0.5 * LOG2E


def _softmax_av(s_parts, av_fns):
    m = functools.reduce(jnp.maximum, [jnp.max(s, axis=-1, keepdims=True) for s in s_parts])
    e_parts = [jnp.exp2(s - m) for s in s_parts]
    l = functools.reduce(jnp.add, [jnp.sum(e, axis=-1, keepdims=True) for e in e_parts])
    o = functools.reduce(jnp.add, [av(e) for e, av in zip(e_parts, av_fns)])
    return o / l


BIAS_PERIOD = 2 * K_WINDOW


def _band_attn_kernel(q_ref, k_ref, v_ref, wrow_ref, y_ref, kc_ref, vc_ref, bias_ref, *, t, keep):
    qb = pl.program_id(1)

    @pl.when((pl.program_id(0) == 0) & (qb == 0))
    def _():
        qi = _iota((Q_BLOCK, K_WINDOW), 0)
        kj = _iota((Q_BLOCK, K_WINDOW), 1)
        for h in range(H_B):
            base = pltpu.roll(jnp.broadcast_to(wrow_ref[h], (Q_BLOCK, BIAS_PERIOD)), 0, 1,
                              stride=1, stride_axis=0)
            for v in range(3):
                dchunk = (qi + v * Q_BLOCK) // CHUNK - kj // CHUNK
                vis = (dchunk >= 0) & (dchunk <= BAND_CHUNKS)
                lo = (2 - v) * Q_BLOCK
                bias_ref[v, h] = jnp.where(vis, base[:, lo:lo + K_WINDOW], NEG_INF)

    start = pl.multiple_of(jnp.maximum(qb * Q_BLOCK - BAND_CHUNKS * CHUNK, 0), Q_BLOCK)
    variant = jnp.minimum(qb, 2)
    outs = []
    for h in range(H_B):
        lanes = slice(h * DH_B, (h + 1) * DH_B)
        q = q_ref[:, lanes] * Q_SCALE
        k = k_ref[pl.ds(start, K_WINDOW), lanes]
        v = v_ref[pl.ds(start, K_WINDOW), lanes]
        s = _dot_nt(q, k) + bias_ref[variant, h]
        outs.append(_softmax_av([s], [lambda e, v=v: _dot(e, v)]))
    y_ref[...] = jnp.concatenate(outs, axis=1)

    @pl.when(qb == pl.num_programs(1) - 1)
    def _():
        for h in range(H_B):
            lanes = slice(h * DH_B, (h + 1) * DH_B)
            kc_ref[h] = k_ref[t - keep:t, lanes]
            vc_ref[h] = v_ref[t - keep:t, lanes]


def _rel_bias_rows(table, nq, nk, off):
    period = nq + nk
    m = np.arange(period)
    rel = np.where(m < nk, m, m - period) - off
    row = table.astype(f32)[..., np.clip(rel, -REL_CLIP, REL_CLIP) + REL_CLIP]
    lead = table.shape[:-1]
    flat = jnp.tile(row, (1,) * len(lead) + (nq,))[..., :nq * (period - 1)]
    return flat.reshape(lead + (nq, period - 1))[..., :nk]


def _band_bias_row(table):
    m = np.arange(BIAS_PERIOD)
    e = np.where(m < 2 * Q_BLOCK + K_WINDOW, m, m - BIAS_PERIOD)
    idx = np.clip(e - 2 * Q_BLOCK, -REL_CLIP, REL_CLIP) + REL_CLIP
    return table.astype(f32)[..., idx][:, :, None, :]


def _band_attn_call(p, b, t, wrow, l):
    nq = t // Q_BLOCK
    keep = min(ATTN_WINDOW, t)
    cache = jax.ShapeDtypeStruct((b, H_B, keep, DH_B), f32)
    cache_spec = pl.BlockSpec((None, H_B, keep, DH_B), lambda i, j: (i, 0, 0, 0))
    return pl.pallas_call(
        functools.partial(_band_attn_kernel, t=t, keep=keep),
        out_shape=(jax.ShapeDtypeStruct((b * t, W_B), f32), cache, cache),
        grid=(b, nq),
        in_specs=[
            pl.BlockSpec((Q_BLOCK, W_B), lambda i, j: (i * nq + j, P_QB // W_B)),
            pl.BlockSpec((t, W_B), lambda i, j: (i, P_KB // W_B)),
            pl.BlockSpec((t, W_B), lambda i, j: (i, P_VB // W_B)),
            pl.BlockSpec((None, H_B, 1, BIAS_PERIOD), lambda i, j: (l, 0, 0, 0)),
        ],
        out_specs=(pl.BlockSpec((Q_BLOCK, W_B), lambda i, j: (i * nq + j, 0)), cache_spec, cache_spec),
        scratch_shapes=[pltpu.VMEM((3, H_B, Q_BLOCK, K_WINDOW), f32)],
        compiler_params=_params(2),
        name="band_attn",
    )(p, p, p, wrow)


def _sample_attn_kernel(q_ref, k_ref, v_ref, kct_ref, vct_ref, bias_ref, y_ref, kn_ref, vn_ref,
                        *, ns, n_cache):
    def one_sequence(s, carry):
        outs = []
        for h in range(H_B):
            lanes = slice(h * DH_B, (h + 1) * DH_B)
            q = q_ref[s, :, lanes] * Q_SCALE
            k_new, v_new = k_ref[s, :, lanes], v_ref[s, :, lanes]
            s_old = _dot(q, kct_ref[s, h]) + bias_ref[h, :, 0:n_cache]
            s_new = _dot_nt(q, k_new) + bias_ref[h, :, n_cache:]
            outs.append(_softmax_av(
                [s_old, s_new],
                [lambda e, h=h: _dot_nt(e, vct_ref[s, h]), lambda e, v=v_new: _dot(e, v)]))
            kn_ref[s, h] = k_new
            vn_ref[s, h] = v_new
        y_ref[s] = jnp.concatenate(outs, axis=1)
        return carry

    lax.fori_loop(0, ns, one_sequence, 0, unroll=4)


def _sample_attn_call(p, b, t, k_cache_t, v_cache_t, bias, l):
    n_cache = k_cache_t.shape[4]
    ns = min(b, SAMPLE_SEQS)
    cache_spec = pl.BlockSpec((None, ns, H_B, DH_B, n_cache), lambda i: (l, i, 0, 0, 0))
    new = jax.ShapeDtypeStruct((b, H_B, t, DH_B), f32)
    new_spec = pl.BlockSpec((ns, H_B, t, DH_B), lambda i: (i, 0, 0, 0))
    p3 = p.reshape(b, t, P_COLS)
    y, k_new, v_new = pl.pallas_call(
        functools.partial(_sample_attn_kernel, ns=ns, n_cache=n_cache),
        out_shape=(jax.ShapeDtypeStruct((b, t, W_B), f32), new, new),
        grid=(b // ns,),
        in_specs=[
            pl.BlockSpec((ns, t, W_B), lambda i: (i, 0, P_QB // W_B)),
            pl.BlockSpec((ns, t, W_B), lambda i: (i, 0, P_KB // W_B)),
            pl.BlockSpec((ns, t, W_B), lambda i: (i, 0, P_VB // W_B)),
            cache_spec, cache_spec,
            pl.BlockSpec((None, H_B, t, n_cache + t), lambda i: (l, 0, 0, 0)),
        ],
        out_specs=(pl.BlockSpec((ns, t, W_B), lambda i: (i, 0, 0)), new_spec, new_spec),
        compiler_params=_params(1),
        name="sample_attn",
    )(p3, p3, p3, k_cache_t, v_cache_t, bias)
    return y.reshape(b * t, W_B), k_new, v_new


def _delta_kernel(q_ref, k_ref, v_ref, z_ref, sm_ref, hist_ref, s0_ref, cw_ref, alog_ref, dtb_ref, ng_ref,
                  y_ref, sout_ref, cout_ref, ext_ref, s_ref, *, ns, tl, c, carry):
    t_idx = pl.program_id(1)
    tb = ns * tl
    nch = tb // c
    cw = H_C * c

    def chunks(x):
        return x.reshape(nch, c, x.shape[-1])

    def pack_cols(x3):
        if c == DK_C:
            return x3
        return jnp.concatenate([x3[:, :, h * DK_C:h * DK_C + c] for h in range(H_C)], axis=-1)

    def block_t(x3, width):
        return jnp.concatenate(
            [jnp.swapaxes(x3[:, :, h * width:(h + 1) * width], 1, 2) for h in range(H_C)], axis=-1)

    def head_mask(rows, row_w, cols, col_w):
        return (_iota((rows, cols), 0) // row_w == _iota((rows, cols), 1) // col_w)[None]

    def block_diag(y3, mask):
        return jnp.where(mask, jnp.concatenate([y3] * H_C, axis=1), 0.0)

    if carry:
        @pl.when(t_idx == 0)
        def _():
            s_ref[...] = jnp.concatenate([s0_ref[:, h] for h in range(H_C)], axis=-1)
            for s in range(3):
                ext_ref[s, :, 0:8, :] = jnp.zeros((ns, 8, 256), f32)
                ext_ref[s, :, 5:8, :] = hist_ref[:, :, s * 256:(s + 1) * 256]
    else:
        for s in range(3):
            ext_ref[s, :, 5:8, :] = hist_ref[:, :, s * 256:(s + 1) * 256]

    parts = []
    for s, ref in enumerate((q_ref, k_ref, v_ref)):
        ext_ref[s, :, 8:8 + tl, :] = ref[...]
        acc = ext_ref[s, :, 5:5 + tl, :] * cw_ref[0:1, s * 256:(s + 1) * 256]
        for j in range(1, CONV_W):
            acc = acc + ext_ref[s, :, 5 + j:5 + j + tl, :] * cw_ref[j:j + 1, s * 256:(s + 1) * 256]
        parts.append(_silu(acc).reshape(tb, 256))
        last = ext_ref[s, :, tl + 5:tl + 8, :]
        if carry:
            ext_ref[s, :, 5:8, :] = last
        cout_ref[:, :, s * 256:(s + 1) * 256] = last
    cq, ck, cv = parts
    seg = _seg_ones(256, DK_C, 256, DK_C)
    qn = cq * lax.rsqrt(_seg_sum(cq * cq, seg) + EPS) * (DK_C ** -0.5)
    kn = ck * lax.rsqrt(_seg_sum(ck * ck, seg) + EPS)

    sm = sm_ref[...].reshape(tb, SMALL_W)
    g = -jnp.exp(alog_ref[...]) * jax.nn.softplus(sm + dtb_ref[...])
    beta = jax.nn.sigmoid(sm)
    ltri = jnp.broadcast_to((_iota((c, c), 0) >= _iota((c, c), 1)).astype(bf16)[None], (nch, c, c))
    gcum = _bmm_sel(ltri, g.reshape(nch, c, SMALL_W)).reshape(tb, SMALL_W)
    lane_head = _iota((SMALL_W, 256), 1) // 64
    gb = _dot_sel(gcum, (_iota((SMALL_W, 256), 0) == lane_head).astype(bf16))
    betab = _seg_sum(beta, (_iota((SMALL_W, 256), 0) == lane_head + H_C).astype(bf16))
    eg = jnp.exp(gb)
    gl = jnp.broadcast_to(gb.reshape(nch, c, 256)[:, c - 1:c, :], (nch, c, 256)).reshape(tb, 256)

    m_k2c = head_mask(W_C, DK_C, cw, c)
    m_c2c = head_mask(cw, c, cw, c)
    m_c2v = head_mask(cw, c, W_C, DV_C)
    m_k2v = head_mask(W_C, DK_C, W_C, DV_C)
    kn3, qn3 = chunks(kn), chunks(qn)
    bd_kt = jnp.where(m_k2c, jnp.concatenate([jnp.swapaxes(kn3, 1, 2)] * H_C, axis=-1), 0.0)
    gcb = pack_cols(chunks(gb))
    grb = block_t(gcb, c)
    rp, cp = _iota((c, cw), 0), _iota((c, cw), 1) % c
    incl_p = (rp >= cp)[None]
    dec = jnp.where(incl_p, jnp.exp(jnp.where(incl_p, gcb - grb, 0.0)), 0.0)
    a = jnp.where((rp > cp)[None], pack_cols(chunks(betab)) * _bmm(kn3, bd_kt) * dec, 0.0)
    d = jnp.where(rp == cp, 1.0, 0.0)[None] - jnp.where(((rp // 2 == cp // 2) & (rp > cp))[None], a, 0.0)
    s = 2
    while s < c:
        lower_left = (rp // (2 * s) == cp // (2 * s)) & (rp % (2 * s) >= s) & (cp % (2 * s) < s)
        e = jnp.where(lower_left[None], a, 0.0)
        d = d - _bmm(d, block_diag(_bmm(e, block_diag(d, m_c2c)), m_c2c))
        s *= 2
    u = _bmm(d, block_diag(chunks(betab * cv), m_c2v))
    w = _bmm(d, block_diag(chunks(betab * eg * kn), m_c2v))
    qk = _bmm(qn3, bd_kt) * dec
    qt3 = chunks(qn * eg)
    ktt = block_t(chunks(kn * jnp.exp(gl - gb)), DK_C)
    egl = chunks(jnp.exp(gl))[:, 0:1, :]

    if carry:
        o_chunks = [None] * nch
        for s in range(ns):
            st = s_ref[s]
            for ic in range(s * (tl // c), (s + 1) * (tl // c)):
                bds = jnp.where(m_k2v[0], jnp.concatenate([st] * H_C, axis=0), 0.0)
                un = u[ic] - _dot(w[ic], bds)
                bdun = jnp.where(m_c2v[0], jnp.concatenate([un] * H_C, axis=0), 0.0)
                o_chunks[ic] = _dot(qt3[ic], bds) + _dot(qk[ic], bdun)
                st = egl[ic] * st + _dot(ktt[ic], bdun)
            s_ref[s] = st
        o = jnp.concatenate(o_chunks, axis=0)

        @pl.when(t_idx == pl.num_programs(1) - 1)
        def _():
            for h in range(H_C):
                sout_ref[:, h] = s_ref[:, :, h * DV_C:(h + 1) * DV_C]
    else:
        st = jnp.concatenate([s0_ref[:, h] for h in range(H_C)], axis=-1)
        bds = block_diag(st, m_k2v)
        un = u - _bmm(w, bds)
        bdun = block_diag(un, m_c2v)
        o = (_bmm(qt3, bds) + _bmm(qk, bdun)).reshape(tb, W_C)
        st = egl * st + _bmm(ktt, bdun)
        for h in range(H_C):
            sout_ref[:, h] = st[:, :, h * DV_C:(h + 1) * DV_C]

    o = o * lax.rsqrt(_seg_sum(o * o, seg) * (1.0 / DV_C) + EPS) * ng_ref[...]
    y_ref[...] = (o * _silu(z_ref[...].reshape(tb, W_C))).reshape(ns, tl, W_C)


def _delta_call(p, b, t, hist, s0, lh, conv_w, alog, dtb, ng, l):
    c = min(CHUNK, t)
    carry = t > c
    ns = min(b, DELTA_SEQS) if carry else min(b, DELTA_ROWS // (4 * t))
    tl = min(t, DELTA_ROWS // ns) if carry else t
    nt = t // tl
    p3 = p.reshape(b, t, P_COLS)

    def col(cb):
        return lambda i, j: (i, j, cb)

    layer = lambda i, j: (l, 0, 0)
    state_spec = pl.BlockSpec((ns, H_C, DK_C, DV_C), lambda i, j: (i, 0, 0, 0))
    y, s_new, conv_new = pl.pallas_call(
        functools.partial(_delta_kernel, ns=ns, tl=tl, c=c, carry=carry),
        out_shape=(jax.ShapeDtypeStruct((b, t, W_C), f32),
                   jax.ShapeDtypeStruct((b, H_C, DK_C, DV_C), f32),
                   jax.ShapeDtypeStruct((b, CONV_W - 1, QKV_C), f32)),
        grid=(b // ns, nt),
        in_specs=[
            pl.BlockSpec((ns, tl, 256), col(P_QKVC // 256)),
            pl.BlockSpec((ns, tl, 256), col(P_QKVC // 256 + 1)),
            pl.BlockSpec((ns, tl, 256), col(P_QKVC // 256 + 2)),
            pl.BlockSpec((ns, tl, 256), col(P_ZC // 256)),
            pl.BlockSpec((ns, tl, SMALL_W), col(P_SMALL // SMALL_W)),
            pl.BlockSpec((None, ns, CONV_W - 1, QKV_C), lambda i, j: (lh, i, 0, 0)),
            pl.BlockSpec((None, ns, H_C, DK_C, DV_C), lambda i, j: (lh, i, 0, 0, 0)),
            pl.BlockSpec((None, CONV_W, QKV_C), layer),
            pl.BlockSpec((None, 1, SMALL_W), layer),
            pl.BlockSpec((None, 1, SMALL_W), layer),
            pl.BlockSpec((None, 1, W_C), layer),
        ],
        out_specs=(pl.BlockSpec((ns, tl, W_C), lambda i, j: (i, j, 0)),
                   state_spec,
                   pl.BlockSpec((ns, CONV_W - 1, QKV_C), lambda i, j: (i, 0, 0))),
        scratch_shapes=[
            pltpu.VMEM((3, ns, tl + 8, 256), f32),
            pltpu.VMEM((ns, DK_C, W_C), f32),
        ],
        compiler_params=_params(2),
        name=f"delta_t{t}",
    )(p3, p3, p3, p3, p3, hist, s0, conv_w, alog, dtb, ng)
    return y.reshape(b * t, W_C), s_new, conv_new


def _gla_kernel(q_ref, k_ref, v_ref, gate_ref, sm_ref, s0_ref, wgk_ref, bgk_ref, ng_ref,
                y_ref, sout_ref, qs_ref, g_ref, o_ref, s_ref, *, ns, tl, c, carry):
    t_idx = pl.program_id(1)
    tb = ns * tl
    blk = _iota((WK_D, W_D), 0) // DK_D == _iota((WK_D, W_D), 1) // DV_D
    seg_kv = blk.astype(bf16)

    def block_diag(s4):
        zero = jnp.zeros((DK_D, DV_D), f32)
        return jnp.concatenate(
            [jnp.concatenate([s4[h] if h2 == h else zero for h2 in range(H_D)], axis=1)
             for h in range(H_D)], axis=0)

    if carry:
        @pl.when(t_idx == 0)
        def _():
            s_ref[...] = block_diag(s0_ref[0])

    gk = jax.nn.log_sigmoid(_dot(sm_ref[...].reshape(tb, SMALL_W), wgk_ref[...]) + bgk_ref[...])
    g_ref[...] = gk * (1.0 / GLA_GATE_NORM)
    qs_ref[...] = q_ref[...].reshape(tb, WK_D) * (DK_D ** -0.5)
    ltri = (_iota((c, c), 0) >= _iota((c, c), 1)).astype(bf16)
    row8 = _iota((8, WK_D), 0)
    rowid = _iota((c, WK_D), 0)
    lane_head_k = _iota((GLA_SUB, WK_D), 1) // DK_D
    lane_head_v = _iota((GLA_SUB, W_D), 1) // DV_D

    def chunk(ic, st):
        rows = pl.ds(pl.multiple_of(ic * c, c), c)
        g_hi, g_lo = _split(g_ref[rows, :])
        g = _dot(ltri, g_hi) + _dot(ltri, g_lo)
        q = qs_ref[rows, :]
        if carry:
            k, v = k_ref[0, rows, :], v_ref[0, rows, :]
        else:
            k, v, st = k_ref[ic], v_ref[ic], block_diag(s0_ref[ic])
        o_state = _dot(q * jnp.exp(g), st)
        o8 = [None] * (c // 8)
        for jb in range(c // 8):
            lo = 8 * jb
            hi = GLA_SUB * (lo // GLA_SUB + 1)
            m = hi - lo
            e = []
            for j in range(lo, lo + 8):
                ej = jnp.exp(g[lo:hi] - g[j:j + 1]) * (q[lo:hi] * k[j:j + 1])
                first = jnp.where(row8 >= j - lo, ej[:8], 0.0)
                e.append(first if m == 8 else jnp.concatenate([first, ej[8:]], axis=0))
            att = _dot(jnp.concatenate(e, axis=0).astype(bf16), seg_kv)
            upd = att[0:m] * v[lo:lo + 1]
            for jj in range(1, 8):
                upd = upd + att[jj * m:(jj + 1) * m] * v[lo + jj:lo + jj + 1]
            for r in range(m // 8):
                piece = upd[8 * r:8 * (r + 1)]
                o8[jb + r] = piece if o8[jb + r] is None else o8[jb + r] + piece
        for sb in range(1, c // GLA_SUB):
            r0 = sb * GLA_SUB
            ref = g[r0 - 1:r0]
            qx = q[r0:r0 + GLA_SUB] * jnp.exp(g[r0:r0 + GLA_SUB] - ref)
            kx = jnp.where(rowid < r0, k * jnp.exp(jnp.minimum(ref - g, 0.0)), 0.0)
            qh = jnp.concatenate([jnp.where(lane_head_k == h, qx, 0.0) for h in range(H_D)], axis=0)
            res = _dot(_dot_nt(qh, kx), v)
            upd = None
            for h in range(H_D):
                part = jnp.where(lane_head_v == h, res[h * GLA_SUB:(h + 1) * GLA_SUB], 0.0)
                upd = part if upd is None else upd + part
            for r in range(GLA_SUB // 8):
                o8[r0 // 8 + r] = o8[r0 // 8 + r] + upd[8 * r:8 * (r + 1)]
        o_ref[rows, :] = o_state + jnp.concatenate(o8, axis=0)
        gl = g[c - 1:c]
        egl_col = jnp.exp(g.T[:, c - 1:c])
        s_new = jnp.where(blk, egl_col * st + _dot_tn(k * jnp.exp(gl - g), v), 0.0)
        if not carry:
            for h in range(H_D):
                sout_ref[ic, h] = s_new[h * DK_D:(h + 1) * DK_D, h * DV_D:(h + 1) * DV_D]
        return s_new

    s_first = s_ref[...] if carry else jnp.zeros((WK_D, W_D), f32)
    s_last = lax.fori_loop(0, tb // c, chunk, s_first, unroll=min(GLA_UNROLL, tb // c))

    o = o_ref[...]
    o = o * lax.rsqrt(_seg_sum(o * o, _seg_ones(W_D, DV_D, W_D, DV_D)) * (1.0 / DV_D) + EPS) * ng_ref[...]
    y_ref[...] = (o * _silu(gate_ref[...].reshape(tb, W_D))).reshape(ns, tl, W_D)

    if carry:
        s_ref[...] = s_last

        @pl.when(t_idx == pl.num_programs(1) - 1)
        def _():
            for h in range(H_D):
                sout_ref[0, h] = s_ref[h * DK_D:(h + 1) * DK_D, h * DV_D:(h + 1) * DV_D]


def _gla_call(p, b, t, s0, lh, wgk, bgk, ng, l):
    c = min(CHUNK, t)
    carry = t > c
    ns, tl = (1, min(t, ROW_TILE)) if carry else (min(b, SAMPLE_SEQS), t)
    tb = ns * tl
    nt = t // tl
    p3 = p.reshape(b, t, P_COLS)

    def col(cb):
        return lambda i, j: (i, j, cb)

    layer = lambda i, j: (l, 0, 0)
    y, s_new = pl.pallas_call(
        functools.partial(_gla_kernel, ns=ns, tl=tl, c=c, carry=carry),
        out_shape=(jax.ShapeDtypeStruct((b, t, W_D), f32),
                   jax.ShapeDtypeStruct((b, H_D, DK_D, DV_D), f32)),
        grid=(b // ns, nt),
        in_specs=[
            pl.BlockSpec((ns, tl, WK_D), col(P_QD // WK_D)),
            pl.BlockSpec((ns, tl, WK_D), col(P_KD // WK_D)),
            pl.BlockSpec((ns, tl, W_D), col(P_VD // W_D)),
            pl.BlockSpec((ns, tl, W_D), col(P_GD // W_D)),
            pl.BlockSpec((ns, tl, SMALL_W), col(P_SMALL // SMALL_W)),
            pl.BlockSpec((None, ns, H_D, DK_D, DV_D), lambda i, j: (lh, i, 0, 0, 0)),
            pl.BlockSpec((None, SMALL_W, WK_D), layer),
            pl.BlockSpec((None, 1, WK_D), layer),
            pl.BlockSpec((None, 1, W_D), layer),
        ],
        out_specs=(pl.BlockSpec((ns, tl, W_D), lambda i, j: (i, j, 0)),
                   pl.BlockSpec((ns, H_D, DK_D, DV_D), lambda i, j: (i, 0, 0, 0))),
        scratch_shapes=[
            pltpu.VMEM((tb, WK_D), f32), pltpu.VMEM((tb, WK_D), f32),
            pltpu.VMEM((tb, W_D), f32),
            pltpu.VMEM((WK_D, W_D), f32),
        ],
        compiler_params=_params(2),
        name=f"gla_t{t}",
    )(p3, p3, p3, p3, p3, s0, wgk, bgk, ng)
    return y.reshape(b * t, W_D), s_new


def _reorder_w_in(w_in):
    pad = jnp.zeros(w_in.shape[:-1] + (P_COLS - P_SMALL - 2 * H_C - GLA_RANK,), w_in.dtype)
    return jnp.concatenate([
        w_in[..., 0:2048],
        w_in[..., 2312:2568],
        w_in[..., 2568:2824],
        w_in[..., 2056:2184],
        w_in[..., 2184:2312],
        w_in[..., 2048:2056],
        w_in[..., 2824:2840],
        pad], axis=-1)


def _lane_pad(x, width):
    return jnp.pad(x, ((0, 0), (0, width - x.shape[-1])))[:, None, :]


def kernel(x_prompt, x_sample, cache_pool, cache_attn_k, cache_attn_v, state_conv, state_delta,
           state_gla, attn_norm_g, w_in, pool_w, pool_scale, rel_bias, conv_w, a_log, dt_bias,
           delta_norm_g, gla_w_gk, gla_b_gk, gla_norm_g, w_out, mlp_norm_g, w_up, w_down,
           final_norm_g):
    bp, tp, _ = x_prompt.shape
    bs, ts, _ = x_sample.shape
    w_in_r = _reorder_w_in(w_in.astype(bf16))
    w_out_b, w_up_b, w_down_b = w_out.astype(bf16), w_up.astype(bf16), w_down.astype(bf16)
    g1 = attn_norm_g[:, None, :]
    g2 = mlp_norm_g[:, None, :]
    gf = final_norm_g.reshape(1, D_MODEL)
    pool_wbd = jnp.zeros((DEPTH, W_A, W_A), f32)
    for gi in range(N_POOL):
        sl = slice(gi * POOL_GW, (gi + 1) * POOL_GW)
        pool_wbd = pool_wbd.at[:, sl, sl].set(pool_w[:, gi])
    pool_sc = pool_scale[:, None, :]
    band_bias = _band_bias_row(rel_bias) * LOG2E
    n_cache = cache_attn_k.shape[3]
    sample_bias = _rel_bias_rows(rel_bias, ts, n_cache + ts, n_cache) * LOG2E
    cache_kt, cache_vt = jnp.swapaxes(cache_attn_k, 3, 4), jnp.swapaxes(cache_attn_v, 3, 4)
    alog = _lane_pad(a_log, SMALL_W)
    dtb = _lane_pad(dt_bias, SMALL_W)
    ng_c = jnp.tile(delta_norm_g, (1, H_C))[:, None, :]
    wgk = jnp.zeros((DEPTH, SMALL_W, WK_D), f32).at[:, 2 * H_C:2 * H_C + GLA_RANK].set(gla_w_gk)
    bgk = gla_b_gk[:, None, :]
    ng_d = jnp.tile(gla_norm_g, (1, H_D))[:, None, :]
    zeros = (jnp.zeros((1, bp, POOL_HIST, W_A), f32), jnp.zeros((1, bp, CONV_W - 1, QKV_C), f32),
             jnp.zeros((1, bp, H_C, DK_C, DV_C), f32), jnp.zeros((1, bp, H_D, DK_D, DV_D), f32))
    carried = (cache_pool, state_conv, state_delta, state_gla)

    xs = [x_prompt.reshape(bp * tp, D_MODEL), x_sample.reshape(bs * ts, D_MODEL)]
    states = ([], [])
    for l in range(DEPTH):
        for s, (b, t, pos0) in enumerate(((bp, tp, 0), (bs, ts, tp))):
            pool_h, conv_h, sd_h, sg_h = zeros if s == 0 else carried
            lh = 0 if s == 0 else l
            p = _inproj(xs[s], g1, w_in_r, l)
            y_a, st_pool = _pool_call(p, b, t, pos0, pool_h, lh, pool_wbd, pool_sc, l)
            if s == 0:
                y_b, st_k, st_v = _band_attn_call(p, b, t, band_bias, l)
            else:
                y_b, st_k, st_v = _sample_attn_call(p, b, t, cache_kt, cache_vt, sample_bias, l)
            y_c, st_d, st_conv = _delta_call(p, b, t, conv_h, sd_h, lh, conv_w, alog, dtb, ng_c, l)
            y_d, st_g = _gla_call(p, b, t, sg_h, lh, wgk, bgk, ng_d, l)
            xs[s] = _outmlp(xs[s], (y_a, y_b, y_c, y_d), w_out_b, g2, w_up_b, w_down_b, gf, l,
                            final=(l == DEPTH - 1))
            states[s].append((st_pool, st_k, st_v, st_conv, st_d, st_g))
    y_prompt = xs[0].reshape(bp, tp, D_MODEL)
    y_sample = xs[1].reshape(bs, ts, D_MODEL)
    outs_p = tuple(jnp.stack([st[i] for st in states[0]]) for i in range(6))
    outs_s = tuple(jnp.stack([st[i] for st in states[1]]) for i in range(6))
    return (y_prompt, y_sample) + outs_p + outs_s
```

```python
import functools

import jax
import jax.numpy as jnp
import numpy as np
from jax import lax
from jax.experimental import pallas as pl
from jax.experimental.pallas import tpu as pltpu

D_MODEL = 1024
DEPTH = 4
CHUNK = 64
W_A = W_B = W_C = W_D = 256
POOL_WINDOWS = (2, 4, 8, 16)
N_POOL = 4
POOL_GW = 64
POOL_HIST = 15
H_B = 4
DH_B = 64
BAND_CHUNKS = 8
ATTN_WINDOW = 512
REL_CLIP = 256
H_C = 4
DK_C = 64
DV_C = 64
CONV_W = 4
QKV_C = 768
H_D = 4
DK_D = 32
DV_D = 64
WK_D = 128
GLA_RANK = 16
GLA_GATE_NORM = 16.0
D_FF = 4096
EPS = 1e-6
NEG_INF = -1e30

P_UA = 0
P_QB, P_KB, P_VB = 256, 512, 768
P_QKVC = 1024
P_ZC = 1792
P_KEEP = 2048
P_VD = 2048
P_GD = 2304
P_QD = 2560
P_KD = 2688
P_SMALL = 2816
SMALL_W = 128
P_COLS = 2944

ROW_TILE = 512
FF_CHUNK = 1024
POOL_TILE = 2048
SAMPLE_SEQS = 8
GLA_SUB = 16
GLA_UNROLL = 8
DELTA_ROWS = 2048
DELTA_SEQS = 8
Q_BLOCK = 256
K_WINDOW = Q_BLOCK + BAND_CHUNKS * CHUNK
VMEM_LIMIT = 56 * 1024 * 1024

f32 = jnp.float32
bf16 = jnp.bfloat16


def _params(n_axes):
    return pltpu.CompilerParams(dimension_semantics=("arbitrary",) * n_axes, vmem_limit_bytes=VMEM_LIMIT)


def _dot(a, b):
    return jnp.dot(a, b, preferred_element_type=f32)


def _dot_nt(a, b):
    return lax.dot_general(a, b, (((1,), (1,)), ((), ())), preferred_element_type=f32)


def _dot_tn(a, b):
    return lax.dot_general(a, b, (((0,), (0,)), ((), ())), preferred_element_type=f32)


def _bmm(a, b):
    return jnp.einsum('nij,njk->nik', a, b, preferred_element_type=f32)


def _split(x):
    hi = x.astype(bf16)
    return hi, (x - hi.astype(f32)).astype(bf16)


def _dot_sel(x, sel):
    hi, lo = _split(x)
    return _dot(hi, sel) + _dot(lo, sel)


def _seg_sum(x, sel):
    return _dot(x.astype(bf16), sel)


def _bmm_sel(sel, x):
    hi, lo = _split(x)
    return _bmm(sel, hi) + _bmm(sel, lo)


def _iota(shape, dim):
    return lax.broadcasted_iota(jnp.int32, shape, dim)


def _seg_ones(n_in, seg_in, n_out, seg_out):
    return (_iota((n_in, n_out), 0) // seg_in == _iota((n_in, n_out), 1) // seg_out).astype(bf16)


def _silu(x):
    return x * jax.nn.sigmoid(x)


def _rms(x):
    return x * lax.rsqrt(jnp.mean(x * x, axis=-1, keepdims=True) + EPS)


def _inproj_kernel(x_ref, g_ref, wm_ref, wt_ref, o_ref):
    h = (_rms(x_ref[...]) * g_ref[...]).astype(bf16)
    o_ref[:, :P_KEEP] = _dot(h, wm_ref[...])
    o_ref[:, P_KEEP:] = _dot(h, wt_ref[...])


def _inproj(x, g, w_main, w_tail, l):
    n = x.shape[0]
    single = pl.Buffered(1)
    return pl.pallas_call(
        _inproj_kernel,
        out_shape=jax.ShapeDtypeStruct((n, P_COLS), f32),
        grid=(n // ROW_TILE,),
        in_specs=[
            pl.BlockSpec((ROW_TILE, D_MODEL), lambda i: (i, 0)),
            pl.BlockSpec((None, 1, D_MODEL), lambda i: (l, 0, 0)),
            pl.BlockSpec((None, D_MODEL, P_KEEP), lambda i: (l, 0, 0), pipeline_mode=single),
            pl.BlockSpec((None, D_MODEL, P_COLS - P_KEEP), lambda i: (l, 0, 0), pipeline_mode=single),
        ],
        out_specs=pl.BlockSpec((ROW_TILE, P_COLS), lambda i: (i, 0)),
        compiler_params=_params(1),
        name="inproj",
    )(x, g, w_main, w_tail)


def _outmlp_kernel(x_ref, ya_ref, yb_ref, yc_ref, yd_ref, wo_ref, g2_ref, wu_ref, wd_ref, gf_ref,
                   o_ref, *, final):
    acc = None
    for k, y_ref in enumerate((ya_ref, yb_ref, yc_ref, yd_ref)):
        d = _dot(y_ref[...].astype(bf16), wo_ref[k * 256:(k + 1) * 256, :])
        acc = d if acc is None else acc + d
    x = x_ref[...] + acc
    h2 = (_rms(x) * g2_ref[...]).astype(bf16)
    acc = None
    for c in range(D_FF // FF_CHUNK):
        up = _dot(h2, wu_ref[:, c * FF_CHUNK:(c + 1) * FF_CHUNK])
        up = jnp.square(jnp.maximum(up, 0.0)).astype(bf16)
        d = _dot(up, wd_ref[c * FF_CHUNK:(c + 1) * FF_CHUNK, :])
        acc = d if acc is None else acc + d
    x = x + acc
    if final:
        x = _rms(x) * gf_ref[...]
    o_ref[...] = x


def _outmlp(x, ys, wo, g2, wu, wd, gf, l, final):
    n = x.shape[0]
    row = lambda i: (i, 0)
    layer = lambda i: (l, 0, 0)
    single = pl.Buffered(1)
    return pl.pallas_call(
        functools.partial(_outmlp_kernel, final=final),
        out_shape=jax.ShapeDtypeStruct((n, D_MODEL), f32),
        grid=(n // ROW_TILE,),
        in_specs=[pl.BlockSpec((ROW_TILE, D_MODEL), row)]
        + [pl.BlockSpec((ROW_TILE, 256), row)] * 4
        + [pl.BlockSpec((None, D_MODEL, D_MODEL), layer, pipeline_mode=single),
           pl.BlockSpec((None, 1, D_MODEL), layer),
           pl.BlockSpec((None, D_MODEL, D_FF), layer, pipeline_mode=single),
           pl.BlockSpec((None, D_FF, D_MODEL), layer, pipeline_mode=single),
           pl.BlockSpec((1, D_MODEL), lambda i: (0, 0))],
        out_specs=pl.BlockSpec((ROW_TILE, D_MODEL), row),
        compiler_params=_params(1),
        name="outmlp_final" if final else "outmlp",
    )(x, *ys, wo, g2, wu, wd, gf)


def _pool_kernel(u_ref, hist_ref, w_ref, scale_ref, y_ref, st_ref, ext_ref, *, ns, tl, pos0):
    t_idx = pl.program_id(1)

    @pl.when(t_idx == 0)
    def _():
        ext_ref[:, 0:8, :] = jnp.zeros((ns, 8, W_A), f32)
        ext_ref[:, 1:16, :] = hist_ref[...]

    x = u_ref[...]
    ext_ref[:, 16:16 + tl, :] = x
    wsum = {}
    acc = x
    for k in range(1, 16):
        acc = acc + ext_ref[:, 16 - k:16 - k + tl, :]
        if k + 1 in POOL_WINDOWS:
            wsum[k + 1] = acc
    ext_ref[:, 1:16, :] = ext_ref[:, tl + 1:tl + 16, :]
    lane_group = _iota((ns, tl, W_A), 2) // POOL_GW
    pos = pos0 + t_idx * tl + _iota((ns, tl, W_A), 1)
    ws, win = wsum[POOL_WINDOWS[-1]], jnp.full((ns, tl, W_A), POOL_WINDOWS[-1], jnp.int32)
    for gi in range(N_POOL - 2, -1, -1):
        ws = jnp.where(lane_group == gi, wsum[POOL_WINDOWS[gi]], ws)
        win = jnp.where(lane_group == gi, POOL_WINDOWS[gi], win)
    cnt = jnp.minimum(win, pos + 1).astype(f32)
    pooled = (ws / cnt - x).reshape(ns * tl, W_A)
    y_ref[...] = (_dot(pooled, w_ref[...]) * scale_ref[...]).reshape(ns, tl, W_A)

    @pl.when(t_idx == pl.num_programs(1) - 1)
    def _():
        st_ref[...] = ext_ref[:, 1:16, :]


def _pool_call(p, b, t, pos0, hist, lh, wbd, scale, l):
    tl = min(t, POOL_TILE)
    ns = min(b, POOL_TILE // tl)
    nt = t // tl
    y, st = pl.pallas_call(
        functools.partial(_pool_kernel, ns=ns, tl=tl, pos0=pos0),
        out_shape=(jax.ShapeDtypeStruct((b, t, W_A), f32),
                   jax.ShapeDtypeStruct((b, POOL_HIST, W_A), f32)),
        grid=(b // ns, nt),
        in_specs=[
            pl.BlockSpec((ns, tl, W_A), lambda i, j: (i, j, P_UA // W_A)),
            pl.BlockSpec((None, ns, POOL_HIST, W_A), lambda i, j: (lh, i, 0, 0)),
            pl.BlockSpec((None, W_A, W_A), lambda i, j: (l, 0, 0)),
            pl.BlockSpec((None, 1, W_A), lambda i, j: (l, 0, 0)),
        ],
        out_specs=(pl.BlockSpec((ns, tl, W_A), lambda i, j: (i, j, 0)),
                   pl.BlockSpec((ns, POOL_HIST, W_A), lambda i, j: (i, 0, 0))),
        scratch_shapes=[pltpu.VMEM((ns, tl + 16, W_A), f32)],
        compiler_params=_params(2),
        name=f"pool_t{t}",
    )(p.reshape(b, t, P_COLS), hist, wbd, scale)
    return y.reshape(b * t, W_A), st


LOG2E = 1.4426950408889634
Q_SCALE = DH_B ** -0.5 * LOG2E


def _softmax_av(s_parts, av_fns):
    m = functools.reduce(jnp.maximum, [jnp.max(s, axis=-1, keepdims=True) for s in s_parts])
    e_parts = [jnp.exp2(s - m) for s in s_parts]
    l = functools.reduce(jnp.add, [jnp.sum(e, axis=-1, keepdims=True) for e in e_parts])
    o = functools.reduce(jnp.add, [av(e) for e, av in zip(e_parts, av_fns)])
    return o / l


BIAS_PERIOD = 2 * K_WINDOW


def _band_attn_kernel(q_ref, k_ref, v_ref, wrow_ref, y_ref, kc_ref, vc_ref, bias_ref, *, t, keep):
    qb = pl.program_id(1)

    @pl.when((pl.program_id(0) == 0) & (qb == 0))
    def _():
        qi = _iota((Q_BLOCK, K_WINDOW), 0)
        kj = _iota((Q_BLOCK, K_WINDOW), 1)
        for h in range(H_B):
            base = pltpu.roll(jnp.broadcast_to(wrow_ref[h], (Q_BLOCK, BIAS_PERIOD)), 0, 1,
                              stride=1, stride_axis=0)
            for v in range(3):
                dchunk = (qi + v * Q_BLOCK) // CHUNK - kj // CHUNK
                vis = (dchunk >= 0) & (dchunk <= BAND_CHUNKS)
                lo = (2 - v) * Q_BLOCK
                bias_ref[v, h] = jnp.where(vis, base[:, lo:lo + K_WINDOW], NEG_INF)

    start = pl.multiple_of(jnp.maximum(qb * Q_BLOCK - BAND_CHUNKS * CHUNK, 0), Q_BLOCK)
    variant = jnp.minimum(qb, 2)
    outs = []
    for h in range(H_B):
        lanes = slice(h * DH_B, (h + 1) * DH_B)
        q = q_ref[:, lanes] * Q_SCALE
        k = k_ref[pl.ds(start, K_WINDOW), lanes]
        v = v_ref[pl.ds(start, K_WINDOW), lanes]
        s = _dot_nt(q, k) + bias_ref[variant, h]
        outs.append(_softmax_av([s], [lambda e, v=v: _dot(e, v)]))
    y_ref[...] = jnp.concatenate(outs, axis=1)

    @pl.when(qb == pl.num_programs(1) - 1)
    def _():
        for h in range(H_B):
            lanes = slice(h * DH_B, (h + 1) * DH_B)
            kc_ref[h] = k_ref[t - keep:t, lanes]
            vc_ref[h] = v_ref[t - keep:t, lanes]


def _rel_bias_rows(table, nq, nk, off):
    period = nq + nk
    m = np.arange(period)
    rel = np.where(m < nk, m, m - period) - off
    row = table.astype(f32)[..., np.clip(rel, -REL_CLIP, REL_CLIP) + REL_CLIP]
    lead = table.shape[:-1]
    flat = jnp.tile(row, (1,) * len(lead) + (nq,))[..., :nq * (period - 1)]
    return flat.reshape(lead + (nq, period - 1))[..., :nk]


def _band_bias_row(table):
    m = np.arange(BIAS_PERIOD)
    e = np.where(m < 2 * Q_BLOCK + K_WINDOW, m, m - BIAS_PERIOD)
    idx = np.clip(e - 2 * Q_BLOCK, -REL_CLIP, REL_CLIP) + REL_CLIP
    return table.astype(f32)[..., idx][:, :, None, :]


def _band_attn_call(p, b, t, wrow, l):
    nq = t // Q_BLOCK
    keep = min(ATTN_WINDOW, t)
    cache = jax.ShapeDtypeStruct((b, H_B, keep, DH_B), f32)
    cache_spec = pl.BlockSpec((None, H_B, keep, DH_B), lambda i, j: (i, 0, 0, 0))
    return pl.pallas_call(
        functools.partial(_band_attn_kernel, t=t, keep=keep),
        out_shape=(jax.ShapeDtypeStruct((b * t, W_B), f32), cache, cache),
        grid=(b, nq),
        in_specs=[
            pl.BlockSpec((Q_BLOCK, W_B), lambda i, j: (i * nq + j, P_QB // W_B)),
            pl.BlockSpec((t, W_B), lambda i, j: (i, P_KB // W_B)),
            pl.BlockSpec((t, W_B), lambda i, j: (i, P_VB // W_B)),
            pl.BlockSpec((None, H_B, 1, BIAS_PERIOD), lambda i, j: (l, 0, 0, 0)),
        ],
        out_specs=(pl.BlockSpec((Q_BLOCK, W_B), lambda i, j: (i * nq + j, 0)), cache_spec, cache_spec),
        scratch_shapes=[pltpu.VMEM((3, H_B, Q_BLOCK, K_WINDOW), f32)],
        compiler_params=_params(2),
        name="band_attn",
    )(p, p, p, wrow)


def _sample_attn_kernel(q_ref, k_ref, v_ref, kct_ref, vct_ref, bias_ref, y_ref, kn_ref, vn_ref,
                        *, ns, n_cache):
    def one_sequence(s, carry):
        outs = []
        for h in range(H_B):
            lanes = slice(h * DH_B, (h + 1) * DH_B)
            q = q_ref[s, :, lanes] * Q_SCALE
            k_new, v_new = k_ref[s, :, lanes], v_ref[s, :, lanes]
            s_old = _dot(q, kct_ref[s, h]) + bias_ref[h, :, 0:n_cache]
            s_new = _dot_nt(q, k_new) + bias_ref[h, :, n_cache:]
            outs.append(_softmax_av(
                [s_old, s_new],
                [lambda e, h=h: _dot_nt(e, vct_ref[s, h]), lambda e, v=v_new: _dot(e, v)]))
            kn_ref[s, h] = k_new
            vn_ref[s, h] = v_new
        y_ref[s] = jnp.concatenate(outs, axis=1)
        return carry

    lax.fori_loop(0, ns, one_sequence, 0, unroll=4)


def _sample_attn_call(p, b, t, k_cache_t, v_cache_t, bias, l):
    n_cache = k_cache_t.shape[4]
    ns = min(b, SAMPLE_SEQS)
    cache_spec = pl.BlockSpec((None, ns, H_B, DH_B, n_cache), lambda i: (l, i, 0, 0, 0))
    new = jax.ShapeDtypeStruct((b, H_B, t, DH_B), f32)
    new_spec = pl.BlockSpec((ns, H_B, t, DH_B), lambda i: (i, 0, 0, 0))
    p3 = p.reshape(b, t, P_COLS)
    y, k_new, v_new = pl.pallas_call(
        functools.partial(_sample_attn_kernel, ns=ns, n_cache=n_cache),
        out_shape=(jax.ShapeDtypeStruct((b, t, W_B), f32), new, new),
        grid=(b // ns,),
        in_specs=[
            pl.BlockSpec((ns, t, W_B), lambda i: (i, 0, P_QB // W_B)),
            pl.BlockSpec((ns, t, W_B), lambda i: (i, 0, P_KB // W_B)),
            pl.BlockSpec((ns, t, W_B), lambda i: (i, 0, P_VB // W_B)),
            cache_spec, cache_spec,
            pl.BlockSpec((None, H_B, t, n_cache + t), lambda i: (l, 0, 0, 0)),
        ],
        out_specs=(pl.BlockSpec((ns, t, W_B), lambda i: (i, 0, 0)), new_spec, new_spec),
        compiler_params=_params(1),
        name="sample_attn",
    )(p3, p3, p3, k_cache_t, v_cache_t, bias)
    return y.reshape(b * t, W_B), k_new, v_new


def _delta_kernel(q_ref, k_ref, v_ref, z_ref, sm_ref, hist_ref, s0_ref, cw_ref, alog_ref, dtb_ref, ng_ref,
                  y_ref, sout_ref, cout_ref, ext_ref, s_ref, *, ns, tl, c, carry):
    t_idx = pl.program_id(1)
    tb = ns * tl
    nch = tb // c
    cw = H_C * c

    def chunks(x):
        return x.reshape(nch, c, x.shape[-1])

    def pack_cols(x3):
        if c == DK_C:
            return x3
        return jnp.concatenate([x3[:, :, h * DK_C:h * DK_C + c] for h in range(H_C)], axis=-1)

    def block_t(x3, width):
        return jnp.concatenate(
            [jnp.swapaxes(x3[:, :, h * width:(h + 1) * width], 1, 2) for h in range(H_C)], axis=-1)

    def head_mask(rows, row_w, cols, col_w):
        return (_iota((rows, cols), 0) // row_w == _iota((rows, cols), 1) // col_w)[None]

    def block_diag(y3, mask):
        return jnp.where(mask, jnp.concatenate([y3] * H_C, axis=1), 0.0)

    if carry:
        @pl.when(t_idx == 0)
        def _():
            s_ref[...] = jnp.concatenate([s0_ref[:, h] for h in range(H_C)], axis=-1)
            for s in range(3):
                ext_ref[s, :, 0:8, :] = jnp.zeros((ns, 8, 256), f32)
                ext_ref[s, :, 5:8, :] = hist_ref[:, :, s * 256:(s + 1) * 256]
    else:
        for s in range(3):
            ext_ref[s, :, 5:8, :] = hist_ref[:, :, s * 256:(s + 1) * 256]

    parts = []
    for s, ref in enumerate((q_ref, k_ref, v_ref)):
        ext_ref[s, :, 8:8 + tl, :] = ref[...]
        acc = ext_ref[s, :, 5:5 + tl, :] * cw_ref[0:1, s * 256:(s + 1) * 256]
        for j in range(1, CONV_W):
            acc = acc + ext_ref[s, :, 5 + j:5 + j + tl, :] * cw_ref[j:j + 1, s * 256:(s + 1) * 256]
        parts.append(_silu(acc).reshape(tb, 256))
        last = ext_ref[s, :, tl + 5:tl + 8, :]
        if carry:
            ext_ref[s, :, 5:8, :] = last
        cout_ref[:, :, s * 256:(s + 1) * 256] = last
    cq, ck, cv = parts
    seg = _seg_ones(256, DK_C, 256, DK_C)
    qn = cq * lax.rsqrt(_seg_sum(cq * cq, seg) + EPS) * (DK_C ** -0.5)
    kn = ck * lax.rsqrt(_seg_sum(ck * ck, seg) + EPS)

    sm = sm_ref[...].reshape(tb, SMALL_W)
    g = -jnp.exp(alog_ref[...]) * jax.nn.softplus(sm + dtb_ref[...])
    beta = jax.nn.sigmoid(sm)
    ltri = jnp.broadcast_to((_iota((c, c), 0) >= _iota((c, c), 1)).astype(bf16)[None], (nch, c, c))
    gcum = _bmm_sel(ltri, g.reshape(nch, c, SMALL_W)).reshape(tb, SMALL_W)
    lane_head = _iota((SMALL_W, 256), 1) // 64
    gb = _dot_sel(gcum, (_iota((SMALL_W, 256), 0) == lane_head).astype(bf16))
    betab = _seg_sum(beta, (_iota((SMALL_W, 256), 0) == lane_head + H_C).astype(bf16))
    eg = jnp.exp(gb)
    gl = jnp.broadcast_to(gb.reshape(nch, c, 256)[:, c - 1:c, :], (nch, c, 256)).reshape(tb, 256)

    m_k2c = head_mask(W_C, DK_C, cw, c)
    m_c2c = head_mask(cw, c, cw, c)
    m_c2v = head_mask(cw, c, W_C, DV_C)
    m_k2v = head_mask(W_C, DK_C, W_C, DV_C)
    kn3, qn3 = chunks(kn), chunks(qn)
    bd_kt = jnp.where(m_k2c, jnp.concatenate([jnp.swapaxes(kn3, 1, 2)] * H_C, axis=-1), 0.0)
    gcb = pack_cols(chunks(gb))
    grb = block_t(gcb, c)
    rp, cp = _iota((c, cw), 0), _iota((c, cw), 1) % c
    incl_p = (rp >= cp)[None]
    dec = jnp.where(incl_p, jnp.exp(jnp.where(incl_p, gcb - grb, 0.0)), 0.0)
    a = jnp.where((rp > cp)[None], pack_cols(chunks(betab)) * _bmm(kn3, bd_kt) * dec, 0.0)
    d = jnp.where(rp == cp, 1.0, 0.0)[None] - jnp.where(((rp // 2 == cp // 2) & (rp > cp))[None], a, 0.0)
    s = 2
    while s < c:
        lower_left = (rp // (2 * s) == cp // (2 * s)) & (rp % (2 * s) >= s) & (cp % (2 * s) < s)
        e = jnp.where(lower_left[None], a, 0.0)
        d = d - _bmm(d, block_diag(_bmm(e, block_diag(d, m_c2c)), m_c2c))
        s *= 2
    u = _bmm(d, block_diag(chunks(betab * cv), m_c2v))
    w = _bmm(d, block_diag(chunks(betab * eg * kn), m_c2v))
    qk = _bmm(qn3, bd_kt) * dec
    qt3 = chunks(qn * eg)
    ktt = block_t(chunks(kn * jnp.exp(gl - gb)), DK_C)
    egl = chunks(jnp.exp(gl))[:, 0:1, :]

    if carry:
        o_chunks = [None] * nch
        for s in range(ns):
            st = s_ref[s]
            for ic in range(s * (tl // c), (s + 1) * (tl // c)):
                bds = jnp.where(m_k2v[0], jnp.concatenate([st] * H_C, axis=0), 0.0)
                un = u[ic] - _dot(w[ic], bds)
                bdun = jnp.where(m_c2v[0], jnp.concatenate([un] * H_C, axis=0), 0.0)
                o_chunks[ic] = _dot(qt3[ic], bds) + _dot(qk[ic], bdun)
                st = egl[ic] * st + _dot(ktt[ic], bdun)
            s_ref[s] = st
        o = jnp.concatenate(o_chunks, axis=0)

        @pl.when(t_idx == pl.num_programs(1) - 1)
        def _():
            for h in range(H_C):
                sout_ref[:, h] = s_ref[:, :, h * DV_C:(h + 1) * DV_C]
    else:
        st = jnp.concatenate([s0_ref[:, h] for h in range(H_C)], axis=-1)
        bds = block_diag(st, m_k2v)
        un = u - _bmm(w, bds)
        bdun = block_diag(un, m_c2v)
        o = (_bmm(qt3, bds) + _bmm(qk, bdun)).reshape(tb, W_C)
        st = egl * st + _bmm(ktt, bdun)
        for h in range(H_C):
            sout_ref[:, h] = st[:, :, h * DV_C:(h + 1) * DV_C]

    o = o * lax.rsqrt(_seg_sum(o * o, seg) * (1.0 / DV_C) + EPS) * ng_ref[...]
    y_ref[...] = (o * _silu(z_ref[...].reshape(tb, W_C))).reshape(ns, tl, W_C)


def _delta_call(p, b, t, hist, s0, lh, conv_w, alog, dtb, ng, l):
    c = min(CHUNK, t)
    carry = t > c
    ns = min(b, DELTA_SEQS) if carry else min(b, DELTA_ROWS // (4 * t))
    tl = min(t, DELTA_ROWS // ns) if carry else t
    nt = t // tl
    p3 = p.reshape(b, t, P_COLS)

    def col(cb):
        return lambda i, j: (i, j, cb)

    layer = lambda i, j: (l, 0, 0)
    state_spec = pl.BlockSpec((ns, H_C, DK_C, DV_C), lambda i, j: (i, 0, 0, 0))
    y, s_new, conv_new = pl.pallas_call(
        functools.partial(_delta_kernel, ns=ns, tl=tl, c=c, carry=carry),
        out_shape=(jax.ShapeDtypeStruct((b, t, W_C), f32),
                   jax.ShapeDtypeStruct((b, H_C, DK_C, DV_C), f32),
                   jax.ShapeDtypeStruct((b, CONV_W - 1, QKV_C), f32)),
        grid=(b // ns, nt),
        in_specs=[
            pl.BlockSpec((ns, tl, 256), col(P_QKVC // 256)),
            pl.BlockSpec((ns, tl, 256), col(P_QKVC // 256 + 1)),
            pl.BlockSpec((ns, tl, 256), col(P_QKVC // 256 + 2)),
            pl.BlockSpec((ns, tl, 256), col(P_ZC // 256)),
            pl.BlockSpec((ns, tl, SMALL_W), col(P_SMALL // SMALL_W)),
            pl.BlockSpec((None, ns, CONV_W - 1, QKV_C), lambda i, j: (lh, i, 0, 0)),
            pl.BlockSpec((None, ns, H_C, DK_C, DV_C), lambda i, j: (lh, i, 0, 0, 0)),
            pl.BlockSpec((None, CONV_W, QKV_C), layer),
            pl.BlockSpec((None, 1, SMALL_W), layer),
            pl.BlockSpec((None, 1, SMALL_W), layer),
            pl.BlockSpec((None, 1, W_C), layer),
        ],
        out_specs=(pl.BlockSpec((ns, tl, W_C), lambda i, j: (i, j, 0)),
                   state_spec,
                   pl.BlockSpec((ns, CONV_W - 1, QKV_C), lambda i, j: (i, 0, 0))),
        scratch_shapes=[
            pltpu.VMEM((3, ns, tl + 8, 256), f32),
            pltpu.VMEM((ns, DK_C, W_C), f32),
        ],
        compiler_params=_params(2),
        name=f"delta_t{t}",
    )(p3, p3, p3, p3, p3, hist, s0, conv_w, alog, dtb, ng)
    return y.reshape(b * t, W_C), s_new, conv_new


def _gla_kernel(q_ref, k_ref, v_ref, gate_ref, sm_ref, s0_ref, wgk_ref, bgk_ref, ng_ref,
                y_ref, sout_ref, qs_ref, g_ref, o_ref, s_ref, *, ns, tl, c, carry):
    t_idx = pl.program_id(1)
    tb = ns * tl
    blk = _iota((WK_D, W_D), 0) // DK_D == _iota((WK_D, W_D), 1) // DV_D
    seg_kv = blk.astype(bf16)

    def block_diag(s4):
        zero = jnp.zeros((DK_D, DV_D), f32)
        return jnp.concatenate(
            [jnp.concatenate([s4[h] if h2 == h else zero for h2 in range(H_D)], axis=1)
             for h in range(H_D)], axis=0)

    if carry:
        @pl.when(t_idx == 0)
        def _():
            s_ref[...] = block_diag(s0_ref[0])

    gk = jax.nn.log_sigmoid(_dot(sm_ref[...].reshape(tb, SMALL_W), wgk_ref[...]) + bgk_ref[...])
    g_ref[...] = gk * (1.0 / GLA_GATE_NORM)
    qs_ref[...] = q_ref[...].reshape(tb, WK_D) * (DK_D ** -0.5)
    ltri = (_iota((c, c), 0) >= _iota((c, c), 1)).astype(bf16)
    row8 = _iota((8, WK_D), 0)
    rowid = _iota((c, WK_D), 0)
    lane_head_k = _iota((GLA_SUB, WK_D), 1) // DK_D
    lane_head_v = _iota((GLA_SUB, W_D), 1) // DV_D

    def chunk(ic, st):
        rows = pl.ds(pl.multiple_of(ic * c, c), c)
        g_hi, g_lo = _split(g_ref[rows, :])
        g = _dot(ltri, g_hi) + _dot(ltri, g_lo)
        q = qs_ref[rows, :]
        if carry:
            k, v = k_ref[0, rows, :], v_ref[0, rows, :]
        else:
            k, v, st = k_ref[ic], v_ref[ic], block_diag(s0_ref[ic])
        o_state = _dot(q * jnp.exp(g), st)
        o8 = [None] * (c // 8)
        for jb in range(c // 8):
            lo = 8 * jb
            hi = GLA_SUB * (lo // GLA_SUB + 1)
            m = hi - lo
            e = []
            for j in range(lo, lo + 8):
                ej = jnp.exp(g[lo:hi] - g[j:j + 1]) * (q[lo:hi] * k[j:j + 1])
                first = jnp.where(row8 >= j - lo, ej[:8], 0.0)
                e.append(first if m == 8 else jnp.concatenate([first, ej[8:]], axis=0))
            att = _dot(jnp.concatenate(e, axis=0).astype(bf16), seg_kv)
            upd = att[0:m] * v[lo:lo + 1]
            for jj in range(1, 8):
                upd = upd + att[jj * m:(jj + 1) * m] * v[lo + jj:lo + jj + 1]
            for r in range(m // 8):
                piece = upd[8 * r:8 * (r + 1)]
                o8[jb + r] = piece if o8[jb + r] is None else o8[jb + r] + piece
        for sb in range(1, c // GLA_SUB):
            r0 = sb * GLA_SUB
            ref = g[r0 - 1:r0]
            qx = q[r0:r0 + GLA_SUB] * jnp.exp(g[r0:r0 + GLA_SUB] - ref)
            kx = jnp.where(rowid < r0, k * jnp.exp(jnp.minimum(ref - g, 0.0)), 0.0)
            qh = jnp.concatenate([jnp.where(lane_head_k == h, qx, 0.0) for h in range(H_D)], axis=0)
            res = _dot(_dot_nt(qh, kx), v)
            upd = None
            for h in range(H_D):
                part = jnp.where(lane_head_v == h, res[h * GLA_SUB:(h + 1) * GLA_SUB], 0.0)
                upd = part if upd is None else upd + part
            for r in range(GLA_SUB // 8):
                o8[r0 // 8 + r] = o8[r0 // 8 + r] + upd[8 * r:8 * (r + 1)]
        o_ref[rows, :] = o_state + jnp.concatenate(o8, axis=0)
        gl = g[c - 1:c]
        egl_col = jnp.exp(g.T[:, c - 1:c])
        s_new = jnp.where(blk, egl_col * st + _dot_tn(k * jnp.exp(gl - g), v), 0.0)
        if not carry:
            for h in range(H_D):
                sout_ref[ic, h] = s_new[h * DK_D:(h + 1) * DK_D, h * DV_D:(h + 1) * DV_D]
        return s_new

    s_first = s_ref[...] if carry else jnp.zeros((WK_D, W_D), f32)
    s_last = lax.fori_loop(0, tb // c, chunk, s_first, unroll=min(GLA_UNROLL, tb // c))

    o = o_ref[...]
    o = o * lax.rsqrt(_seg_sum(o * o, _seg_ones(W_D, DV_D, W_D, DV_D)) * (1.0 / DV_D) + EPS) * ng_ref[...]
    y_ref[...] = (o * _silu(gate_ref[...].reshape(tb, W_D))).reshape(ns, tl, W_D)

    if carry:
        s_ref[...] = s_last

        @pl.when(t_idx == pl.num_programs(1) - 1)
        def _():
            for h in range(H_D):
                sout_ref[0, h] = s_ref[h * DK_D:(h + 1) * DK_D, h * DV_D:(h + 1) * DV_D]


def _gla_call(p, b, t, s0, lh, wgk, bgk, ng, l):
    c = min(CHUNK, t)
    carry = t > c
    ns, tl = (1, min(t, ROW_TILE)) if carry else (min(b, SAMPLE_SEQS), t)
    tb = ns * tl
    nt = t // tl
    p3 = p.reshape(b, t, P_COLS)

    def col(cb):
        return lambda i, j: (i, j, cb)

    layer = lambda i, j: (l, 0, 0)
    y, s_new = pl.pallas_call(
        functools.partial(_gla_kernel, ns=ns, tl=tl, c=c, carry=carry),
        out_shape=(jax.ShapeDtypeStruct((b, t, W_D), f32),
                   jax.ShapeDtypeStruct((b, H_D, DK_D, DV_D), f32)),
        grid=(b // ns, nt),
        in_specs=[
            pl.BlockSpec((ns, tl, WK_D), col(P_QD // WK_D)),
            pl.BlockSpec((ns, tl, WK_D), col(P_KD // WK_D)),
            pl.BlockSpec((ns, tl, W_D), col(P_VD // W_D)),
            pl.BlockSpec((ns, tl, W_D), col(P_GD // W_D)),
            pl.BlockSpec((ns, tl, SMALL_W), col(P_SMALL // SMALL_W)),
            pl.BlockSpec((None, ns, H_D, DK_D, DV_D), lambda i, j: (lh, i, 0, 0, 0)),
            pl.BlockSpec((None, SMALL_W, WK_D), layer),
            pl.BlockSpec((None, 1, WK_D), layer),
            pl.BlockSpec((None, 1, W_D), layer),
        ],
        out_specs=(pl.BlockSpec((ns, tl, W_D), lambda i, j: (i, j, 0)),
                   pl.BlockSpec((ns, H_D, DK_D, DV_D), lambda i, j: (i, 0, 0, 0))),
        scratch_shapes=[
            pltpu.VMEM((tb, WK_D), f32), pltpu.VMEM((tb, WK_D), f32),
            pltpu.VMEM((tb, W_D), f32),
            pltpu.VMEM((WK_D, W_D), f32),
        ],
        compiler_params=_params(2),
        name=f"gla_t{t}",
    )(p3, p3, p3, p3, p3, s0, wgk, bgk, ng)
    return y.reshape(b * t, W_D), s_new


def _reorder_w_in_tail(w_in):
    pad = jnp.zeros(w_in.shape[:-1] + (P_COLS - P_SMALL - 2 * H_C - GLA_RANK,), w_in.dtype)
    return jnp.concatenate([
        w_in[..., 2312:2568],
        w_in[..., 2568:2824],
        w_in[..., 2056:2184],
        w_in[..., 2184:2312],
        w_in[..., 2048:2056],
        w_in[..., 2824:2840],
        pad], axis=-1)


def _lane_pad(x, width):
    return jnp.pad(x, ((0, 0), (0, width - x.shape[-1])))[:, None, :]


def kernel(x_prompt, x_sample, cache_pool, cache_attn_k, cache_attn_v, state_conv, state_delta,
           state_gla, attn_norm_g, w_in, pool_w, pool_scale, rel_bias, conv_w, a_log, dt_bias,
           delta_norm_g, gla_w_gk, gla_b_gk, gla_norm_g, w_out, mlp_norm_g, w_up, w_down,
           final_norm_g):
    bp, tp, _ = x_prompt.shape
    bs, ts, _ = x_sample.shape
    w_in_main = w_in[..., :P_KEEP].astype(bf16)
    w_in_tail = _reorder_w_in_tail(w_in).astype(bf16)
    w_out_b, w_up_b, w_down_b = w_out.astype(bf16), w_up.astype(bf16), w_down.astype(bf16)
    g1 = attn_norm_g[:, None, :]
    g2 = mlp_norm_g[:, None, :]
    gf = final_norm_g.reshape(1, D_MODEL)
    pool_wbd = jnp.zeros((DEPTH, W_A, W_A), f32)
    for gi in range(N_POOL):
        sl = slice(gi * POOL_GW, (gi + 1) * POOL_GW)
        pool_wbd = pool_wbd.at[:, sl, sl].set(pool_w[:, gi])
    pool_sc = pool_scale[:, None, :]
    band_bias = _band_bias_row(rel_bias) * LOG2E
    n_cache = cache_attn_k.shape[3]
    sample_bias = _rel_bias_rows(rel_bias, ts, n_cache + ts, n_cache) * LOG2E
    cache_kt, cache_vt = jnp.swapaxes(cache_attn_k, 3, 4), jnp.swapaxes(cache_attn_v, 3, 4)
    alog = _lane_pad(a_log, SMALL_W)
    dtb = _lane_pad(dt_bias, SMALL_W)
    ng_c = jnp.tile(delta_norm_g, (1, H_C))[:, None, :]
    wgk = jnp.zeros((DEPTH, SMALL_W, WK_D), f32).at[:, 2 * H_C:2 * H_C + GLA_RANK].set(gla_w_gk)
    bgk = gla_b_gk[:, None, :]
    ng_d = jnp.tile(gla_norm_g, (1, H_D))[:, None, :]
    zeros = (jnp.zeros((1, bp, POOL_HIST, W_A), f32), jnp.zeros((1, bp, CONV_W - 1, QKV_C), f32),
             jnp.zeros((1, bp, H_C, DK_C, DV_C), f32), jnp.zeros((1, bp, H_D, DK_D, DV_D), f32))
    carried = (cache_pool, state_conv, state_delta, state_gla)

    xs = [x_prompt.reshape(bp * tp, D_MODEL), x_sample.reshape(bs * ts, D_MODEL)]
    states = ([], [])
    for l in range(DEPTH):
        for s, (b, t, pos0) in enumerate(((bp, tp, 0), (bs, ts, tp))):
            pool_h, conv_h, sd_h, sg_h = zeros if s == 0 else carried
            lh = 0 if s == 0 else l
            p = _inproj(xs[s], g1, w_in_main, w_in_tail, l)
            y_a, st_pool = _pool_call(p, b, t, pos0, pool_h, lh, pool_wbd, pool_sc, l)
            if s == 0:
                y_b, st_k, st_v = _band_attn_call(p, b, t, band_bias, l)
            else:
                y_b, st_k, st_v = _sample_attn_call(p, b, t, cache_kt, cache_vt, sample_bias, l)
            y_c, st_d, st_conv = _delta_call(p, b, t, conv_h, sd_h, lh, conv_w, alog, dtb, ng_c, l)
            y_d, st_g = _gla_call(p, b, t, sg_h, lh, wgk, bgk, ng_d, l)
            xs[s] = _outmlp(xs[s], (y_a, y_b, y_c, y_d), w_out_b, g2, w_up_b, w_down_b, gf, l,
                            final=(l == DEPTH - 1))
            states[s].append((st_pool, st_k, st_v, st_conv, st_d, st_g))
    y_prompt = xs[0].reshape(bp, tp, D_MODEL)
    y_sample = xs[1].reshape(bs, ts, D_MODEL)
    outs_p = tuple(jnp.stack([st[i] for st in states[0]]) for i in range(6))
    outs_s = tuple(jnp.stack([st[i] for st in states[1]]) for i in range(6))
    return (y_prompt, y_sample) + outs_p + outs_s
```

```python
import functools

import jax
import jax.numpy as jnp
import numpy as np
from jax import lax
from jax.experimental import pallas as pl
from jax.experimental.pallas import tpu as pltpu

D_MODEL = 1024
DEPTH = 4
CHUNK = 64
W_A = W_B = W_C = W_D = 256
POOL_WINDOWS = (2, 4, 8, 16)
N_POOL = 4
POOL_GW = 64
POOL_HIST = 15
H_B = 4
DH_B = 64
BAND_CHUNKS = 8
ATTN_WINDOW = 512
REL_CLIP = 256
H_C = 4
DK_C = 64
DV_C = 64
CONV_W = 4
QKV_C = 768
H_D = 4
DK_D = 32
DV_D = 64
WK_D = 128
GLA_RANK = 16
GLA_GATE_NORM = 16.0
D_FF = 4096
EPS = 1e-6
NEG_INF = -1e30

P_UA = 0
P_QB, P_KB, P_VB = 256, 512, 768
P_QKVC = 1024
P_ZC = 1792
P_VD = 2048
P_GD = 2304
P_QD = 2560
P_KD = 2688
P_SMALL = 2816
SMALL_W = 128
P_COLS = 2944

ROW_TILE = 512
FF_CHUNK = 1024
POOL_TILE = 2048
SAMPLE_SEQS = 8
GLA_SUB = 16
GLA_UNROLL = 16
GLA_SEQS = 2
DELTA_ROWS = 1024
DELTA_SEQS = 8
Q_BLOCK = 256
K_WINDOW = Q_BLOCK + BAND_CHUNKS * CHUNK
VMEM_LIMIT = 56 * 1024 * 1024

f32 = jnp.float32
bf16 = jnp.bfloat16


def _params(n_axes):
    return pltpu.CompilerParams(dimension_semantics=("arbitrary",) * n_axes, vmem_limit_bytes=VMEM_LIMIT)


def _dot(a, b):
    return jnp.dot(a, b, preferred_element_type=f32)


def _dot_nt(a, b):
    return lax.dot_general(a, b, (((1,), (1,)), ((), ())), preferred_element_type=f32)


def _dot_tn(a, b):
    return lax.dot_general(a, b, (((0,), (0,)), ((), ())), preferred_element_type=f32)


def _bmm(a, b):
    return jnp.einsum('nij,njk->nik', a, b, preferred_element_type=f32)


def _split(x):
    hi = x.astype(bf16)
    return hi, (x - hi.astype(f32)).astype(bf16)


def _dot_sel(x, sel):
    hi, lo = _split(x)
    return _dot(hi, sel) + _dot(lo, sel)


def _seg_sum(x, sel):
    return _dot(x.astype(bf16), sel)


def _bmm_sel(sel, x):
    hi, lo = _split(x)
    return _bmm(sel, hi) + _bmm(sel, lo)


def _iota(shape, dim):
    return lax.broadcasted_iota(jnp.int32, shape, dim)


def _seg_ones(n_in, seg_in, n_out, seg_out):
    return (_iota((n_in, n_out), 0) // seg_in == _iota((n_in, n_out), 1) // seg_out).astype(bf16)


def _silu(x):
    return x * jax.nn.sigmoid(x)


def _rms(x):
    return x * lax.rsqrt(jnp.mean(x * x, axis=-1, keepdims=True) + EPS)


def _inproj_kernel(x_ref, g_ref, w_ref, o_ref):
    h = _rms(x_ref[...]) * g_ref[...]
    o_ref[...] = _dot(h.astype(bf16), w_ref[...])


def _inproj(x, g, w, l):
    n = x.shape[0]
    return pl.pallas_call(
        _inproj_kernel,
        out_shape=jax.ShapeDtypeStruct((n, P_COLS), f32),
        grid=(n // ROW_TILE,),
        in_specs=[
            pl.BlockSpec((ROW_TILE, D_MODEL), lambda i: (i, 0)),
            pl.BlockSpec((None, 1, D_MODEL), lambda i: (l, 0, 0)),
            pl.BlockSpec((None, D_MODEL, P_COLS), lambda i: (l, 0, 0), pipeline_mode=pl.Buffered(1)),
        ],
        out_specs=pl.BlockSpec((ROW_TILE, P_COLS), lambda i: (i, 0)),
        compiler_params=_params(1),
        name="inproj",
    )(x, g, w)


def _outmlp_kernel(x_ref, ya_ref, yb_ref, yc_ref, yd_ref, wo_ref, g2_ref, wu_ref, wd_ref, gf_ref,
                   o_ref, *, final):
    acc = None
    for k, y_ref in enumerate((ya_ref, yb_ref, yc_ref, yd_ref)):
        d = _dot(y_ref[...].astype(bf16), wo_ref[k * 256:(k + 1) * 256, :])
        acc = d if acc is None else acc + d
    x = x_ref[...] + acc
    h2 = (_rms(x) * g2_ref[...]).astype(bf16)
    acc = None
    for c in range(D_FF // FF_CHUNK):
        up = _dot(h2, wu_ref[:, c * FF_CHUNK:(c + 1) * FF_CHUNK])
        up = jnp.square(jnp.maximum(up, 0.0)).astype(bf16)
        d = _dot(up, wd_ref[c * FF_CHUNK:(c + 1) * FF_CHUNK, :])
        acc = d if acc is None else acc + d
    x = x + acc
    if final:
        x = _rms(x) * gf_ref[...]
    o_ref[...] = x


def _outmlp(x, ys, wo, g2, wu, wd, gf, l, final):
    n = x.shape[0]
    row = lambda i: (i, 0)
    layer = lambda i: (l, 0, 0)
    single = pl.Buffered(1)
    return pl.pallas_call(
        functools.partial(_outmlp_kernel, final=final),
        out_shape=jax.ShapeDtypeStruct((n, D_MODEL), f32),
        grid=(n // ROW_TILE,),
        in_specs=[pl.BlockSpec((ROW_TILE, D_MODEL), row)]
        + [pl.BlockSpec((ROW_TILE, 256), row)] * 4
        + [pl.BlockSpec((None, D_MODEL, D_MODEL), layer, pipeline_mode=single),
           pl.BlockSpec((None, 1, D_MODEL), layer),
           pl.BlockSpec((None, D_MODEL, D_FF), layer, pipeline_mode=single),
           pl.BlockSpec((None, D_FF, D_MODEL), layer, pipeline_mode=single),
           pl.BlockSpec((1, D_MODEL), lambda i: (0, 0))],
        out_specs=pl.BlockSpec((ROW_TILE, D_MODEL), row),
        compiler_params=_params(1),
        name="outmlp_final" if final else "outmlp",
    )(x, *ys, wo, g2, wu, wd, gf)


def _pool_kernel(u_ref, hist_ref, w_ref, scale_ref, y_ref, st_ref, ext_ref, *, ns, tl, pos0):
    t_idx = pl.program_id(1)

    @pl.when(t_idx == 0)
    def _():
        ext_ref[:, 0:8, :] = jnp.zeros((ns, 8, W_A), f32)
        ext_ref[:, 1:16, :] = hist_ref[...]

    x = u_ref[...]
    ext_ref[:, 16:16 + tl, :] = x
    wsum = {}
    acc = x
    for k in range(1, 16):
        acc = acc + ext_ref[:, 16 - k:16 - k + tl, :]
        if k + 1 in POOL_WINDOWS:
            wsum[k + 1] = acc
    ext_ref[:, 1:16, :] = ext_ref[:, tl + 1:tl + 16, :]
    lane_group = _iota((ns, tl, W_A), 2) // POOL_GW
    pos = pos0 + t_idx * tl + _iota((ns, tl, W_A), 1)
    ws, win = wsum[POOL_WINDOWS[-1]], jnp.full((ns, tl, W_A), POOL_WINDOWS[-1], jnp.int32)
    for gi in range(N_POOL - 2, -1, -1):
        ws = jnp.where(lane_group == gi, wsum[POOL_WINDOWS[gi]], ws)
        win = jnp.where(lane_group == gi, POOL_WINDOWS[gi], win)
    cnt = jnp.minimum(win, pos + 1).astype(f32)
    pooled = (ws / cnt - x).reshape(ns * tl, W_A)
    y_ref[...] = (_dot(pooled, w_ref[...]) * scale_ref[...]).reshape(ns, tl, W_A)

    @pl.when(t_idx == pl.num_programs(1) - 1)
    def _():
        st_ref[...] = ext_ref[:, 1:16, :]


def _pool_call(p, b, t, pos0, hist, lh, wbd, scale, l):
    tl = min(t, POOL_TILE)
    ns = min(b, POOL_TILE // tl)
    nt = t // tl
    y, st = pl.pallas_call(
        functools.partial(_pool_kernel, ns=ns, tl=tl, pos0=pos0),
        out_shape=(jax.ShapeDtypeStruct((b, t, W_A), f32),
                   jax.ShapeDtypeStruct((b, POOL_HIST, W_A), f32)),
        grid=(b // ns, nt),
        in_specs=[
            pl.BlockSpec((ns, tl, W_A), lambda i, j: (i, j, P_UA // W_A)),
            pl.BlockSpec((None, ns, POOL_HIST, W_A), lambda i, j: (lh, i, 0, 0)),
            pl.BlockSpec((None, W_A, W_A), lambda i, j: (l, 0, 0)),
            pl.BlockSpec((None, 1, W_A), lambda i, j: (l, 0, 0)),
        ],
        out_specs=(pl.BlockSpec((ns, tl, W_A), lambda i, j: (i, j, 0)),
                   pl.BlockSpec((ns, POOL_HIST, W_A), lambda i, j: (i, 0, 0))),
        scratch_shapes=[pltpu.VMEM((ns, tl + 16, W_A), f32)],
        compiler_params=_params(2),
        name=f"pool_t{t}",
    )(p.reshape(b, t, P_COLS), hist, wbd, scale)
    return y.reshape(b * t, W_A), st


LOG2E = 1.4426950408889634
Q_SCALE = DH_B ** -0.5 * LOG2E


def _softmax_av(s_parts, av_fns):
    m = functools.reduce(jnp.maximum, [jnp.max(s, axis=-1, keepdims=True) for s in s_parts])
    e_parts = [jnp.exp2(s - m) for s in s_parts]
    l = functools.reduce(jnp.add, [jnp.sum(e, axis=-1, keepdims=True) for e in e_parts])
    o = functools.reduce(jnp.add, [av(e) for e, av in zip(e_parts, av_fns)])
    return o / l


BIAS_PERIOD = 2 * K_WINDOW


def _band_attn_kernel(q_ref, k_ref, v_ref, wrow_ref, y_ref, kc_ref, vc_ref, bias_ref, *, t, keep):
    qb = pl.program_id(1)

    @pl.when((pl.program_id(0) == 0) & (qb == 0))
    def _():
        qi = _iota((Q_BLOCK, K_WINDOW), 0)
        kj = _iota((Q_BLOCK, K_WINDOW), 1)
        for h in range(H_B):
            base = pltpu.roll(jnp.broadcast_to(wrow_ref[h], (Q_BLOCK, BIAS_PERIOD)), 0, 1,
                              stride=1, stride_axis=0)
            for v in range(3):
                dchunk = (qi + v * Q_BLOCK) // CHUNK - kj // CHUNK
                vis = (dchunk >= 0) & (dchunk <= BAND_CHUNKS)
                lo = (2 - v) * Q_BLOCK
                bias_ref[v, h] = jnp.where(vis, base[:, lo:lo + K_WINDOW], NEG_INF)

    start = pl.multiple_of(jnp.maximum(qb * Q_BLOCK - BAND_CHUNKS * CHUNK, 0), Q_BLOCK)
    variant = jnp.minimum(qb, 2)
    outs = []
    for h in range(H_B):
        lanes = slice(h * DH_B, (h + 1) * DH_B)
        q = q_ref[:, lanes] * Q_SCALE
        k = k_ref[pl.ds(start, K_WINDOW), lanes]
        v = v_ref[pl.ds(start, K_WINDOW), lanes]
        s = _dot_nt(q, k) + bias_ref[variant, h]
        outs.append(_softmax_av([s], [lambda e, v=v: _dot(e, v)]))
    y_ref[...] = jnp.concatenate(outs, axis=1)

    @pl.when(qb == pl.num_programs(1) - 1)
    def _():
        for h in range(H_B):
            lanes = slice(h * DH_B, (h + 1) * DH_B)
            kc_ref[h] = k_ref[t - keep:t, lanes]
            vc_ref[h] = v_ref[t - keep:t, lanes]


def _rel_bias_rows(table, nq, nk, off):
    period = nq + nk
    m = np.arange(period)
    rel = np.where(m < nk, m, m - period) - off
    row = table.astype(f32)[..., np.clip(rel, -REL_CLIP, REL_CLIP) + REL_CLIP]
    lead = table.shape[:-1]
    flat = jnp.tile(row, (1,) * len(lead) + (nq,))[..., :nq * (period - 1)]
    return flat.reshape(lead + (nq, period - 1))[..., :nk]


def _band_bias_row(table):
    m = np.arange(BIAS_PERIOD)
    e = np.where(m < 2 * Q_BLOCK + K_WINDOW, m, m - BIAS_PERIOD)
    idx = np.clip(e - 2 * Q_BLOCK, -REL_CLIP, REL_CLIP) + REL_CLIP
    return table.astype(f32)[..., idx][:, :, None, :]


def _band_attn_call(p, b, t, wrow, l):
    nq = t // Q_BLOCK
    keep = min(ATTN_WINDOW, t)
    cache = jax.ShapeDtypeStruct((b, H_B, keep, DH_B), f32)
    cache_spec = pl.BlockSpec((None, H_B, keep, DH_B), lambda i, j: (i, 0, 0, 0))
    return pl.pallas_call(
        functools.partial(_band_attn_kernel, t=t, keep=keep),
        out_shape=(jax.ShapeDtypeStruct((b * t, W_B), f32), cache, cache),
        grid=(b, nq),
        in_specs=[
            pl.BlockSpec((Q_BLOCK, W_B), lambda i, j: (i * nq + j, P_QB // W_B)),
            pl.BlockSpec((t, W_B), lambda i, j: (i, P_KB // W_B)),
            pl.BlockSpec((t, W_B), lambda i, j: (i, P_VB // W_B)),
            pl.BlockSpec((None, H_B, 1, BIAS_PERIOD), lambda i, j: (l, 0, 0, 0)),
        ],
        out_specs=(pl.BlockSpec((Q_BLOCK, W_B), lambda i, j: (i * nq + j, 0)), cache_spec, cache_spec),
        scratch_shapes=[pltpu.VMEM((3, H_B, Q_BLOCK, K_WINDOW), f32)],
        compiler_params=_params(2),
        name="band_attn",
    )(p, p, p, wrow)


def _sample_attn_kernel(q_ref, k_ref, v_ref, kct_ref, vct_ref, bias_ref, y_ref, kn_ref, vn_ref,
                        *, ns, n_cache):
    def one_sequence(s, carry):
        outs = []
        for h in range(H_B):
            lanes = slice(h * DH_B, (h + 1) * DH_B)
            q = q_ref[s, :, lanes] * Q_SCALE
            k_new, v_new = k_ref[s, :, lanes], v_ref[s, :, lanes]
            s_old = _dot(q, kct_ref[s, h]) + bias_ref[h, :, 0:n_cache]
            s_new = _dot_nt(q, k_new) + bias_ref[h, :, n_cache:]
            outs.append(_softmax_av(
                [s_old, s_new],
                [lambda e, h=h: _dot_nt(e, vct_ref[s, h]), lambda e, v=v_new: _dot(e, v)]))
            kn_ref[s, h] = k_new
            vn_ref[s, h] = v_new
        y_ref[s] = jnp.concatenate(outs, axis=1)
        return carry

    lax.fori_loop(0, ns, one_sequence, 0, unroll=4)


def _sample_attn_call(p, b, t, k_cache_t, v_cache_t, bias, l):
    n_cache = k_cache_t.shape[4]
    ns = min(b, SAMPLE_SEQS)
    cache_spec = pl.BlockSpec((None, ns, H_B, DH_B, n_cache), lambda i: (l, i, 0, 0, 0))
    new = jax.ShapeDtypeStruct((b, H_B, t, DH_B), f32)
    new_spec = pl.BlockSpec((ns, H_B, t, DH_B), lambda i: (i, 0, 0, 0))
    p3 = p.reshape(b, t, P_COLS)
    y, k_new, v_new = pl.pallas_call(
        functools.partial(_sample_attn_kernel, ns=ns, n_cache=n_cache),
        out_shape=(jax.ShapeDtypeStruct((b, t, W_B), f32), new, new),
        grid=(b // ns,),
        in_specs=[
            pl.BlockSpec((ns, t, W_B), lambda i: (i, 0, P_QB // W_B)),
            pl.BlockSpec((ns, t, W_B), lambda i: (i, 0, P_KB // W_B)),
            pl.BlockSpec((ns, t, W_B), lambda i: (i, 0, P_VB // W_B)),
            cache_spec, cache_spec,
            pl.BlockSpec((None, H_B, t, n_cache + t), lambda i: (l, 0, 0, 0)),
        ],
        out_specs=(pl.BlockSpec((ns, t, W_B), lambda i: (i, 0, 0)), new_spec, new_spec),
        compiler_params=_params(1),
        name="sample_attn",
    )(p3, p3, p3, k_cache_t, v_cache_t, bias)
    return y.reshape(b * t, W_B), k_new, v_new


def _delta_kernel(q_ref, k_ref, v_ref, z_ref, sm_ref, hist_ref, s0_ref, cw_ref, alog_ref, dtb_ref, ng_ref,
                  y_ref, sout_ref, cout_ref, ext_ref, s_ref, *, ns, tl, c, carry):
    t_idx = pl.program_id(1)
    tb = ns * tl
    nch = tb // c
    cw = H_C * c

    def chunks(x):
        return x.reshape(nch, c, x.shape[-1])

    def pack_cols(x3):
        if c == DK_C:
            return x3
        return jnp.concatenate([x3[:, :, h * DK_C:h * DK_C + c] for h in range(H_C)], axis=-1)

    def block_t(x3, width):
        return jnp.concatenate(
            [jnp.swapaxes(x3[:, :, h * width:(h + 1) * width], 1, 2) for h in range(H_C)], axis=-1)

    def head_mask(rows, row_w, cols, col_w):
        return (_iota((rows, cols), 0) // row_w == _iota((rows, cols), 1) // col_w)[None]

    def block_diag(y3, mask):
        return jnp.where(mask, jnp.concatenate([y3] * H_C, axis=1), 0.0)

    if carry:
        @pl.when(t_idx == 0)
        def _():
            s_ref[...] = jnp.concatenate([s0_ref[:, h] for h in range(H_C)], axis=-1)
            for s in range(3):
                ext_ref[s, :, 0:8, :] = jnp.zeros((ns, 8, 256), f32)
                ext_ref[s, :, 5:8, :] = hist_ref[:, :, s * 256:(s + 1) * 256]
    else:
        for s in range(3):
            ext_ref[s, :, 5:8, :] = hist_ref[:, :, s * 256:(s + 1) * 256]

    parts = []
    for s, ref in enumerate((q_ref, k_ref, v_ref)):
        ext_ref[s, :, 8:8 + tl, :] = ref[...]
        acc = ext_ref[s, :, 5:5 + tl, :] * cw_ref[0:1, s * 256:(s + 1) * 256]
        for j in range(1, CONV_W):
            acc = acc + ext_ref[s, :, 5 + j:5 + j + tl, :] * cw_ref[j:j + 1, s * 256:(s + 1) * 256]
        parts.append(_silu(acc).reshape(tb, 256))
        last = ext_ref[s, :, tl + 5:tl + 8, :]
        if carry:
            ext_ref[s, :, 5:8, :] = last
        cout_ref[:, :, s * 256:(s + 1) * 256] = last
    cq, ck, cv = parts
    seg = _seg_ones(256, DK_C, 256, DK_C)
    qn = cq * lax.rsqrt(_seg_sum(cq * cq, seg) + EPS) * (DK_C ** -0.5)
    kn = ck * lax.rsqrt(_seg_sum(ck * ck, seg) + EPS)

    sm = sm_ref[...].reshape(tb, SMALL_W)
    g = -jnp.exp(alog_ref[...]) * jax.nn.softplus(sm + dtb_ref[...])
    beta = jax.nn.sigmoid(sm)
    ltri = jnp.broadcast_to((_iota((c, c), 0) >= _iota((c, c), 1)).astype(bf16)[None], (nch, c, c))
    gcum = _bmm_sel(ltri, g.reshape(nch, c, SMALL_W)).reshape(tb, SMALL_W)
    lane_head = _iota((SMALL_W, 256), 1) // 64
    gb = _dot_sel(gcum, (_iota((SMALL_W, 256), 0) == lane_head).astype(bf16))
    betab = _seg_sum(beta, (_iota((SMALL_W, 256), 0) == lane_head + H_C).astype(bf16))
    eg = jnp.exp(gb)
    gl = jnp.broadcast_to(gb.reshape(nch, c, 256)[:, c - 1:c, :], (nch, c, 256)).reshape(tb, 256)

    m_k2c = head_mask(W_C, DK_C, cw, c)
    m_c2c = head_mask(cw, c, cw, c)
    m_c2v = head_mask(cw, c, W_C, DV_C)
    m_k2v = head_mask(W_C, DK_C, W_C, DV_C)
    kn3, qn3 = chunks(kn), chunks(qn)
    bd_kt = jnp.where(m_k2c, jnp.concatenate([jnp.swapaxes(kn3, 1, 2)] * H_C, axis=-1), 0.0)
    gcb = pack_cols(chunks(gb))
    grb = block_t(gcb, c)
    rp, cp = _iota((c, cw), 0), _iota((c, cw), 1) % c
    incl_p = (rp >= cp)[None]
    dec = jnp.where(incl_p, jnp.exp(jnp.where(incl_p, gcb - grb, 0.0)), 0.0)
    a = jnp.where((rp > cp)[None], pack_cols(chunks(betab)) * _bmm(kn3, bd_kt) * dec, 0.0)
    d = jnp.where(rp == cp, 1.0, 0.0)[None] - jnp.where(((rp // 2 == cp // 2) & (rp > cp))[None], a, 0.0)
    s = 2
    while s < c:
        lower_left = (rp // (2 * s) == cp // (2 * s)) & (rp % (2 * s) >= s) & (cp % (2 * s) < s)
        e = jnp.where(lower_left[None], a, 0.0)
        d = d - _bmm(d, block_diag(_bmm(e, block_diag(d, m_c2c)), m_c2c))
        s *= 2
    u = _bmm(d, block_diag(chunks(betab * cv), m_c2v))
    w = _bmm(d, block_diag(chunks(betab * eg * kn), m_c2v))
    qk = _bmm(qn3, bd_kt) * dec
    qt3 = chunks(qn * eg)
    ktt = block_t(chunks(kn * jnp.exp(gl - gb)), DK_C)
    egl = chunks(jnp.exp(gl))[:, 0:1, :]

    if carry:
        o_chunks = [None] * nch
        for s in range(ns):
            st = s_ref[s]
            for ic in range(s * (tl // c), (s + 1) * (tl // c)):
                bds = jnp.where(m_k2v[0], jnp.concatenate([st] * H_C, axis=0), 0.0)
                un = u[ic] - _dot(w[ic], bds)
                bdun = jnp.where(m_c2v[0], jnp.concatenate([un] * H_C, axis=0), 0.0)
                o_chunks[ic] = _dot(qt3[ic], bds) + _dot(qk[ic], bdun)
                st = egl[ic] * st + _dot(ktt[ic], bdun)
            s_ref[s] = st
        o = jnp.concatenate(o_chunks, axis=0)

        @pl.when(t_idx == pl.num_programs(1) - 1)
        def _():
            for h in range(H_C):
                sout_ref[:, h] = s_ref[:, :, h * DV_C:(h + 1) * DV_C]
    else:
        st = jnp.concatenate([s0_ref[:, h] for h in range(H_C)], axis=-1)
        bds = block_diag(st, m_k2v)
        un = u - _bmm(w, bds)
        bdun = block_diag(un, m_c2v)
        o = (_bmm(qt3, bds) + _bmm(qk, bdun)).reshape(tb, W_C)
        st = egl * st + _bmm(ktt, bdun)
        for h in range(H_C):
            sout_ref[:, h] = st[:, :, h * DV_C:(h + 1) * DV_C]

    o = o * lax.rsqrt(_seg_sum(o * o, seg) * (1.0 / DV_C) + EPS) * ng_ref[...]
    y_ref[...] = (o * _silu(z_ref[...].reshape(tb, W_C))).reshape(ns, tl, W_C)


def _delta_call(p, b, t, hist, s0, lh, conv_w, alog, dtb, ng, l):
    c = min(CHUNK, t)
    carry = t > c
    ns = min(b, DELTA_SEQS) if carry else min(b, DELTA_ROWS // (4 * t))
    tl = min(t, DELTA_ROWS // ns) if carry else t
    nt = t // tl
    p3 = p.reshape(b, t, P_COLS)

    def col(cb):
        return lambda i, j: (i, j, cb)

    layer = lambda i, j: (l, 0, 0)
    state_spec = pl.BlockSpec((ns, H_C, DK_C, DV_C), lambda i, j: (i, 0, 0, 0))
    y, s_new, conv_new = pl.pallas_call(
        functools.partial(_delta_kernel, ns=ns, tl=tl, c=c, carry=carry),
        out_shape=(jax.ShapeDtypeStruct((b, t, W_C), f32),
                   jax.ShapeDtypeStruct((b, H_C, DK_C, DV_C), f32),
                   jax.ShapeDtypeStruct((b, CONV_W - 1, QKV_C), f32)),
        grid=(b // ns, nt),
        in_specs=[
            pl.BlockSpec((ns, tl, 256), col(P_QKVC // 256)),
            pl.BlockSpec((ns, tl, 256), col(P_QKVC // 256 + 1)),
            pl.BlockSpec((ns, tl, 256), col(P_QKVC // 256 + 2)),
            pl.BlockSpec((ns, tl, 256), col(P_ZC // 256)),
            pl.BlockSpec((ns, tl, SMALL_W), col(P_SMALL // SMALL_W)),
            pl.BlockSpec((None, ns, CONV_W - 1, QKV_C), lambda i, j: (lh, i, 0, 0)),
            pl.BlockSpec((None, ns, H_C, DK_C, DV_C), lambda i, j: (lh, i, 0, 0, 0)),
            pl.BlockSpec((None, CONV_W, QKV_C), layer),
            pl.BlockSpec((None, 1, SMALL_W), layer),
            pl.BlockSpec((None, 1, SMALL_W), layer),
            pl.BlockSpec((None, 1, W_C), layer),
        ],
        out_specs=(pl.BlockSpec((ns, tl, W_C), lambda i, j: (i, j, 0)),
                   state_spec,
                   pl.BlockSpec((ns, CONV_W - 1, QKV_C), lambda i, j: (i, 0, 0))),
        scratch_shapes=[
            pltpu.VMEM((3, ns, tl + 8, 256), f32),
            pltpu.VMEM((ns, DK_C, W_C), f32),
        ],
        compiler_params=_params(2),
        name=f"delta_t{t}",
    )(p3, p3, p3, p3, p3, hist, s0, conv_w, alog, dtb, ng)
    return y.reshape(b * t, W_C), s_new, conv_new


def _gla_kernel(q_ref, k_ref, v_ref, gate_ref, sm_ref, s0_ref, wgk_ref, bgk_ref, ng_ref,
                y_ref, sout_ref, qs_ref, g_ref, o_ref, s_ref, *, ns, tl, c, carry):
    t_idx = pl.program_id(1)
    tb = ns * tl
    blk = _iota((WK_D, W_D), 0) // DK_D == _iota((WK_D, W_D), 1) // DV_D
    seg_kv = blk.astype(bf16)

    def block_diag(s4):
        zero = jnp.zeros((DK_D, DV_D), f32)
        return jnp.concatenate(
            [jnp.concatenate([s4[h] if h2 == h else zero for h2 in range(H_D)], axis=1)
             for h in range(H_D)], axis=0)

    if carry:
        @pl.when(t_idx == 0)
        def _():
            for s in range(ns):
                s_ref[s] = block_diag(s0_ref[s])

    gk = jax.nn.log_sigmoid(_dot(sm_ref[...].reshape(tb, SMALL_W), wgk_ref[...]) + bgk_ref[...])
    g_ref[...] = gk * (1.0 / GLA_GATE_NORM)
    qs_ref[...] = q_ref[...].reshape(tb, WK_D) * (DK_D ** -0.5)
    ltri = (_iota((c, c), 0) >= _iota((c, c), 1)).astype(bf16)
    row8 = _iota((8, WK_D), 0)
    rowid = _iota((c, WK_D), 0)
    lane_head_k = _iota((GLA_SUB, WK_D), 1) // DK_D
    lane_head_v = _iota((GLA_SUB, W_D), 1) // DV_D

    def one_chunk(row0, k, v, st):
        rows = pl.ds(pl.multiple_of(row0, c), c)
        g_hi, g_lo = _split(g_ref[rows, :])
        g = _dot(ltri, g_hi) + _dot(ltri, g_lo)
        q = qs_ref[rows, :]
        o_state = _dot(q * jnp.exp(g), st)
        o8 = [None] * (c // 8)
        for jb in range(c // 8):
            lo = 8 * jb
            hi = GLA_SUB * (lo // GLA_SUB + 1)
            m = hi - lo
            e = []
            for j in range(lo, lo + 8):
                ej = jnp.exp(g[lo:hi] - g[j:j + 1]) * (q[lo:hi] * k[j:j + 1])
                first = jnp.where(row8 >= j - lo, ej[:8], 0.0)
                e.append(first if m == 8 else jnp.concatenate([first, ej[8:]], axis=0))
            att = _dot(jnp.concatenate(e, axis=0).astype(bf16), seg_kv)
            upd = att[0:m] * v[lo:lo + 1]
            for jj in range(1, 8):
                upd = upd + att[jj * m:(jj + 1) * m] * v[lo + jj:lo + jj + 1]
            for r in range(m // 8):
                piece = upd[8 * r:8 * (r + 1)]
                o8[jb + r] = piece if o8[jb + r] is None else o8[jb + r] + piece
        for sb in range(1, c // GLA_SUB):
            r0 = sb * GLA_SUB
            ref = g[r0 - 1:r0]
            qx = q[r0:r0 + GLA_SUB] * jnp.exp(g[r0:r0 + GLA_SUB] - ref)
            kx = jnp.where(rowid < r0, k * jnp.exp(jnp.minimum(ref - g, 0.0)), 0.0)
            qh = jnp.concatenate([jnp.where(lane_head_k == h, qx, 0.0) for h in range(H_D)], axis=0)
            res = _dot(_dot_nt(qh, kx), v)
            upd = None
            for h in range(H_D):
                part = jnp.where(lane_head_v == h, res[h * GLA_SUB:(h + 1) * GLA_SUB], 0.0)
                upd = part if upd is None else upd + part
            for r in range(GLA_SUB // 8):
                o8[r0 // 8 + r] = o8[r0 // 8 + r] + upd[8 * r:8 * (r + 1)]
        o_ref[rows, :] = o_state + jnp.concatenate(o8, axis=0)
        gl = g[c - 1:c]
        egl_col = jnp.exp(g.T[:, c - 1:c])
        return jnp.where(blk, egl_col * st + _dot_tn(k * jnp.exp(gl - g), v), 0.0)

    if carry:
        def step(ic, states):
            r = pl.multiple_of(ic * c, c)
            return tuple(one_chunk(s * tl + r, k_ref[s, pl.ds(r, c), :], v_ref[s, pl.ds(r, c), :], states[s])
                         for s in range(ns))

        s_last = lax.fori_loop(0, tl // c, step, tuple(s_ref[s] for s in range(ns)),
                               unroll=min(GLA_UNROLL // ns, tl // c))
    else:
        def step(ic, carry_):
            s_new = one_chunk(ic * c, k_ref[ic], v_ref[ic], block_diag(s0_ref[ic]))
            for h in range(H_D):
                sout_ref[ic, h] = s_new[h * DK_D:(h + 1) * DK_D, h * DV_D:(h + 1) * DV_D]
            return carry_

        lax.fori_loop(0, ns, step, 0, unroll=min(GLA_UNROLL, ns))

    o = o_ref[...]
    o = o * lax.rsqrt(_seg_sum(o * o, _seg_ones(W_D, DV_D, W_D, DV_D)) * (1.0 / DV_D) + EPS) * ng_ref[...]
    y_ref[...] = (o * _silu(gate_ref[...].reshape(tb, W_D))).reshape(ns, tl, W_D)

    if carry:
        for s in range(ns):
            s_ref[s] = s_last[s]

        @pl.when(t_idx == pl.num_programs(1) - 1)
        def _():
            for s in range(ns):
                for h in range(H_D):
                    sout_ref[s, h] = s_ref[s, h * DK_D:(h + 1) * DK_D, h * DV_D:(h + 1) * DV_D]


def _gla_call(p, b, t, s0, lh, wgk, bgk, ng, l):
    c = min(CHUNK, t)
    carry = t > c
    ns, tl = (min(b, GLA_SEQS), min(t, ROW_TILE)) if carry else (min(b, SAMPLE_SEQS), t)
    tb = ns * tl
    nt = t // tl
    p3 = p.reshape(b, t, P_COLS)

    def col(cb):
        return lambda i, j: (i, j, cb)

    layer = lambda i, j: (l, 0, 0)
    y, s_new = pl.pallas_call(
        functools.partial(_gla_kernel, ns=ns, tl=tl, c=c, carry=carry),
        out_shape=(jax.ShapeDtypeStruct((b, t, W_D), f32),
                   jax.ShapeDtypeStruct((b, H_D, DK_D, DV_D), f32)),
        grid=(b // ns, nt),
        in_specs=[
            pl.BlockSpec((ns, tl, WK_D), col(P_QD // WK_D)),
            pl.BlockSpec((ns, tl, WK_D), col(P_KD // WK_D)),
            pl.BlockSpec((ns, tl, W_D), col(P_VD // W_D)),
            pl.BlockSpec((ns, tl, W_D), col(P_GD // W_D)),
            pl.BlockSpec((ns, tl, SMALL_W), col(P_SMALL // SMALL_W)),
            pl.BlockSpec((None, ns, H_D, DK_D, DV_D), lambda i, j: (lh, i, 0, 0, 0)),
            pl.BlockSpec((None, SMALL_W, WK_D), layer),
            pl.BlockSpec((None, 1, WK_D), layer),
            pl.BlockSpec((None, 1, W_D), layer),
        ],
        out_specs=(pl.BlockSpec((ns, tl, W_D), lambda i, j: (i, j, 0)),
                   pl.BlockSpec((ns, H_D, DK_D, DV_D), lambda i, j: (i, 0, 0, 0))),
        scratch_shapes=[
            pltpu.VMEM((tb, WK_D), f32), pltpu.VMEM((tb, WK_D), f32),
            pltpu.VMEM((tb, W_D), f32),
            pltpu.VMEM((ns, WK_D, W_D), f32),
        ],
        compiler_params=_params(2),
        name=f"gla_t{t}",
    )(p3, p3, p3, p3, p3, s0, wgk, bgk, ng)
    return y.reshape(b * t, W_D), s_new


def _reorder_w_in(w_in):
    pad = jnp.zeros(w_in.shape[:-1] + (P_COLS - P_SMALL - 2 * H_C - GLA_RANK,), w_in.dtype)
    return jnp.concatenate([
        w_in[..., 0:2048],
        w_in[..., 2312:2568],
        w_in[..., 2568:2824],
        w_in[..., 2056:2184],
        w_in[..., 2184:2312],
        w_in[..., 2048:2056],
        w_in[..., 2824:2840],
        pad], axis=-1)


def _lane_pad(x, width):
    return jnp.pad(x, ((0, 0), (0, width - x.shape[-1])))[:, None, :]


def kernel(x_prompt, x_sample, cache_pool, cache_attn_k, cache_attn_v, state_conv, state_delta,
           state_gla, attn_norm_g, w_in, pool_w, pool_scale, rel_bias, conv_w, a_log, dt_bias,
           delta_norm_g, gla_w_gk, gla_b_gk, gla_norm_g, w_out, mlp_norm_g, w_up, w_down,
           final_norm_g):
    bp, tp, _ = x_prompt.shape
    bs, ts, _ = x_sample.shape
    w_in_r = _reorder_w_in(w_in.astype(bf16))
    w_out_b, w_up_b, w_down_b = w_out.astype(bf16), w_up.astype(bf16), w_down.astype(bf16)
    g1 = attn_norm_g[:, None, :]
    g2 = mlp_norm_g[:, None, :]
    gf = final_norm_g.reshape(1, D_MODEL)
    pool_wbd = jnp.zeros((DEPTH, W_A, W_A), f32)
    for gi in range(N_POOL):
        sl = slice(gi * POOL_GW, (gi + 1) * POOL_GW)
        pool_wbd = pool_wbd.at[:, sl, sl].set(pool_w[:, gi])
    pool_sc = pool_scale[:, None, :]
    band_bias = _band_bias_row(rel_bias) * LOG2E
    n_cache = cache_attn_k.shape[3]
    sample_bias = _rel_bias_rows(rel_bias, ts, n_cache + ts, n_cache) * LOG2E
    cache_kt, cache_vt = jnp.swapaxes(cache_attn_k, 3, 4), jnp.swapaxes(cache_attn_v, 3, 4)
    alog = _lane_pad(a_log, SMALL_W)
    dtb = _lane_pad(dt_bias, SMALL_W)
    ng_c = jnp.tile(delta_norm_g, (1, H_C))[:, None, :]
    wgk = jnp.zeros((DEPTH, SMALL_W, WK_D), f32).at[:, 2 * H_C:2 * H_C + GLA_RANK].set(gla_w_gk)
    bgk = gla_b_gk[:, None, :]
    ng_d = jnp.tile(gla_norm_g, (1, H_D))[:, None, :]
    zeros = (jnp.zeros((1, bp, POOL_HIST, W_A), f32), jnp.zeros((1, bp, CONV_W - 1, QKV_C), f32),
             jnp.zeros((1, bp, H_C, DK_C, DV_C), f32), jnp.zeros((1, bp, H_D, DK_D, DV_D), f32))
    carried = (cache_pool, state_conv, state_delta, state_gla)

    xs = [x_prompt.reshape(bp * tp, D_MODEL), x_sample.reshape(bs * ts, D_MODEL)]
    states = ([], [])
    for l in range(DEPTH):
        for s, (b, t, pos0) in enumerate(((bp, tp, 0), (bs, ts, tp))):
            pool_h, conv_h, sd_h, sg_h = zeros if s == 0 else carried
            lh = 0 if s == 0 else l
            p = _inproj(xs[s], g1, w_in_r, l)
            y_a, st_pool = _pool_call(p, b, t, pos0, pool_h, lh, pool_wbd, pool_sc, l)
            if s == 0:
                y_b, st_k, st_v = _band_attn_call(p, b, t, band_bias, l)
            else:
                y_b, st_k, st_v = _sample_attn_call(p, b, t, cache_kt, cache_vt, sample_bias, l)
            y_c, st_d, st_conv = _delta_call(p, b, t, conv_h, sd_h, lh, conv_w, alog, dtb, ng_c, l)
            y_d, st_g = _gla_call(p, b, t, sg_h, lh, wgk, bgk, ng_d, l)
            xs[s] = _outmlp(xs[s], (y_a, y_b, y_c, y_d), w_out_b, g2, w_up_b, w_down_b, gf, l,
                            final=(l == DEPTH - 1))
            states[s].append((st_pool, st_k, st_v, st_conv, st_d, st_g))
    y_prompt = xs[0].reshape(bp, tp, D_MODEL)
    y_sample = xs[1].reshape(bs, ts, D_MODEL)
    outs_p = tuple(jnp.stack([st[i] for st in states[0]]) for i in range(6))
    outs_s = tuple(jnp.stack([st[i] for st in states[1]]) for i in range(6))
    return (y_prompt, y_sample) + outs_p + outs_s
```
